```python
import math
import jax, jax.numpy as jnp
from jax import lax
import numpy as np

D_MODEL = 1024
BATCH = 16
SEQ = 2048
DEPTH = 2

A_HEADS = 8
A_HEAD_DIM = 64
A_WIDTH = A_HEADS * A_HEAD_DIM
IDX_HEADS = 8
IDX_DIM = 64
TOPK_MAX = 256
Q_BLOCK = 128
B_GROUPS = 4
B_GROUP_DIM = 128
B_WIDTH = B_GROUPS * B_GROUP_DIM
CHUNK = 128
REL_BUCKETS = 32
REL_MAX_DIST = 128
N_BRANCH = 2
EPS = 1e-6

SPLIT_SIZES = (A_WIDTH, A_HEAD_DIM, A_HEAD_DIM, A_WIDTH,
               IDX_HEADS * IDX_DIM, IDX_DIM, IDX_HEADS,
               B_WIDTH, B_WIDTH, B_WIDTH,
               D_MODEL, D_MODEL)
N_IN = 5320

kernel_name = "hybrid_dsa_gmlp_gated_parallel"


def rms_norm(x, g):
    xf = x.astype(jnp.float32)
    y = xf * lax.rsqrt(jnp.mean(xf * xf, axis=-1, keepdims=True) + EPS)
    return (y * g.astype(jnp.float32)).astype(x.dtype)


def layer_norm(x, g, b):
    xf = x.astype(jnp.float32)
    mu = jnp.mean(xf, axis=-1, keepdims=True)
    var = jnp.mean(jnp.square(xf - mu), axis=-1, keepdims=True)
    y = (xf - mu) * lax.rsqrt(var + EPS)
    return (y * g.astype(jnp.float32) + b.astype(jnp.float32)).astype(x.dtype)


def split_offsets():
    offs, acc = [], 0
    for s in SPLIT_SIZES[:-1]:
        acc += s
        offs.append(acc)
    return offs


def t5_bucket(rel):
    max_exact = REL_BUCKETS // 2
    is_small = rel < max_exact
    nf = jnp.maximum(rel, 1).astype(jnp.float32)
    large = max_exact + (jnp.log(nf / max_exact) / math.log(REL_MAX_DIST / max_exact)
                         * (REL_BUCKETS - max_exact)).astype(jnp.int32)
    large = jnp.minimum(large, REL_BUCKETS - 1)
    return jnp.where(is_small, rel, large)


def dsa_attention(q, k, v, q_idx, k_idx, w_idx, rel_bias):
    B, S = q.shape[0], q.shape[1]
    k_top = min(TOPK_MAX, S // 4)
    n_blk = S // Q_BLOCK
    scale = A_HEAD_DIM ** -0.5
    key_pos = jnp.arange(S, dtype=jnp.int32)
    w_idx = w_idx * (IDX_HEADS ** -0.5 * IDX_DIM ** -0.5)

    def block(i):
        start = i * Q_BLOCK
        qb = lax.dynamic_slice_in_dim(q, start, Q_BLOCK, axis=1)
        qib = lax.dynamic_slice_in_dim(q_idx, start, Q_BLOCK, axis=1)
        wb = lax.dynamic_slice_in_dim(w_idx, start, Q_BLOCK, axis=1)
        qpos = start + jnp.arange(Q_BLOCK, dtype=jnp.int32)
        sc = jnp.einsum('bqhd,bsd->bqhs', qib, k_idx)
        sc = jnp.einsum('bqhs,bqh->bqs', jax.nn.relu(sc), wb).astype(jnp.float32)
        causal = key_pos[None, :] <= qpos[:, None]
        sc = jnp.where(causal[None], sc, -jnp.inf)
        _, idx = lax.top_k(sc, k_top)
        valid = idx <= qpos[None, :, None]
        k_sel = jax.vmap(lambda kb, ib: kb[ib])(k, idx)
        v_sel = jax.vmap(lambda vb, ib: vb[ib])(v, idx)
        logits = jnp.einsum('bqhd,bqkd->bqhk', qb, k_sel).astype(jnp.float32) * scale
        rel = jnp.maximum(qpos[None, :, None] - idx, 0)
        bias = rel_bias[t5_bucket(rel)]
        logits = logits + jnp.moveaxis(bias, -1, 2).astype(jnp.float32)
        logits = jnp.where(valid[:, :, None, :], logits, -jnp.inf)
        p = jax.nn.softmax(logits, axis=-1).astype(v.dtype)
        return jnp.einsum('bqhk,bqkd->bqhd', p, v_sel)

    out = lax.map(block, jnp.arange(n_blk, dtype=jnp.int32))
    return jnp.moveaxis(out, 0, 1).reshape(B, S, A_WIDTH)


def chunked_sgu(u, v, ln_g, ln_b, w_s, b_s):
    B, S = u.shape[0], u.shape[1]
    v = layer_norm(v, ln_g, ln_b)
    vc = v.reshape(B, S // CHUNK, CHUNK, B_GROUPS, B_GROUP_DIM)
    mask = jnp.tril(jnp.ones((CHUNK, CHUNK), dtype=bool))
    w = jnp.where(mask, w_s, 0)
    s = jnp.einsum('gts,bnsgc->bntgc', w, vc) + b_s.T[None, None, :, :, None]
    return u * s.reshape(B, S, B_WIDTH)


def setup_inputs(seed: int = 0) -> dict:
    key = jax.random.key(seed)
    ks = jax.random.split(key, 16)
    f32 = jnp.float32
    x = jax.random.normal(ks[0], (BATCH, SEQ, D_MODEL), f32)
    norm_g = 1.0 + 0.05 * jax.random.normal(ks[1], (DEPTH, D_MODEL), f32)
    w_in = jax.random.normal(ks[2], (DEPTH, D_MODEL, N_IN), f32) * D_MODEL ** -0.5
    q_norm_g = 1.0 + 0.05 * jax.random.normal(ks[3], (DEPTH, A_HEAD_DIM), f32)
    k_norm_g = 1.0 + 0.05 * jax.random.normal(ks[4], (DEPTH, A_HEAD_DIM), f32)
    rel_bias = 0.5 * jax.random.normal(ks[5], (REL_BUCKETS, A_HEADS), f32)
    sgu_ln_g = 1.0 + 0.05 * jax.random.normal(ks[6], (DEPTH, B_WIDTH), f32)
    sgu_ln_b = 0.02 * jax.random.normal(ks[7], (DEPTH, B_WIDTH), f32)
    w_spatial = jax.random.normal(ks[8], (DEPTH, B_GROUPS, CHUNK, CHUNK), f32) * CHUNK ** -0.5
    b_spatial = 1.0 + 0.1 * jax.random.normal(ks[9], (DEPTH, B_GROUPS, CHUNK), f32)
    w_branch = jax.random.normal(ks[10], (DEPTH, N_BRANCH, A_WIDTH, D_MODEL), f32) * A_WIDTH ** -0.5
    w_out = jax.random.normal(ks[11], (DEPTH, D_MODEL, D_MODEL), f32) * D_MODEL ** -0.5
    return {"x": x, "norm_g": norm_g, "w_in": w_in, "q_norm_g": q_norm_g,
            "k_norm_g": k_norm_g, "rel_bias": rel_bias, "sgu_ln_g": sgu_ln_g,
            "sgu_ln_b": sgu_ln_b, "w_spatial": w_spatial, "b_spatial": b_spatial,
            "w_branch": w_branch, "w_out": w_out}


def reference(x, norm_g, w_in, q_norm_g, k_norm_g, rel_bias, sgu_ln_g, sgu_ln_b,
              w_spatial, b_spatial, w_branch, w_out):
    B, S, _ = x.shape
    offs = split_offsets()
    for l in range(DEPTH):
        h = rms_norm(x, norm_g[l])
        z = jnp.einsum('bsd,dn->bsn', h, w_in[l])
        (q, k, v, gate_a, q_idx, k_idx, w_idx,
         u, v_b, gate_b, merge_a, merge_b) = jnp.split(z, offs, axis=-1)
        q = rms_norm(q.reshape(B, S, A_HEADS, A_HEAD_DIM), q_norm_g[l])
        k = rms_norm(k, k_norm_g[l])
        q_idx = q_idx.reshape(B, S, IDX_HEADS, IDX_DIM)
        y_a = dsa_attention(q, k, v, q_idx, k_idx, w_idx, rel_bias) * jax.nn.silu(gate_a)
        y_b = chunked_sgu(jax.nn.gelu(u), jax.nn.gelu(v_b), sgu_ln_g[l], sgu_ln_b[l],
                          w_spatial[l], b_spatial[l]) * jax.nn.silu(gate_b)
        y = jnp.stack([y_a, y_b], axis=2)
        y_d = jnp.einsum('bsnc,ncd->bsnd', y, w_branch[l])
        merged = jax.nn.sigmoid(merge_a) * y_d[:, :, 0] + jax.nn.sigmoid(merge_b) * y_d[:, :, 1]
        x = x + jnp.einsum('bsd,de->bse', merged, w_out[l])
    return x
```

```python
import functools
import math

import numpy as np
import jax
import jax.numpy as jnp
from jax import lax
from jax.experimental import pallas as pl
from jax.experimental.pallas import tpu as pltpu

D_MODEL = 1024
A_HEADS = 8
A_HEAD_DIM = 64
A_WIDTH = A_HEADS * A_HEAD_DIM
IDX_HEADS = 8
IDX_DIM = 64
TOPK_MAX = 256
B_GROUPS = 4
B_GROUP_DIM = 128
B_WIDTH = B_GROUPS * B_GROUP_DIM
CHUNK = 128
REL_BUCKETS = 32
REL_MAX_DIST = 128
EPS = 1e-6

LANES = 128
VMEM_LIMIT_BYTES = 56 * 1024 * 1024

TOKEN_TILE = 512
Q_TILE = 256
BISECT_ITERS = 20
BISECT_EXTRA = 4

_SPLITS = (A_WIDTH, A_HEAD_DIM, A_HEAD_DIM, A_WIDTH, IDX_HEADS * IDX_DIM, IDX_DIM,
           IDX_HEADS, B_WIDTH, B_WIDTH, B_WIDTH, D_MODEL, D_MODEL)
_OFFS = np.concatenate([[0], np.cumsum(_SPLITS)])
(_Q0, _K0, _V0, _GA0, _QI0, _KI0, _WI0, _U0, _VB0, _GB0, _MA0, _MB0, _END) = [int(o) for o in _OFFS]

f32 = jnp.float32
bf16 = jnp.bfloat16


def _t5_bucket_np(rel):
    max_exact = REL_BUCKETS // 2
    nf = np.maximum(rel, 1).astype(np.float32)
    large = max_exact + (np.log(nf / np.float32(max_exact))
                         / np.float32(math.log(REL_MAX_DIST / max_exact))
                         * np.float32(REL_BUCKETS - max_exact)).astype(np.int32)
    large = np.minimum(large, REL_BUCKETS - 1)
    return np.where(rel < max_exact, rel, large).astype(np.int32)


def _near_bucket_map():
    tq = np.arange(Q_TILE)[:, None]
    tk = np.arange(2 * Q_TILE)[None, :]
    rel = Q_TILE + tq - tk
    return _t5_bucket_np(np.maximum(rel, 0))


def _rms(x, g):
    ms = jnp.mean(x * x, axis=-1, keepdims=True)
    return x * lax.rsqrt(ms + EPS) * g


def _bias_band_kernel(rb_ref, bmap_ref, out_ref):
    h = pl.program_id(0)
    bmap = bmap_ref[...]
    acc = jnp.zeros(bmap.shape, f32)
    for b in range(REL_BUCKETS):
        acc = jnp.where(bmap == b, rb_ref[b, h], acc)
    out_ref[0] = acc


def _bias_band(rel_bias):
    bmap = jnp.asarray(_near_bucket_map())
    return pl.pallas_call(
        _bias_band_kernel,
        grid=(A_HEADS,),
        in_specs=[pl.BlockSpec(memory_space=pltpu.SMEM),
                  pl.BlockSpec((Q_TILE, 2 * Q_TILE), lambda h: (0, 0))],
        out_specs=pl.BlockSpec((1, Q_TILE, 2 * Q_TILE), lambda h: (h, 0, 0)),
        out_shape=jax.ShapeDtypeStruct((A_HEADS, Q_TILE, 2 * Q_TILE), f32),
        name="bias_band",
    )(rel_bias, bmap)


def _split3(s):
    hi = s.astype(bf16)
    r1 = s - hi.astype(f32)
    mid = r1.astype(bf16)
    lo = (r1 - mid.astype(f32)).astype(bf16)
    return hi, mid, lo


def _proj_a_kernel(x_ref, g_ref, w_ref, qg_ref, kg_ref, bd_ref,
                   q_out, qi_out, kkt_out, vext_out, widx_out, gate_out):
    h = _rms(x_ref[0], g_ref[...]).astype(bf16)

    def proj(lo, hi):
        return jnp.dot(h, w_ref[:, lo:hi], preferred_element_type=f32)

    zq = proj(0, A_WIDTH)
    bd = bd_ref[...]
    ssq = sum(jnp.dot(t, bd, preferred_element_type=f32) for t in _split3(zq * zq))
    qn = zq * lax.rsqrt(ssq * (1.0 / A_HEAD_DIM) + EPS) * qg_ref[...]
    qn = (qn * (A_HEAD_DIM ** -0.5)).astype(bf16)
    zqi = proj(A_WIDTH, 2 * A_WIDTH).astype(bf16)
    for hd in range(A_HEADS):
        sl = slice(hd * A_HEAD_DIM, (hd + 1) * A_HEAD_DIM)
        q_out[0, hd] = qn[:, sl]
        qi_out[0, hd] = zqi[:, sl]

    gate_out[0] = proj(2 * A_WIDTH, 3 * A_WIDTH)

    zkk = proj(3 * A_WIDTH, 3 * A_WIDTH + LANES)
    lane = lax.broadcasted_iota(jnp.int32, zkk.shape, 1)
    is_k = lane < A_HEAD_DIM
    ssk = jnp.sum(jnp.where(is_k, zkk * zkk, 0.0), axis=-1, keepdims=True)
    kn = zkk * lax.rsqrt(ssk * (1.0 / A_HEAD_DIM) + EPS) * kg_ref[...]
    kk = jnp.where(is_k, kn, zkk)
    kkt_out[0] = kk.T.astype(bf16)

    zvw = proj(3 * A_WIDTH + LANES, 3 * A_WIDTH + 2 * LANES)
    vext = jnp.where(lane < A_HEAD_DIM, zvw, jnp.where(lane == A_HEAD_DIM, 1.0, 0.0))
    vext_out[0] = vext.astype(bf16)
    widx_out[0] = zvw[:, A_HEAD_DIM:A_HEAD_DIM + IDX_HEADS] * (IDX_HEADS ** -0.5 * IDX_DIM ** -0.5)


def _proj_a(x, norm_g, w_a, qg, kg, bd):
    B, S, _ = x.shape
    T = TOKEN_TILE
    n_a = w_a.shape[1]
    const = lambda b, i: (0, 0)
    return pl.pallas_call(
        _proj_a_kernel,
        grid=(B, S // T),
        in_specs=[pl.BlockSpec((1, T, D_MODEL), lambda b, i: (b, i, 0)),
                  pl.BlockSpec((1, D_MODEL), const),
                  pl.BlockSpec((D_MODEL, n_a), const),
                  pl.BlockSpec((1, A_WIDTH), const),
                  pl.BlockSpec((1, LANES), const),
                  pl.BlockSpec((A_WIDTH, A_WIDTH), const)],
        out_specs=[pl.BlockSpec((1, A_HEADS, T, A_HEAD_DIM), lambda b, i: (b, 0, i, 0)),
                   pl.BlockSpec((1, IDX_HEADS, T, IDX_DIM), lambda b, i: (b, 0, i, 0)),
                   pl.BlockSpec((1, LANES, T), lambda b, i: (b, 0, i)),
                   pl.BlockSpec((1, T, LANES), lambda b, i: (b, i, 0)),
                   pl.BlockSpec((1, T, IDX_HEADS), lambda b, i: (b, i, 0)),
                   pl.BlockSpec((1, T, A_WIDTH), lambda b, i: (b, i, 0))],
        out_shape=[jax.ShapeDtypeStruct((B, A_HEADS, S, A_HEAD_DIM), bf16),
                   jax.ShapeDtypeStruct((B, IDX_HEADS, S, IDX_DIM), bf16),
                   jax.ShapeDtypeStruct((B, LANES, S), bf16),
                   jax.ShapeDtypeStruct((B, S, LANES), bf16),
                   jax.ShapeDtypeStruct((B, S, IDX_HEADS), f32),
                   jax.ShapeDtypeStruct((B, S, A_WIDTH), f32)],
        compiler_params=pltpu.CompilerParams(
            dimension_semantics=("arbitrary", "arbitrary"),
            vmem_limit_bytes=VMEM_LIMIT_BYTES),
        name="proj_a",
    )(x, norm_g, w_a, qg, kg, bd)


def _count(pred):
    return jnp.sum(jnp.where(pred, 1.0, 0.0), axis=1, keepdims=True)


def _attn_block(c, rb_ref, q_ref, qi_ref, kkt_ref, vext_ref, widx_ref, gate_ref, band_ref,
                y_out, sc_ref, cut_ref, yh_ref):
    nk = (c + 1) * Q_TILE
    k_top = float(TOPK_MAX)
    kt = kkt_ref[0, 0:A_HEAD_DIM, 0:nk]
    vext = vext_ref[0, 0:nk, :]
    row = lax.broadcasted_iota(jnp.int32, (Q_TILE, nk), 0) + c * Q_TILE
    col = lax.broadcasted_iota(jnp.int32, (Q_TILE, nk), 1)
    causal = col <= row

    if nk > TOPK_MAX:
        kit = kkt_ref[0, A_HEAD_DIM:2 * A_HEAD_DIM, 0:nk]
        widx = widx_ref[0]

        def idx_term(hd):
            s = jnp.dot(qi_ref[0, hd], kit, preferred_element_type=f32)
            return widx[:, hd:hd + 1] * jnp.maximum(s, 0.0)

        sc = idx_term(0)
        for hd in range(1, IDX_HEADS):
            sc = sc + idx_term(hd)
        sc_ref[:, 0:nk] = jnp.where(causal, sc, -jnp.inf)

        def scores():
            return sc_ref[:, 0:nk]

        sc = scores()
        lo0 = jnp.min(jnp.where(causal, sc, jnp.inf), axis=1, keepdims=True)
        hi0 = jnp.max(sc, axis=1, keepdims=True)

        def halve(_, lohi):
            lo, hi = lohi
            mid = lo + (hi - lo) * 0.5
            ge = _count(scores() >= mid) >= k_top
            return jnp.where(ge, mid, lo), jnp.where(ge, hi, mid)

        def candidate(lo):
            s = scores()
            a = jnp.min(jnp.where(s >= lo, s, jnp.inf), axis=1, keepdims=True)
            return a, _count(s > a)

        lo, hi = lax.fori_loop(0, BISECT_ITERS, halve, (lo0, hi0))
        a, cgt = candidate(lo)

        def unfinished(st):
            return jnp.max(st[3]) >= k_top

        def refine(st):
            lo, hi, a, cgt = st
            s = scores()
            nxt = jnp.min(jnp.where(s > a, s, jnp.inf), axis=1, keepdims=True)
            lo = jnp.where(cgt >= k_top, nxt, lo)
            lo, hi = lax.fori_loop(0, BISECT_EXTRA, halve, (lo, hi))
            a, cgt = candidate(lo)
            return lo, hi, a, cgt

        _, _, thr, cgt = lax.while_loop(unfinished, refine, (lo, hi, a, cgt))

        cut_ref[...] = jnp.full((Q_TILE, 1), nk, jnp.int32)
        cge = _count(scores() >= thr)

        @pl.when(jnp.max(cge) > k_top)
        def _():
            eq = scores() == thr
            need = k_top - cgt

            def step(_, lh):
                lo_i, hi_i = lh
                mid = lax.shift_right_arithmetic(lo_i + hi_i, 1)
                ok = _count(eq & (col <= mid)) >= need
                return jnp.where(ok, lo_i, mid), jnp.where(ok, mid, hi_i)

            lo_i = jnp.full((Q_TILE, 1), -1, jnp.int32)
            hi_i = jnp.full((Q_TILE, 1), nk - 1, jnp.int32)
            _, hi_i = lax.fori_loop(0, int(math.ceil(math.log2(nk))), step, (lo_i, hi_i))
            cut_ref[...] = hi_i

        s = scores()
        keep = (s > thr) | ((s == thr) & (col <= cut_ref[...]))
    else:
        keep = causal

    def head(hd, carry):
        logit = jnp.dot(q_ref[0, hd], kt, preferred_element_type=f32)
        near_w = min(nk, 2 * Q_TILE)
        near = logit[:, nk - near_w:] + band_ref[hd][:, 2 * Q_TILE - near_w:]
        if nk > near_w:
            far = logit[:, :nk - near_w] + rb_ref[REL_BUCKETS - 1, hd]
            logit = jnp.concatenate([far, near], axis=1)
        else:
            logit = near
        logit = jnp.where(keep, logit, -jnp.inf)
        m = jnp.max(logit, axis=1, keepdims=True)
        p = jnp.exp(logit - m).astype(bf16)
        pv = jnp.dot(p, vext, preferred_element_type=f32)
        yh_ref[hd] = pv[:, 0:A_HEAD_DIM] / pv[:, A_HEAD_DIM:A_HEAD_DIM + 1]
        return carry

    lax.fori_loop(0, A_HEADS, head, 0)
    y = jnp.concatenate([yh_ref[hd] for hd in range(A_HEADS)], axis=1)
    g = gate_ref[0]
    y_out[0] = (y * (g * jax.nn.sigmoid(g))).astype(bf16)


def _attn_kernel(rb_ref, q_ref, qi_ref, kkt_ref, vext_ref, widx_ref, gate_ref, band_ref,
                 y_out, sc_ref, cut_ref, yh_ref):
    i = pl.program_id(1)
    for c in range(kkt_ref.shape[2] // Q_TILE):
        @pl.when(i == c)
        def _(c=c):
            _attn_block(c, rb_ref, q_ref, qi_ref, kkt_ref, vext_ref, widx_ref, gate_ref, band_ref,
                        y_out, sc_ref, cut_ref, yh_ref)


def _attn(rel_bias, q, qi, kkt, vext, widx, gate, band):
    B, _, S, _ = q.shape
    return pl.pallas_call(
        _attn_kernel,
        grid=(B, S // Q_TILE),
        in_specs=[pl.BlockSpec(memory_space=pltpu.SMEM),
                  pl.BlockSpec((1, A_HEADS, Q_TILE, A_HEAD_DIM), lambda b, i: (b, 0, i, 0)),
                  pl.BlockSpec((1, IDX_HEADS, Q_TILE, IDX_DIM), lambda b, i: (b, 0, i, 0)),
                  pl.BlockSpec((1, LANES, S), lambda b, i: (b, 0, 0)),
                  pl.BlockSpec((1, S, LANES), lambda b, i: (b, 0, 0)),
                  pl.BlockSpec((1, Q_TILE, IDX_HEADS), lambda b, i: (b, i, 0)),
                  pl.BlockSpec((1, Q_TILE, A_WIDTH), lambda b, i: (b, i, 0)),
                  pl.BlockSpec((A_HEADS, Q_TILE, 2 * Q_TILE), lambda b, i: (0, 0, 0))],
        out_specs=pl.BlockSpec((1, Q_TILE, A_WIDTH), lambda b, i: (b, i, 0)),
        out_shape=jax.ShapeDtypeStruct((B, S, A_WIDTH), bf16),
        scratch_shapes=[pltpu.VMEM((Q_TILE, S), f32),
                        pltpu.VMEM((Q_TILE, 1), jnp.int32),
                        pltpu.VMEM((A_HEADS, Q_TILE, A_HEAD_DIM), f32)],
        compiler_params=pltpu.CompilerParams(
            dimension_semantics=("arbitrary", "arbitrary"),
            vmem_limit_bytes=VMEM_LIMIT_BYTES),
        name="attn",
    )(rel_bias, q, qi, kkt, vext, widx, gate, band)


def _mix_kernel(x_ref, ya_ref, g_ref, w_ref, lng_ref, lnb_ref, ws_ref, bst_ref, wbr_ref, wo_ref,
                out_ref):
    x = x_ref[...]
    h = _rms(x, g_ref[...]).astype(bf16)
    T = x.shape[0]

    def proj(lo, hi):
        return jnp.dot(h, w_ref[:, lo:hi], preferred_element_type=f32)

    u = jax.nn.gelu(proj(0, B_WIDTH))
    vb = jax.nn.gelu(proj(B_WIDTH, 2 * B_WIDTH))
    mu = jnp.mean(vb, axis=-1, keepdims=True)
    var = jnp.mean(jnp.square(vb - mu), axis=-1, keepdims=True)
    vln = ((vb - mu) * lax.rsqrt(var + EPS) * lng_ref[...] + lnb_ref[...]).astype(bf16)

    n_ch = T // CHUNK
    tril = (lax.broadcasted_iota(jnp.int32, (CHUNK, CHUNK), 1)
            <= lax.broadcasted_iota(jnp.int32, (CHUNK, CHUNK), 0))
    bst = bst_ref[...]
    per_group = []
    for gi in range(B_GROUPS):
        wg = jnp.where(tril, ws_ref[gi], 0.0).astype(bf16)
        cols = slice(gi * B_GROUP_DIM, (gi + 1) * B_GROUP_DIM)
        vg = jnp.concatenate([vln[ci * CHUNK:(ci + 1) * CHUNK, cols] for ci in range(n_ch)], axis=1)
        sg = jnp.dot(wg, vg, preferred_element_type=f32) + bst[:, gi:gi + 1]
        per_group.append(sg)
    s = jnp.concatenate(
        [jnp.concatenate([per_group[gi][:, ci * B_GROUP_DIM:(ci + 1) * B_GROUP_DIM]
                          for gi in range(B_GROUPS)], axis=1)
         for ci in range(n_ch)], axis=0)

    gb = proj(2 * B_WIDTH, 3 * B_WIDTH)
    yb = (u * s * (gb * jax.nn.sigmoid(gb))).astype(bf16)
    yd_a = jnp.dot(ya_ref[...], wbr_ref[0], preferred_element_type=f32)
    yd_b = jnp.dot(yb, wbr_ref[1], preferred_element_type=f32)
    ma = proj(3 * B_WIDTH, 3 * B_WIDTH + D_MODEL)
    mb = proj(3 * B_WIDTH + D_MODEL, 3 * B_WIDTH + 2 * D_MODEL)
    merged = jax.nn.sigmoid(ma) * yd_a + jax.nn.sigmoid(mb) * yd_b
    out_ref[...] = x + jnp.dot(merged.astype(bf16), wo_ref[...], preferred_element_type=f32)


def _mix(x2, ya2, norm_g, w_b, ln_g, ln_b, w_sp, b_sp_t, w_br, w_o):
    N = x2.shape[0]
    T = TOKEN_TILE
    n_b = w_b.shape[1]
    c2 = lambda i: (0, 0)
    c3 = lambda i: (0, 0, 0)
    return pl.pallas_call(
        _mix_kernel,
        grid=(N // T,),
        in_specs=[pl.BlockSpec((T, D_MODEL), lambda i: (i, 0)),
                  pl.BlockSpec((T, A_WIDTH), lambda i: (i, 0)),
                  pl.BlockSpec((1, D_MODEL), c2),
                  pl.BlockSpec((D_MODEL, n_b), c2),
                  pl.BlockSpec((1, B_WIDTH), c2),
                  pl.BlockSpec((1, B_WIDTH), c2),
                  pl.BlockSpec((B_GROUPS, CHUNK, CHUNK), c3),
                  pl.BlockSpec((CHUNK, B_GROUPS), c2),
                  pl.BlockSpec((2, A_WIDTH, D_MODEL), c3),
                  pl.BlockSpec((D_MODEL, D_MODEL), c2)],
        out_specs=pl.BlockSpec((T, D_MODEL), lambda i: (i, 0)),
        out_shape=jax.ShapeDtypeStruct((N, D_MODEL), f32),
        compiler_params=pltpu.CompilerParams(
            dimension_semantics=("arbitrary",),
            vmem_limit_bytes=VMEM_LIMIT_BYTES),
        name="mix",
    )(x2, ya2, norm_g, w_b, ln_g, ln_b, w_sp, b_sp_t, w_br, w_o)


def _attn_side_weights(w):
    pad = jnp.zeros((D_MODEL, LANES - A_HEAD_DIM - IDX_HEADS), w.dtype)
    return jnp.concatenate(
        [w[:, _Q0:_K0], w[:, _QI0:_KI0], w[:, _GA0:_QI0],
         w[:, _K0:_V0], w[:, _KI0:_WI0],
         w[:, _V0:_GA0], w[:, _WI0:_U0], pad], axis=1).astype(bf16)


def kernel(x, norm_g, w_in, q_norm_g, k_norm_g, rel_bias, sgu_ln_g, sgu_ln_b,
           w_spatial, b_spatial, w_branch, w_out):
    B, S, D = x.shape
    depth = w_in.shape[0]
    band = _bias_band(rel_bias)
    head_of = np.arange(A_WIDTH) // A_HEAD_DIM
    bd = jnp.asarray(head_of[:, None] == head_of[None, :], dtype=bf16)
    for l in range(depth):
        w_a = _attn_side_weights(w_in[l])
        w_b = w_in[l][:, _U0:_END].astype(bf16)
        g = norm_g[l][None, :]
        qg = jnp.tile(q_norm_g[l], A_HEADS)[None, :]
        kg = jnp.concatenate([k_norm_g[l], jnp.ones((LANES - A_HEAD_DIM,), f32)])[None, :]
        q, qi, kkt, vext, widx, gate = _proj_a(x, g, w_a, qg, kg, bd)
        ya = _attn(rel_bias, q, qi, kkt, vext, widx, gate, band)
        x2 = _mix(x.reshape(B * S, D), ya.reshape(B * S, A_WIDTH), g, w_b,
                  sgu_ln_g[l][None, :], sgu_ln_b[l][None, :], w_spatial[l],
                  b_spatial[l].T, w_branch[l].astype(bf16), w_out[l].astype(bf16))
        x = x2.reshape(B, S, D)
    return x
```

```python
import functools
import math

import numpy as np
import jax
import jax.numpy as jnp
from jax import lax
from jax.experimental import pallas as pl
from jax.experimental.pallas import tpu as pltpu

D_MODEL = 1024
A_HEADS = 8
A_HEAD_DIM = 64
A_WIDTH = A_HEADS * A_HEAD_DIM
IDX_HEADS = 8
IDX_DIM = 64
TOPK_MAX = 256
B_GROUPS = 4
B_GROUP_DIM = 128
B_WIDTH = B_GROUPS * B_GROUP_DIM
CHUNK = 128
REL_BUCKETS = 32
REL_MAX_DIST = 128
EPS = 1e-6

LANES = 128
VMEM_LIMIT_BYTES = 56 * 1024 * 1024

TOKEN_TILE = 512
Q_TILE = 256
BISECT_ITERS = 20
BISECT_EXTRA = 4
BISECT_UNROLL = 4
HEAD_UNROLL = 2

_SPLITS = (A_WIDTH, A_HEAD_DIM, A_HEAD_DIM, A_WIDTH, IDX_HEADS * IDX_DIM, IDX_DIM,
           IDX_HEADS, B_WIDTH, B_WIDTH, B_WIDTH, D_MODEL, D_MODEL)
_OFFS = np.concatenate([[0], np.cumsum(_SPLITS)])
(_Q0, _K0, _V0, _GA0, _QI0, _KI0, _WI0, _U0, _VB0, _GB0, _MA0, _MB0, _END) = [int(o) for o in _OFFS]

f32 = jnp.float32
bf16 = jnp.bfloat16


def _t5_bucket_np(rel):
    max_exact = REL_BUCKETS // 2
    nf = np.maximum(rel, 1).astype(np.float32)
    large = max_exact + (np.log(nf / np.float32(max_exact))
                         / np.float32(math.log(REL_MAX_DIST / max_exact))
                         * np.float32(REL_BUCKETS - max_exact)).astype(np.int32)
    large = np.minimum(large, REL_BUCKETS - 1)
    return np.where(rel < max_exact, rel, large).astype(np.int32)


def _near_bucket_map():
    tq = np.arange(Q_TILE)[:, None]
    tk = np.arange(2 * Q_TILE)[None, :]
    rel = Q_TILE + tq - tk
    return _t5_bucket_np(np.maximum(rel, 0))


def _rms(x, g):
    ms = jnp.mean(x * x, axis=-1, keepdims=True)
    return x * lax.rsqrt(ms + EPS) * g


def _bias_band_kernel(rb_ref, bmap_ref, out_ref):
    h = pl.program_id(0)
    bmap = bmap_ref[...]
    acc = jnp.zeros(bmap.shape, f32)
    for b in range(REL_BUCKETS):
        acc = jnp.where(bmap == b, rb_ref[b, h], acc)
    out_ref[0] = acc - rb_ref[REL_BUCKETS - 1, h]


def _bias_band(rel_bias):
    bmap = jnp.asarray(_near_bucket_map())
    return pl.pallas_call(
        _bias_band_kernel,
        grid=(A_HEADS,),
        in_specs=[pl.BlockSpec(memory_space=pltpu.SMEM),
                  pl.BlockSpec((Q_TILE, 2 * Q_TILE), lambda h: (0, 0))],
        out_specs=pl.BlockSpec((1, Q_TILE, 2 * Q_TILE), lambda h: (h, 0, 0)),
        out_shape=jax.ShapeDtypeStruct((A_HEADS, Q_TILE, 2 * Q_TILE), f32),
        name="bias_band",
    )(rel_bias, bmap)


def _split3(s):
    hi = s.astype(bf16)
    r1 = s - hi.astype(f32)
    mid = r1.astype(bf16)
    lo = (r1 - mid.astype(f32)).astype(bf16)
    return hi, mid, lo


def _proj_a_kernel(x_ref, g_ref, w_ref, qg_ref, kg_ref, bd_ref,
                   q_out, qi_out, kkt_out, vext_out, widx_out, gate_out):
    h = _rms(x_ref[0], g_ref[...]).astype(bf16)

    def proj(lo, hi):
        return jnp.dot(h, w_ref[:, lo:hi], preferred_element_type=f32)

    zq = proj(0, A_WIDTH)
    bd = bd_ref[...]
    ssq = sum(jnp.dot(t, bd, preferred_element_type=f32) for t in _split3(zq * zq))
    qn = zq * lax.rsqrt(ssq * (1.0 / A_HEAD_DIM) + EPS) * qg_ref[...]
    qn = (qn * (A_HEAD_DIM ** -0.5)).astype(bf16)
    zqi = proj(A_WIDTH, 2 * A_WIDTH).astype(bf16)
    for hd in range(A_HEADS):
        sl = slice(hd * A_HEAD_DIM, (hd + 1) * A_HEAD_DIM)
        q_out[0, hd] = qn[:, sl]
        qi_out[0, hd] = zqi[:, sl]

    gate_out[0] = proj(2 * A_WIDTH, 3 * A_WIDTH)

    zkk = proj(3 * A_WIDTH, 3 * A_WIDTH + LANES)
    lane = lax.broadcasted_iota(jnp.int32, zkk.shape, 1)
    is_k = lane < A_HEAD_DIM
    ssk = jnp.sum(jnp.where(is_k, zkk * zkk, 0.0), axis=-1, keepdims=True)
    kn = zkk * lax.rsqrt(ssk * (1.0 / A_HEAD_DIM) + EPS) * kg_ref[...]
    kk = jnp.where(is_k, kn, zkk)
    kkt_out[0] = kk.T.astype(bf16)

    zvw = proj(3 * A_WIDTH + LANES, 3 * A_WIDTH + 2 * LANES)
    vext = jnp.where(lane < A_HEAD_DIM, zvw, jnp.where(lane == A_HEAD_DIM, 1.0, 0.0))
    vext_out[0] = vext.astype(bf16)
    widx_out[0] = zvw[:, A_HEAD_DIM:A_HEAD_DIM + IDX_HEADS] * (IDX_HEADS ** -0.5 * IDX_DIM ** -0.5)


def _proj_a(x, norm_g, w_a, qg, kg, bd):
    B, S, _ = x.shape
    T = TOKEN_TILE
    n_a = w_a.shape[1]
    const = lambda b, i: (0, 0)
    return pl.pallas_call(
        _proj_a_kernel,
        grid=(B, S // T),
        in_specs=[pl.BlockSpec((1, T, D_MODEL), lambda b, i: (b, i, 0)),
                  pl.BlockSpec((1, D_MODEL), const),
                  pl.BlockSpec((D_MODEL, n_a), const),
                  pl.BlockSpec((1, A_WIDTH), const),
                  pl.BlockSpec((1, LANES), const),
                  pl.BlockSpec((A_WIDTH, A_WIDTH), const)],
        out_specs=[pl.BlockSpec((1, A_HEADS, T, A_HEAD_DIM), lambda b, i: (b, 0, i, 0)),
                   pl.BlockSpec((1, IDX_HEADS, T, IDX_DIM), lambda b, i: (b, 0, i, 0)),
                   pl.BlockSpec((1, LANES, T), lambda b, i: (b, 0, i)),
                   pl.BlockSpec((1, T, LANES), lambda b, i: (b, i, 0)),
                   pl.BlockSpec((1, T, IDX_HEADS), lambda b, i: (b, i, 0)),
                   pl.BlockSpec((1, T, A_WIDTH), lambda b, i: (b, i, 0))],
        out_shape=[jax.ShapeDtypeStruct((B, A_HEADS, S, A_HEAD_DIM), bf16),
                   jax.ShapeDtypeStruct((B, IDX_HEADS, S, IDX_DIM), bf16),
                   jax.ShapeDtypeStruct((B, LANES, S), bf16),
                   jax.ShapeDtypeStruct((B, S, LANES), bf16),
                   jax.ShapeDtypeStruct((B, S, IDX_HEADS), f32),
                   jax.ShapeDtypeStruct((B, S, A_WIDTH), f32)],
        compiler_params=pltpu.CompilerParams(
            dimension_semantics=("arbitrary", "arbitrary"),
            vmem_limit_bytes=VMEM_LIMIT_BYTES),
        name="proj_a",
    )(x, norm_g, w_a, qg, kg, bd)


def _count(pred):
    return jnp.sum(jnp.where(pred, 1.0, 0.0), axis=1, keepdims=True)


def _attn_block(c, q_ref, qi_ref, kkt_ref, vext_ref, widx_ref, gate_ref, band_ref,
                y_out, sc_ref, cut_ref, yh_ref):
    nk = (c + 1) * Q_TILE
    k_top = float(TOPK_MAX)
    kt = kkt_ref[0, 0:A_HEAD_DIM, 0:nk]
    vext = vext_ref[0, 0:nk, :]
    row = lax.broadcasted_iota(jnp.int32, (Q_TILE, nk), 0) + c * Q_TILE
    col = lax.broadcasted_iota(jnp.int32, (Q_TILE, nk), 1)
    causal = col <= row

    if nk > TOPK_MAX:
        kit = kkt_ref[0, A_HEAD_DIM:2 * A_HEAD_DIM, 0:nk]
        widx = widx_ref[0]

        def idx_term(hd):
            s = jnp.dot(qi_ref[0, hd], kit, preferred_element_type=f32)
            return widx[:, hd:hd + 1] * jnp.maximum(s, 0.0)

        sc = idx_term(0)
        for hd in range(1, IDX_HEADS):
            sc = sc + idx_term(hd)
        sc_ref[:, 0:nk] = jnp.where(causal, sc, -jnp.inf)

        def scores():
            return sc_ref[:, 0:nk]

        sc = scores()
        lo0 = jnp.min(jnp.where(causal, sc, jnp.inf), axis=1, keepdims=True)
        hi0 = jnp.max(sc, axis=1, keepdims=True)

        def halve(_, lohi):
            lo, hi = lohi
            mid = lo + (hi - lo) * 0.5
            ge = _count(scores() >= mid) >= k_top
            return jnp.where(ge, mid, lo), jnp.where(ge, hi, mid)

        def candidate(lo):
            s = scores()
            a = jnp.min(jnp.where(s >= lo, s, jnp.inf), axis=1, keepdims=True)
            return a, _count(s > a)

        lo, hi = lax.fori_loop(0, BISECT_ITERS, halve, (lo0, hi0), unroll=BISECT_UNROLL)
        a, cgt = candidate(lo)

        def unfinished(st):
            return jnp.max(st[3]) >= k_top

        def refine(st):
            lo, hi, a, cgt = st
            s = scores()
            nxt = jnp.min(jnp.where(s > a, s, jnp.inf), axis=1, keepdims=True)
            lo = jnp.where(cgt >= k_top, nxt, lo)
            lo, hi = lax.fori_loop(0, BISECT_EXTRA, halve, (lo, hi))
            a, cgt = candidate(lo)
            return lo, hi, a, cgt

        _, _, thr, cgt = lax.while_loop(unfinished, refine, (lo, hi, a, cgt))

        cut_ref[...] = jnp.full((Q_TILE, 1), nk, jnp.int32)
        cge = _count(scores() >= thr)

        @pl.when(jnp.max(cge) > k_top)
        def _():
            eq = scores() == thr
            need = k_top - cgt

            def step(_, lh):
                lo_i, hi_i = lh
                mid = lax.shift_right_arithmetic(lo_i + hi_i, 1)
                ok = _count(eq & (col <= mid)) >= need
                return jnp.where(ok, lo_i, mid), jnp.where(ok, mid, hi_i)

            lo_i = jnp.full((Q_TILE, 1), -1, jnp.int32)
            hi_i = jnp.full((Q_TILE, 1), nk - 1, jnp.int32)
            _, hi_i = lax.fori_loop(0, int(math.ceil(math.log2(nk))), step, (lo_i, hi_i))
            cut_ref[...] = hi_i

        s = scores()
        keep = (s > thr) | ((s == thr) & (col <= cut_ref[...]))
    else:
        keep = causal
    sc_ref[:, 0:nk] = jnp.where(keep, 0.0, -jnp.inf)
    near_w = min(nk, 2 * Q_TILE)

    def head(hd, carry):
        logit = jnp.dot(q_ref[0, hd], kt, preferred_element_type=f32)
        near = logit[:, nk - near_w:] + (band_ref[hd][:, 2 * Q_TILE - near_w:] + sc_ref[:, nk - near_w:nk])
        if nk > near_w:
            far = logit[:, :nk - near_w] + sc_ref[:, 0:nk - near_w]
            logit = jnp.concatenate([far, near], axis=1)
        else:
            logit = near
        m = jnp.max(logit, axis=1, keepdims=True)
        p = jnp.exp(logit - m).astype(bf16)
        pv = jnp.dot(p, vext, preferred_element_type=f32)
        yh_ref[hd] = pv[:, 0:A_HEAD_DIM] / pv[:, A_HEAD_DIM:A_HEAD_DIM + 1]
        return carry

    lax.fori_loop(0, A_HEADS, head, 0, unroll=HEAD_UNROLL)
    y = jnp.concatenate([yh_ref[hd] for hd in range(A_HEADS)], axis=1)
    g = gate_ref[0]
    y_out[0] = (y * (g * jax.nn.sigmoid(g))).astype(bf16)


def _attn_kernel(q_ref, qi_ref, kkt_ref, vext_ref, widx_ref, gate_ref, band_ref,
                 y_out, sc_ref, cut_ref, yh_ref):
    i = pl.program_id(1)
    for c in range(kkt_ref.shape[2] // Q_TILE):
        @pl.when(i == c)
        def _(c=c):
            _attn_block(c, q_ref, qi_ref, kkt_ref, vext_ref, widx_ref, gate_ref, band_ref,
                        y_out, sc_ref, cut_ref, yh_ref)


def _attn(q, qi, kkt, vext, widx, gate, band):
    B, _, S, _ = q.shape
    return pl.pallas_call(
        _attn_kernel,
        grid=(B, S // Q_TILE),
        in_specs=[pl.BlockSpec((1, A_HEADS, Q_TILE, A_HEAD_DIM), lambda b, i: (b, 0, i, 0)),
                  pl.BlockSpec((1, IDX_HEADS, Q_TILE, IDX_DIM), lambda b, i: (b, 0, i, 0)),
                  pl.BlockSpec((1, LANES, S), lambda b, i: (b, 0, 0)),
                  pl.BlockSpec((1, S, LANES), lambda b, i: (b, 0, 0)),
                  pl.BlockSpec((1, Q_TILE, IDX_HEADS), lambda b, i: (b, i, 0)),
                  pl.BlockSpec((1, Q_TILE, A_WIDTH), lambda b, i: (b, i, 0)),
                  pl.BlockSpec((A_HEADS, Q_TILE, 2 * Q_TILE), lambda b, i: (0, 0, 0))],
        out_specs=pl.BlockSpec((1, Q_TILE, A_WIDTH), lambda b, i: (b, i, 0)),
        out_shape=jax.ShapeDtypeStruct((B, S, A_WIDTH), bf16),
        scratch_shapes=[pltpu.VMEM((Q_TILE, S), f32),
                        pltpu.VMEM((Q_TILE, 1), jnp.int32),
                        pltpu.VMEM((A_HEADS, Q_TILE, A_HEAD_DIM), f32)],
        compiler_params=pltpu.CompilerParams(
            dimension_semantics=("arbitrary", "arbitrary"),
            vmem_limit_bytes=VMEM_LIMIT_BYTES),
        name="attn",
    )(q, qi, kkt, vext, widx, gate, band)


def _mix_kernel(x_ref, ya_ref, g_ref, w_ref, lng_ref, lnb_ref, ws_ref, bst_ref, wbr_ref, wo_ref,
                out_ref):
    x = x_ref[...]
    h = _rms(x, g_ref[...]).astype(bf16)
    T = x.shape[0]

    def proj(lo, hi):
        return jnp.dot(h, w_ref[:, lo:hi], preferred_element_type=f32)

    u = jax.nn.gelu(proj(0, B_WIDTH))
    vb = jax.nn.gelu(proj(B_WIDTH, 2 * B_WIDTH))
    mu = jnp.mean(vb, axis=-1, keepdims=True)
    var = jnp.mean(jnp.square(vb - mu), axis=-1, keepdims=True)
    vln = ((vb - mu) * lax.rsqrt(var + EPS) * lng_ref[...] + lnb_ref[...]).astype(bf16)

    n_ch = T // CHUNK
    tril = (lax.broadcasted_iota(jnp.int32, (CHUNK, CHUNK), 1)
            <= lax.broadcasted_iota(jnp.int32, (CHUNK, CHUNK), 0))
    bst = bst_ref[...]
    per_group = []
    for gi in range(B_GROUPS):
        wg = jnp.where(tril, ws_ref[gi], 0.0).astype(bf16)
        cols = slice(gi * B_GROUP_DIM, (gi + 1) * B_GROUP_DIM)
        vg = jnp.concatenate([vln[ci * CHUNK:(ci + 1) * CHUNK, cols] for ci in range(n_ch)], axis=1)
        sg = jnp.dot(wg, vg, preferred_element_type=f32) + bst[:, gi:gi + 1]
        per_group.append(sg)
    s = jnp.concatenate(
        [jnp.concatenate([per_group[gi][:, ci * B_GROUP_DIM:(ci + 1) * B_GROUP_DIM]
                          for gi in range(B_GROUPS)], axis=1)
         for ci in range(n_ch)], axis=0)

    gb = proj(2 * B_WIDTH, 3 * B_WIDTH)
    yb = (u * s * (gb * jax.nn.sigmoid(gb))).astype(bf16)
    yd_a = jnp.dot(ya_ref[...], wbr_ref[0], preferred_element_type=f32)
    yd_b = jnp.dot(yb, wbr_ref[1], preferred_element_type=f32)
    ma = proj(3 * B_WIDTH, 3 * B_WIDTH + D_MODEL)
    mb = proj(3 * B_WIDTH + D_MODEL, 3 * B_WIDTH + 2 * D_MODEL)
    merged = jax.nn.sigmoid(ma) * yd_a + jax.nn.sigmoid(mb) * yd_b
    out_ref[...] = x + jnp.dot(merged.astype(bf16), wo_ref[...], preferred_element_type=f32)


def _mix(x2, ya2, norm_g, w_b, ln_g, ln_b, w_sp, b_sp_t, w_br, w_o):
    N = x2.shape[0]
    T = TOKEN_TILE
    n_b = w_b.shape[1]
    c2 = lambda i: (0, 0)
    c3 = lambda i: (0, 0, 0)
    return pl.pallas_call(
        _mix_kernel,
        grid=(N // T,),
        in_specs=[pl.BlockSpec((T, D_MODEL), lambda i: (i, 0)),
                  pl.BlockSpec((T, A_WIDTH), lambda i: (i, 0)),
                  pl.BlockSpec((1, D_MODEL), c2),
                  pl.BlockSpec((D_MODEL, n_b), c2),
                  pl.BlockSpec((1, B_WIDTH), c2),
                  pl.BlockSpec((1, B_WIDTH), c2),
                  pl.BlockSpec((B_GROUPS, CHUNK, CHUNK), c3),
                  pl.BlockSpec((CHUNK, B_GROUPS), c2),
                  pl.BlockSpec((2, A_WIDTH, D_MODEL), c3),
                  pl.BlockSpec((D_MODEL, D_MODEL), c2)],
        out_specs=pl.BlockSpec((T, D_MODEL), lambda i: (i, 0)),
        out_shape=jax.ShapeDtypeStruct((N, D_MODEL), f32),
        compiler_params=pltpu.CompilerParams(
            dimension_semantics=("arbitrary",),
            vmem_limit_bytes=VMEM_LIMIT_BYTES),
        name="mix",
    )(x2, ya2, norm_g, w_b, ln_g, ln_b, w_sp, b_sp_t, w_br, w_o)


def _attn_side_weights(w):
    pad = jnp.zeros((D_MODEL, LANES - A_HEAD_DIM - IDX_HEADS), w.dtype)
    return jnp.concatenate(
        [w[:, _Q0:_K0], w[:, _QI0:_KI0], w[:, _GA0:_QI0],
         w[:, _K0:_V0], w[:, _KI0:_WI0],
         w[:, _V0:_GA0], w[:, _WI0:_U0], pad], axis=1).astype(bf16)


def kernel(x, norm_g, w_in, q_norm_g, k_norm_g, rel_bias, sgu_ln_g, sgu_ln_b,
           w_spatial, b_spatial, w_branch, w_out):
    B, S, D = x.shape
    depth = w_in.shape[0]
    band = _bias_band(rel_bias)
    head_of = np.arange(A_WIDTH) // A_HEAD_DIM
    bd = jnp.asarray(head_of[:, None] == head_of[None, :], dtype=bf16)
    for l in range(depth):
        w_a = _attn_side_weights(w_in[l])
        w_b = w_in[l][:, _U0:_END].astype(bf16)
        g = norm_g[l][None, :]
        qg = jnp.tile(q_norm_g[l], A_HEADS)[None, :]
        kg = jnp.concatenate([k_norm_g[l], jnp.ones((LANES - A_HEAD_DIM,), f32)])[None, :]
        q, qi, kkt, vext, widx, gate = _proj_a(x, g, w_a, qg, kg, bd)
        ya = _attn(q, qi, kkt, vext, widx, gate, band)
        x2 = _mix(x.reshape(B * S, D), ya.reshape(B * S, A_WIDTH), g, w_b,
                  sgu_ln_g[l][None, :], sgu_ln_b[l][None, :], w_spatial[l],
                  b_spatial[l].T, w_branch[l].astype(bf16), w_out[l].astype(bf16))
        x = x2.reshape(B, S, D)
    return x
```

```python
import functools
import math

import numpy as np
import jax
import jax.numpy as jnp
from jax import lax
from jax.experimental import pallas as pl
from jax.experimental.pallas import tpu as pltpu

D_MODEL = 1024
A_HEADS = 8
A_HEAD_DIM = 64
A_WIDTH = A_HEADS * A_HEAD_DIM
IDX_HEADS = 8
IDX_DIM = 64
TOPK_MAX = 256
B_GROUPS = 4
B_GROUP_DIM = 128
B_WIDTH = B_GROUPS * B_GROUP_DIM
CHUNK = 128
REL_BUCKETS = 32
REL_MAX_DIST = 128
EPS = 1e-6

LANES = 128
VMEM_LIMIT_BYTES = 56 * 1024 * 1024

TOKEN_TILE = 512
Q_TILE = 256
BISECT_ITERS = 24
BISECT_UNROLL = 4
HEAD_UNROLL = 4

_SPLITS = (A_WIDTH, A_HEAD_DIM, A_HEAD_DIM, A_WIDTH, IDX_HEADS * IDX_DIM, IDX_DIM,
           IDX_HEADS, B_WIDTH, B_WIDTH, B_WIDTH, D_MODEL, D_MODEL)
_OFFS = np.concatenate([[0], np.cumsum(_SPLITS)])
(_Q0, _K0, _V0, _GA0, _QI0, _KI0, _WI0, _U0, _VB0, _GB0, _MA0, _MB0, _END) = [int(o) for o in _OFFS]

f32 = jnp.float32
bf16 = jnp.bfloat16


def _t5_bucket_np(rel):
    max_exact = REL_BUCKETS // 2
    nf = np.maximum(rel, 1).astype(np.float32)
    large = max_exact + (np.log(nf / np.float32(max_exact))
                         / np.float32(math.log(REL_MAX_DIST / max_exact))
                         * np.float32(REL_BUCKETS - max_exact)).astype(np.int32)
    large = np.minimum(large, REL_BUCKETS - 1)
    return np.where(rel < max_exact, rel, large).astype(np.int32)


def _near_bucket_map():
    tq = np.arange(Q_TILE)[:, None]
    tk = np.arange(2 * Q_TILE)[None, :]
    rel = Q_TILE + tq - tk
    return _t5_bucket_np(np.maximum(rel, 0))


def _rms(x, g):
    ms = jnp.mean(x * x, axis=-1, keepdims=True)
    return x * lax.rsqrt(ms + EPS) * g


def _bias_band_kernel(rb_ref, bmap_ref, out_ref):
    h = pl.program_id(0)
    bmap = bmap_ref[...]
    acc = jnp.zeros(bmap.shape, f32)
    for b in range(REL_BUCKETS):
        acc = jnp.where(bmap == b, rb_ref[b, h], acc)
    out_ref[0] = acc - rb_ref[REL_BUCKETS - 1, h]


def _bias_band(rel_bias):
    bmap = jnp.asarray(_near_bucket_map())
    return pl.pallas_call(
        _bias_band_kernel,
        grid=(A_HEADS,),
        in_specs=[pl.BlockSpec(memory_space=pltpu.SMEM),
                  pl.BlockSpec((Q_TILE, 2 * Q_TILE), lambda h: (0, 0))],
        out_specs=pl.BlockSpec((1, Q_TILE, 2 * Q_TILE), lambda h: (h, 0, 0)),
        out_shape=jax.ShapeDtypeStruct((A_HEADS, Q_TILE, 2 * Q_TILE), f32),
        name="bias_band",
    )(rel_bias, bmap)


def _split3(s):
    hi = s.astype(bf16)
    r1 = s - hi.astype(f32)
    mid = r1.astype(bf16)
    lo = (r1 - mid.astype(f32)).astype(bf16)
    return hi, mid, lo


def _proj_a_kernel(x_ref, g_ref, w_ref, qg_ref, kg_ref, bd_ref,
                   q_out, qi_out, kkt_out, vext_out, widx_out, gate_out):
    h = _rms(x_ref[0], g_ref[...]).astype(bf16)

    def proj(lo, hi):
        return jnp.dot(h, w_ref[:, lo:hi], preferred_element_type=f32)

    zq = proj(0, A_WIDTH)
    bd = bd_ref[...]
    ssq = sum(jnp.dot(t, bd, preferred_element_type=f32) for t in _split3(zq * zq))
    qn = zq * lax.rsqrt(ssq * (1.0 / A_HEAD_DIM) + EPS) * qg_ref[...]
    qn = (qn * (A_HEAD_DIM ** -0.5)).astype(bf16)
    zqi = proj(A_WIDTH, 2 * A_WIDTH).astype(bf16)
    for hd in range(A_HEADS):
        sl = slice(hd * A_HEAD_DIM, (hd + 1) * A_HEAD_DIM)
        q_out[0, hd] = qn[:, sl]
        qi_out[0, hd] = zqi[:, sl]

    gate_out[0] = proj(2 * A_WIDTH, 3 * A_WIDTH)

    zkk = proj(3 * A_WIDTH, 3 * A_WIDTH + LANES)
    lane = lax.broadcasted_iota(jnp.int32, zkk.shape, 1)
    is_k = lane < A_HEAD_DIM
    ssk = jnp.sum(jnp.where(is_k, zkk * zkk, 0.0), axis=-1, keepdims=True)
    kn = zkk * lax.rsqrt(ssk * (1.0 / A_HEAD_DIM) + EPS) * kg_ref[...]
    kk = jnp.where(is_k, kn, zkk)
    kkt_out[0] = kk.T.astype(bf16)

    zvw = proj(3 * A_WIDTH + LANES, 3 * A_WIDTH + 2 * LANES)
    vext = jnp.where(lane < A_HEAD_DIM, zvw, jnp.where(lane == A_HEAD_DIM, 1.0, 0.0))
    vext_out[0] = vext.astype(bf16)
    widx_out[0] = zvw[:, A_HEAD_DIM:A_HEAD_DIM + IDX_HEADS] * (IDX_HEADS ** -0.5 * IDX_DIM ** -0.5)


def _proj_a(x, norm_g, w_a, qg, kg, bd):
    B, S, _ = x.shape
    T = TOKEN_TILE
    n_a = w_a.shape[1]
    const = lambda b, i: (0, 0)
    return pl.pallas_call(
        _proj_a_kernel,
        grid=(B, S // T),
        in_specs=[pl.BlockSpec((1, T, D_MODEL), lambda b, i: (b, i, 0)),
                  pl.BlockSpec((1, D_MODEL), const),
                  pl.BlockSpec((D_MODEL, n_a), const),
                  pl.BlockSpec((1, A_WIDTH), const),
                  pl.BlockSpec((1, LANES), const),
                  pl.BlockSpec((A_WIDTH, A_WIDTH), const)],
        out_specs=[pl.BlockSpec((1, A_HEADS, T, A_HEAD_DIM), lambda b, i: (b, 0, i, 0)),
                   pl.BlockSpec((1, IDX_HEADS, T, IDX_DIM), lambda b, i: (b, 0, i, 0)),
                   pl.BlockSpec((1, LANES, T), lambda b, i: (b, 0, i)),
                   pl.BlockSpec((1, T, LANES), lambda b, i: (b, i, 0)),
                   pl.BlockSpec((1, T, IDX_HEADS), lambda b, i: (b, i, 0)),
                   pl.BlockSpec((1, T, A_WIDTH), lambda b, i: (b, i, 0))],
        out_shape=[jax.ShapeDtypeStruct((B, A_HEADS, S, A_HEAD_DIM), bf16),
                   jax.ShapeDtypeStruct((B, IDX_HEADS, S, IDX_DIM), bf16),
                   jax.ShapeDtypeStruct((B, LANES, S), bf16),
                   jax.ShapeDtypeStruct((B, S, LANES), bf16),
                   jax.ShapeDtypeStruct((B, S, IDX_HEADS), f32),
                   jax.ShapeDtypeStruct((B, S, A_WIDTH), f32)],
        compiler_params=pltpu.CompilerParams(
            dimension_semantics=("arbitrary", "arbitrary"),
            vmem_limit_bytes=VMEM_LIMIT_BYTES),
        name="proj_a",
    )(x, norm_g, w_a, qg, kg, bd)


def _count(pred):
    return jnp.sum(jnp.where(pred, 1.0, 0.0), axis=1, keepdims=True)


def _attn_block(c, q_ref, qi_ref, kkt_ref, vext_ref, widx_ref, gate_ref, band_ref,
                y_out, sc_ref, cut_ref, yh_ref):
    nk = (c + 1) * Q_TILE
    k_top = float(TOPK_MAX)
    kt = kkt_ref[0, 0:A_HEAD_DIM, 0:nk]
    vext = vext_ref[0, 0:nk, :]
    row = lax.broadcasted_iota(jnp.int32, (Q_TILE, nk), 0) + c * Q_TILE
    col = lax.broadcasted_iota(jnp.int32, (Q_TILE, nk), 1)
    causal = col <= row

    if nk > TOPK_MAX:
        kit = kkt_ref[0, A_HEAD_DIM:2 * A_HEAD_DIM, 0:nk]
        widx = widx_ref[0]

        def idx_term(hd):
            s = jnp.dot(qi_ref[0, hd], kit, preferred_element_type=f32)
            return widx[:, hd:hd + 1] * jnp.maximum(s, 0.0)

        sc = idx_term(0)
        for hd in range(1, IDX_HEADS):
            sc = sc + idx_term(hd)
        sc_ref[:, 0:nk] = jnp.where(causal, sc, -jnp.inf)

        def scores():
            return sc_ref[:, 0:nk]

        sc = scores()
        lo0 = jnp.min(jnp.where(causal, sc, jnp.inf), axis=1, keepdims=True)
        hi0 = jnp.max(sc, axis=1, keepdims=True)
        clo0 = (lax.broadcasted_iota(jnp.int32, (Q_TILE, 1), 0) + (c * Q_TILE + 1)).astype(f32)

        def halve(_, st):
            lo, hi, clo = st
            mid = lo + (hi - lo) * 0.5
            cnt = _count(scores() >= mid)
            ge = cnt >= k_top
            return jnp.where(ge, mid, lo), jnp.where(ge, hi, mid), jnp.where(ge, cnt, clo)

        def candidate(lo):
            s = scores()
            a = jnp.min(jnp.where(s >= lo, s, jnp.inf), axis=1, keepdims=True)
            return a, _count(s > a)

        lo, _, clo = lax.fori_loop(0, BISECT_ITERS, halve, (lo0, hi0, clo0), unroll=BISECT_UNROLL)
        a, cgt = candidate(lo)

        def unfinished(st):
            return jnp.max(st[2]) >= k_top

        def refine(st):
            clo, a, cgt = st
            s = scores()
            nxt = jnp.min(jnp.where(s > a, s, jnp.inf), axis=1, keepdims=True)
            todo = cgt >= k_top
            lo = jnp.where(todo, nxt, a)
            clo = jnp.where(todo, cgt, clo)
            a, cgt = candidate(lo)
            return clo, a, cgt

        cge, thr, cgt = lax.while_loop(unfinished, refine, (clo, a, cgt))

        cut_ref[...] = jnp.full((Q_TILE, 1), nk, jnp.int32)

        @pl.when(jnp.max(cge) > k_top)
        def _():
            eq = scores() == thr
            need = k_top - cgt

            def step(_, lh):
                lo_i, hi_i = lh
                mid = lax.shift_right_arithmetic(lo_i + hi_i, 1)
                ok = _count(eq & (col <= mid)) >= need
                return jnp.where(ok, lo_i, mid), jnp.where(ok, mid, hi_i)

            lo_i = jnp.full((Q_TILE, 1), -1, jnp.int32)
            hi_i = jnp.full((Q_TILE, 1), nk - 1, jnp.int32)
            _, hi_i = lax.fori_loop(0, int(math.ceil(math.log2(nk))), step, (lo_i, hi_i))
            cut_ref[...] = hi_i

        s = scores()
        keep = (s > thr) | ((s == thr) & (col <= cut_ref[...]))
    else:
        keep = causal
    sc_ref[:, 0:nk] = jnp.where(keep, 0.0, -jnp.inf)
    near_w = min(nk, 2 * Q_TILE)

    def head(hd, carry):
        logit = jnp.dot(q_ref[0, hd], kt, preferred_element_type=f32)
        near = logit[:, nk - near_w:] + (band_ref[hd][:, 2 * Q_TILE - near_w:] + sc_ref[:, nk - near_w:nk])
        if nk > near_w:
            far = logit[:, :nk - near_w] + sc_ref[:, 0:nk - near_w]
            logit = jnp.concatenate([far, near], axis=1)
        else:
            logit = near
        m = jnp.max(logit, axis=1, keepdims=True)
        p = jnp.exp(logit - m).astype(bf16)
        pv = jnp.dot(p, vext, preferred_element_type=f32)
        yh_ref[hd] = pv[:, 0:A_HEAD_DIM] / pv[:, A_HEAD_DIM:A_HEAD_DIM + 1]
        return carry

    lax.fori_loop(0, A_HEADS, head, 0, unroll=HEAD_UNROLL)
    y = jnp.concatenate([yh_ref[hd] for hd in range(A_HEADS)], axis=1)
    g = gate_ref[0]
    y_out[0] = (y * (g * jax.nn.sigmoid(g))).astype(bf16)


def _attn_kernel(q_ref, qi_ref, kkt_ref, vext_ref, widx_ref, gate_ref, band_ref,
                 y_out, sc_ref, cut_ref, yh_ref):
    i = pl.program_id(0)
    for c in range(kkt_ref.shape[2] // Q_TILE):
        @pl.when(i == c)
        def _(c=c):
            _attn_block(c, q_ref, qi_ref, kkt_ref, vext_ref, widx_ref, gate_ref, band_ref,
                        y_out, sc_ref, cut_ref, yh_ref)


def _attn(q, qi, kkt, vext, widx, gate, band):
    B, _, S, _ = q.shape
    return pl.pallas_call(
        _attn_kernel,
        grid=(S // Q_TILE, B),
        in_specs=[pl.BlockSpec((1, A_HEADS, Q_TILE, A_HEAD_DIM), lambda i, b: (b, 0, i, 0)),
                  pl.BlockSpec((1, IDX_HEADS, Q_TILE, IDX_DIM), lambda i, b: (b, 0, i, 0)),
                  pl.BlockSpec((1, LANES, S), lambda i, b: (b, 0, 0)),
                  pl.BlockSpec((1, S, LANES), lambda i, b: (b, 0, 0)),
                  pl.BlockSpec((1, Q_TILE, IDX_HEADS), lambda i, b: (b, i, 0)),
                  pl.BlockSpec((1, Q_TILE, A_WIDTH), lambda i, b: (b, i, 0)),
                  pl.BlockSpec((A_HEADS, Q_TILE, 2 * Q_TILE), lambda i, b: (0, 0, 0))],
        out_specs=pl.BlockSpec((1, Q_TILE, A_WIDTH), lambda i, b: (b, i, 0)),
        out_shape=jax.ShapeDtypeStruct((B, S, A_WIDTH), bf16),
        scratch_shapes=[pltpu.VMEM((Q_TILE, S), f32),
                        pltpu.VMEM((Q_TILE, 1), jnp.int32),
                        pltpu.VMEM((A_HEADS, Q_TILE, A_HEAD_DIM), f32)],
        compiler_params=pltpu.CompilerParams(
            dimension_semantics=("arbitrary", "arbitrary"),
            vmem_limit_bytes=VMEM_LIMIT_BYTES),
        name="attn",
    )(q, qi, kkt, vext, widx, gate, band)


def _mix_kernel(x_ref, ya_ref, g_ref, w_ref, lng_ref, lnb_ref, ws_ref, bst_ref, wbr_ref, wo_ref,
                out_ref):
    x = x_ref[...]
    h = _rms(x, g_ref[...]).astype(bf16)
    T = x.shape[0]

    def proj(lo, hi):
        return jnp.dot(h, w_ref[:, lo:hi], preferred_element_type=f32)

    u = jax.nn.gelu(proj(0, B_WIDTH))
    vb = jax.nn.gelu(proj(B_WIDTH, 2 * B_WIDTH))
    mu = jnp.mean(vb, axis=-1, keepdims=True)
    var = jnp.mean(jnp.square(vb - mu), axis=-1, keepdims=True)
    vln = ((vb - mu) * lax.rsqrt(var + EPS) * lng_ref[...] + lnb_ref[...]).astype(bf16)

    n_ch = T // CHUNK
    tril = (lax.broadcasted_iota(jnp.int32, (CHUNK, CHUNK), 1)
            <= lax.broadcasted_iota(jnp.int32, (CHUNK, CHUNK), 0))
    bst = bst_ref[...]
    per_group = []
    for gi in range(B_GROUPS):
        wg = jnp.where(tril, ws_ref[gi], 0.0).astype(bf16)
        cols = slice(gi * B_GROUP_DIM, (gi + 1) * B_GROUP_DIM)
        vg = jnp.concatenate([vln[ci * CHUNK:(ci + 1) * CHUNK, cols] for ci in range(n_ch)], axis=1)
        sg = jnp.dot(wg, vg, preferred_element_type=f32) + bst[:, gi:gi + 1]
        per_group.append(sg)
    s = jnp.concatenate(
        [jnp.concatenate([per_group[gi][:, ci * B_GROUP_DIM:(ci + 1) * B_GROUP_DIM]
                          for gi in range(B_GROUPS)], axis=1)
         for ci in range(n_ch)], axis=0)

    gb = proj(2 * B_WIDTH, 3 * B_WIDTH)
    yb = (u * s * (gb * jax.nn.sigmoid(gb))).astype(bf16)
    yd_a = jnp.dot(ya_ref[...], wbr_ref[0], preferred_element_type=f32)
    yd_b = jnp.dot(yb, wbr_ref[1], preferred_element_type=f32)
    ma = proj(3 * B_WIDTH, 3 * B_WIDTH + D_MODEL)
    mb = proj(3 * B_WIDTH + D_MODEL, 3 * B_WIDTH + 2 * D_MODEL)
    merged = jax.nn.sigmoid(ma) * yd_a + jax.nn.sigmoid(mb) * yd_b
    out_ref[...] = x + jnp.dot(merged.astype(bf16), wo_ref[...], preferred_element_type=f32)


def _mix(x2, ya2, norm_g, w_b, ln_g, ln_b, w_sp, b_sp_t, w_br, w_o):
    N = x2.shape[0]
    T = TOKEN_TILE
    n_b = w_b.shape[1]
    c2 = lambda i: (0, 0)
    c3 = lambda i: (0, 0, 0)
    return pl.pallas_call(
        _mix_kernel,
        grid=(N // T,),
        in_specs=[pl.BlockSpec((T, D_MODEL), lambda i: (i, 0)),
                  pl.BlockSpec((T, A_WIDTH), lambda i: (i, 0)),
                  pl.BlockSpec((1, D_MODEL), c2),
                  pl.BlockSpec((D_MODEL, n_b), c2),
                  pl.BlockSpec((1, B_WIDTH), c2),
                  pl.BlockSpec((1, B_WIDTH), c2),
                  pl.BlockSpec((B_GROUPS, CHUNK, CHUNK), c3),
                  pl.BlockSpec((CHUNK, B_GROUPS), c2),
                  pl.BlockSpec((2, A_WIDTH, D_MODEL), c3),
                  pl.BlockSpec((D_MODEL, D_MODEL), c2)],
        out_specs=pl.BlockSpec((T, D_MODEL), lambda i: (i, 0)),
        out_shape=jax.ShapeDtypeStruct((N, D_MODEL), f32),
        compiler_params=pltpu.CompilerParams(
            dimension_semantics=("arbitrary",),
            vmem_limit_bytes=VMEM_LIMIT_BYTES),
        name="mix",
    )(x2, ya2, norm_g, w_b, ln_g, ln_b, w_sp, b_sp_t, w_br, w_o)


def _attn_side_weights(w):
    pad = jnp.zeros((D_MODEL, LANES - A_HEAD_DIM - IDX_HEADS), w.dtype)
    return jnp.concatenate(
        [w[:, _Q0:_K0], w[:, _QI0:_KI0], w[:, _GA0:_QI0],
         w[:, _K0:_V0], w[:, _KI0:_WI0],
         w[:, _V0:_GA0], w[:, _WI0:_U0], pad], axis=1).astype(bf16)


def kernel(x, norm_g, w_in, q_norm_g, k_norm_g, rel_bias, sgu_ln_g, sgu_ln_b,
           w_spatial, b_spatial, w_branch, w_out):
    B, S, D = x.shape
    depth = w_in.shape[0]
    band = _bias_band(rel_bias)
    head_of = np.arange(A_WIDTH) // A_HEAD_DIM
    bd = jnp.asarray(head_of[:, None] == head_of[None, :], dtype=bf16)
    for l in range(depth):
        w_a = _attn_side_weights(w_in[l])
        w_b = w_in[l][:, _U0:_END].astype(bf16)
        g = norm_g[l][None, :]
        qg = jnp.tile(q_norm_g[l], A_HEADS)[None, :]
        kg = jnp.concatenate([k_norm_g[l], jnp.ones((LANES - A_HEAD_DIM,), f32)])[None, :]
        q, qi, kkt, vext, widx, gate = _proj_a(x, g, w_a, qg, kg, bd)
        ya = _attn(q, qi, kkt, vext, widx, gate, band)
        x2 = _mix(x.reshape(B * S, D), ya.reshape(B * S, A_WIDTH), g, w_b,
                  sgu_ln_g[l][None, :], sgu_ln_b[l][None, :], w_spatial[l],
                  b_spatial[l].T, w_branch[l].astype(bf16), w_out[l].astype(bf16))
        x = x2.reshape(B, S, D)
    return x
```

```python
import functools
import math

import numpy as np
import jax
import jax.numpy as jnp
from jax import lax
from jax.experimental import pallas as pl
from jax.experimental.pallas import tpu as pltpu

D_MODEL = 1024
A_HEADS = 8
A_HEAD_DIM = 64
A_WIDTH = A_HEADS * A_HEAD_DIM
IDX_HEADS = 8
IDX_DIM = 64
TOPK_MAX = 256
B_GROUPS = 4
B_GROUP_DIM = 128
B_WIDTH = B_GROUPS * B_GROUP_DIM
CHUNK = 128
REL_BUCKETS = 32
REL_MAX_DIST = 128
EPS = 1e-6

LANES = 128
VMEM_LIMIT_BYTES = 56 * 1024 * 1024

TOKEN_TILE = 512
Q_TILE = 256
BISECT_ITERS = 24
BISECT_UNROLL = 1
HEAD_UNROLL = 1

_SPLITS = (A_WIDTH, A_HEAD_DIM, A_HEAD_DIM, A_WIDTH, IDX_HEADS * IDX_DIM, IDX_DIM,
           IDX_HEADS, B_WIDTH, B_WIDTH, B_WIDTH, D_MODEL, D_MODEL)
_OFFS = np.concatenate([[0], np.cumsum(_SPLITS)])
(_Q0, _K0, _V0, _GA0, _QI0, _KI0, _WI0, _U0, _VB0, _GB0, _MA0, _MB0, _END) = [int(o) for o in _OFFS]

f32 = jnp.float32
bf16 = jnp.bfloat16


def _t5_bucket_np(rel):
    max_exact = REL_BUCKETS // 2
    nf = np.maximum(rel, 1).astype(np.float32)
    large = max_exact + (np.log(nf / np.float32(max_exact))
                         / np.float32(math.log(REL_MAX_DIST / max_exact))
                         * np.float32(REL_BUCKETS - max_exact)).astype(np.int32)
    large = np.minimum(large, REL_BUCKETS - 1)
    return np.where(rel < max_exact, rel, large).astype(np.int32)


def _near_bucket_map():
    tq = np.arange(Q_TILE)[:, None]
    tk = np.arange(2 * Q_TILE)[None, :]
    rel = Q_TILE + tq - tk
    return _t5_bucket_np(np.maximum(rel, 0))


def _rms(x, g):
    ms = jnp.mean(x * x, axis=-1, keepdims=True)
    return x * lax.rsqrt(ms + EPS) * g


def _bias_band_kernel(rb_ref, bmap_ref, out_ref):
    h = pl.program_id(0)
    bmap = bmap_ref[...]
    acc = jnp.zeros(bmap.shape, f32)
    for b in range(REL_BUCKETS):
        acc = jnp.where(bmap == b, rb_ref[b, h], acc)
    out_ref[0] = acc - rb_ref[REL_BUCKETS - 1, h]


def _bias_band(rel_bias):
    bmap = jnp.asarray(_near_bucket_map())
    return pl.pallas_call(
        _bias_band_kernel,
        grid=(A_HEADS,),
        in_specs=[pl.BlockSpec(memory_space=pltpu.SMEM),
                  pl.BlockSpec((Q_TILE, 2 * Q_TILE), lambda h: (0, 0))],
        out_specs=pl.BlockSpec((1, Q_TILE, 2 * Q_TILE), lambda h: (h, 0, 0)),
        out_shape=jax.ShapeDtypeStruct((A_HEADS, Q_TILE, 2 * Q_TILE), f32),
        name="bias_band",
    )(rel_bias, bmap)


def _split3(s):
    hi = s.astype(bf16)
    r1 = s - hi.astype(f32)
    mid = r1.astype(bf16)
    lo = (r1 - mid.astype(f32)).astype(bf16)
    return hi, mid, lo


def _proj_a_kernel(x_ref, g_ref, w_ref, qg_ref, kg_ref, bd_ref,
                   q_out, qi_out, kkt_out, vext_out, widx_out, gate_out):
    h = _rms(x_ref[0], g_ref[...]).astype(bf16)

    def proj(lo, hi):
        return jnp.dot(h, w_ref[:, lo:hi], preferred_element_type=f32)

    zq = proj(0, A_WIDTH)
    bd = bd_ref[...]
    ssq = sum(jnp.dot(t, bd, preferred_element_type=f32) for t in _split3(zq * zq))
    qn = zq * lax.rsqrt(ssq * (1.0 / A_HEAD_DIM) + EPS) * qg_ref[...]
    qn = (qn * (A_HEAD_DIM ** -0.5)).astype(bf16)
    zqi = proj(A_WIDTH, 2 * A_WIDTH).astype(bf16)
    for hd in range(A_HEADS):
        sl = slice(hd * A_HEAD_DIM, (hd + 1) * A_HEAD_DIM)
        q_out[0, hd] = qn[:, sl]
        qi_out[0, hd] = zqi[:, sl]

    gate_out[0] = proj(2 * A_WIDTH, 3 * A_WIDTH)

    zkk = proj(3 * A_WIDTH, 3 * A_WIDTH + LANES)
    lane = lax.broadcasted_iota(jnp.int32, zkk.shape, 1)
    is_k = lane < A_HEAD_DIM
    ssk = jnp.sum(jnp.where(is_k, zkk * zkk, 0.0), axis=-1, keepdims=True)
    kn = zkk * lax.rsqrt(ssk * (1.0 / A_HEAD_DIM) + EPS) * kg_ref[...]
    kk = jnp.where(is_k, kn, zkk)
    kkt_out[0] = kk.T.astype(bf16)

    zvw = proj(3 * A_WIDTH + LANES, 3 * A_WIDTH + 2 * LANES)
    vext = jnp.where(lane < A_HEAD_DIM, zvw, jnp.where(lane == A_HEAD_DIM, 1.0, 0.0))
    vext_out[0] = vext.astype(bf16)
    widx_out[0] = zvw[:, A_HEAD_DIM:A_HEAD_DIM + IDX_HEADS] * (IDX_HEADS ** -0.5 * IDX_DIM ** -0.5)


def _proj_a(x, norm_g, w_a, qg, kg, bd):
    B, S, _ = x.shape
    T = TOKEN_TILE
    n_a = w_a.shape[1]
    const = lambda b, i: (0, 0)
    return pl.pallas_call(
        _proj_a_kernel,
        grid=(B, S // T),
        in_specs=[pl.BlockSpec((1, T, D_MODEL), lambda b, i: (b, i, 0)),
                  pl.BlockSpec((1, D_MODEL), const),
                  pl.BlockSpec((D_MODEL, n_a), const),
                  pl.BlockSpec((1, A_WIDTH), const),
                  pl.BlockSpec((1, LANES), const),
                  pl.BlockSpec((A_WIDTH, A_WIDTH), const)],
        out_specs=[pl.BlockSpec((1, A_HEADS, T, A_HEAD_DIM), lambda b, i: (b, 0, i, 0)),
                   pl.BlockSpec((1, IDX_HEADS, T, IDX_DIM), lambda b, i: (b, 0, i, 0)),
                   pl.BlockSpec((1, LANES, T), lambda b, i: (b, 0, i)),
                   pl.BlockSpec((1, T, LANES), lambda b, i: (b, i, 0)),
                   pl.BlockSpec((1, T, IDX_HEADS), lambda b, i: (b, i, 0)),
                   pl.BlockSpec((1, T, A_WIDTH), lambda b, i: (b, i, 0))],
        out_shape=[jax.ShapeDtypeStruct((B, A_HEADS, S, A_HEAD_DIM), bf16),
                   jax.ShapeDtypeStruct((B, IDX_HEADS, S, IDX_DIM), bf16),
                   jax.ShapeDtypeStruct((B, LANES, S), bf16),
                   jax.ShapeDtypeStruct((B, S, LANES), bf16),
                   jax.ShapeDtypeStruct((B, S, IDX_HEADS), f32),
                   jax.ShapeDtypeStruct((B, S, A_WIDTH), f32)],
        compiler_params=pltpu.CompilerParams(
            dimension_semantics=("arbitrary", "arbitrary"),
            vmem_limit_bytes=VMEM_LIMIT_BYTES),
        name="proj_a",
    )(x, norm_g, w_a, qg, kg, bd)


def _count(pred):
    return jnp.sum(jnp.where(pred, 1.0, 0.0), axis=1, keepdims=True)


def _attn_block(c, q_ref, qi_ref, kkt_ref, vext_ref, widx_ref, gate_ref, band_ref,
                y_out, sc_ref, cut_ref, yh_ref):
    nk = (c + 1) * Q_TILE
    k_top = float(TOPK_MAX)
    kt = kkt_ref[0, 0:A_HEAD_DIM, 0:nk]
    vext = vext_ref[0, 0:nk, :]
    row = lax.broadcasted_iota(jnp.int32, (Q_TILE, nk), 0) + c * Q_TILE
    col = lax.broadcasted_iota(jnp.int32, (Q_TILE, nk), 1)
    causal = col <= row

    if nk > TOPK_MAX:
        kit = kkt_ref[0, A_HEAD_DIM:2 * A_HEAD_DIM, 0:nk]
        widx = widx_ref[0]

        def idx_term(hd):
            s = jnp.dot(qi_ref[0, hd], kit, preferred_element_type=f32)
            return widx[:, hd:hd + 1] * jnp.maximum(s, 0.0)

        sc = idx_term(0)
        for hd in range(1, IDX_HEADS):
            sc = sc + idx_term(hd)
        sc_ref[:, 0:nk] = jnp.where(causal, sc, -jnp.inf)

        def scores():
            return sc_ref[:, 0:nk]

        sc = scores()
        lo0 = jnp.min(jnp.where(causal, sc, jnp.inf), axis=1, keepdims=True)
        hi0 = jnp.max(sc, axis=1, keepdims=True)
        clo0 = (lax.broadcasted_iota(jnp.int32, (Q_TILE, 1), 0) + (c * Q_TILE + 1)).astype(f32)

        def halve(_, st):
            lo, hi, clo = st
            mid = lo + (hi - lo) * 0.5
            cnt = _count(scores() >= mid)
            ge = cnt >= k_top
            return jnp.where(ge, mid, lo), jnp.where(ge, hi, mid), jnp.where(ge, cnt, clo)

        def candidate(lo):
            s = scores()
            a = jnp.min(jnp.where(s >= lo, s, jnp.inf), axis=1, keepdims=True)
            return a, _count(s > a)

        lo, _, clo = lax.fori_loop(0, BISECT_ITERS, halve, (lo0, hi0, clo0), unroll=BISECT_UNROLL)
        a, cgt = candidate(lo)

        def unfinished(st):
            return jnp.max(st[2]) >= k_top

        def refine(st):
            clo, a, cgt = st
            s = scores()
            nxt = jnp.min(jnp.where(s > a, s, jnp.inf), axis=1, keepdims=True)
            todo = cgt >= k_top
            lo = jnp.where(todo, nxt, a)
            clo = jnp.where(todo, cgt, clo)
            a, cgt = candidate(lo)
            return clo, a, cgt

        cge, thr, cgt = lax.while_loop(unfinished, refine, (clo, a, cgt))

        cut_ref[...] = jnp.full((Q_TILE, 1), nk, jnp.int32)

        @pl.when(jnp.max(cge) > k_top)
        def _():
            eq = scores() == thr
            need = k_top - cgt

            def step(_, lh):
                lo_i, hi_i = lh
                mid = lax.shift_right_arithmetic(lo_i + hi_i, 1)
                ok = _count(eq & (col <= mid)) >= need
                return jnp.where(ok, lo_i, mid), jnp.where(ok, mid, hi_i)

            lo_i = jnp.full((Q_TILE, 1), -1, jnp.int32)
            hi_i = jnp.full((Q_TILE, 1), nk - 1, jnp.int32)
            _, hi_i = lax.fori_loop(0, int(math.ceil(math.log2(nk))), step, (lo_i, hi_i))
            cut_ref[...] = hi_i

        s = scores()
        keep = (s > thr) | ((s == thr) & (col <= cut_ref[...]))
    else:
        keep = causal
    sc_ref[:, 0:nk] = jnp.where(keep, 0.0, -jnp.inf)
    near_w = min(nk, 2 * Q_TILE)

    def head(hd, carry):
        logit = jnp.dot(q_ref[0, hd], kt, preferred_element_type=f32)
        near = logit[:, nk - near_w:] + (band_ref[hd][:, 2 * Q_TILE - near_w:] + sc_ref[:, nk - near_w:nk])
        if nk > near_w:
            far = logit[:, :nk - near_w] + sc_ref[:, 0:nk - near_w]
            logit = jnp.concatenate([far, near], axis=1)
        else:
            logit = near
        m = jnp.max(logit, axis=1, keepdims=True)
        p = jnp.exp(logit - m).astype(bf16)
        pv = jnp.dot(p, vext, preferred_element_type=f32)
        yh_ref[hd] = pv[:, 0:A_HEAD_DIM] / pv[:, A_HEAD_DIM:A_HEAD_DIM + 1]
        return carry

    lax.fori_loop(0, A_HEADS, head, 0, unroll=HEAD_UNROLL)
    y = jnp.concatenate([yh_ref[hd] for hd in range(A_HEADS)], axis=1)
    g = gate_ref[0]
    y_out[0] = (y * (g * jax.nn.sigmoid(g))).astype(bf16)


def _attn_kernel(q_ref, qi_ref, kkt_ref, vext_ref, widx_ref, gate_ref, band_ref,
                 y_out, sc_ref, cut_ref, yh_ref):
    i = pl.program_id(0)
    for c in range(kkt_ref.shape[2] // Q_TILE):
        @pl.when(i == c)
        def _(c=c):
            _attn_block(c, q_ref, qi_ref, kkt_ref, vext_ref, widx_ref, gate_ref, band_ref,
                        y_out, sc_ref, cut_ref, yh_ref)


def _attn(q, qi, kkt, vext, widx, gate, band):
    B, _, S, _ = q.shape
    return pl.pallas_call(
        _attn_kernel,
        grid=(S // Q_TILE, B),
        in_specs=[pl.BlockSpec((1, A_HEADS, Q_TILE, A_HEAD_DIM), lambda i, b: (b, 0, i, 0)),
                  pl.BlockSpec((1, IDX_HEADS, Q_TILE, IDX_DIM), lambda i, b: (b, 0, i, 0)),
                  pl.BlockSpec((1, LANES, S), lambda i, b: (b, 0, 0)),
                  pl.BlockSpec((1, S, LANES), lambda i, b: (b, 0, 0)),
                  pl.BlockSpec((1, Q_TILE, IDX_HEADS), lambda i, b: (b, i, 0)),
                  pl.BlockSpec((1, Q_TILE, A_WIDTH), lambda i, b: (b, i, 0)),
                  pl.BlockSpec((A_HEADS, Q_TILE, 2 * Q_TILE), lambda i, b: (0, 0, 0))],
        out_specs=pl.BlockSpec((1, Q_TILE, A_WIDTH), lambda i, b: (b, i, 0)),
        out_shape=jax.ShapeDtypeStruct((B, S, A_WIDTH), bf16),
        scratch_shapes=[pltpu.VMEM((Q_TILE, S), f32),
                        pltpu.VMEM((Q_TILE, 1), jnp.int32),
                        pltpu.VMEM((A_HEADS, Q_TILE, A_HEAD_DIM), f32)],
        compiler_params=pltpu.CompilerParams(
            dimension_semantics=("arbitrary", "arbitrary"),
            vmem_limit_bytes=VMEM_LIMIT_BYTES),
        name="attn",
    )(q, qi, kkt, vext, widx, gate, band)


def _mix_kernel(x_ref, ya_ref, g_ref, w_ref, lng_ref, lnb_ref, ws_ref, bst_ref, wbr_ref, wo_ref,
                out_ref):
    x = x_ref[...]
    h = _rms(x, g_ref[...]).astype(bf16)
    T = x.shape[0]

    def proj(lo, hi):
        return jnp.dot(h, w_ref[:, lo:hi], preferred_element_type=f32)

    u = jax.nn.gelu(proj(0, B_WIDTH))
    vb = jax.nn.gelu(proj(B_WIDTH, 2 * B_WIDTH))
    mu = jnp.mean(vb, axis=-1, keepdims=True)
    var = jnp.mean(jnp.square(vb - mu), axis=-1, keepdims=True)
    vln = ((vb - mu) * lax.rsqrt(var + EPS) * lng_ref[...] + lnb_ref[...]).astype(bf16)

    n_ch = T // CHUNK
    tril = (lax.broadcasted_iota(jnp.int32, (CHUNK, CHUNK), 1)
            <= lax.broadcasted_iota(jnp.int32, (CHUNK, CHUNK), 0))
    bst = bst_ref[...]
    per_group = []
    for gi in range(B_GROUPS):
        wg = jnp.where(tril, ws_ref[gi], 0.0).astype(bf16)
        cols = slice(gi * B_GROUP_DIM, (gi + 1) * B_GROUP_DIM)
        vg = jnp.concatenate([vln[ci * CHUNK:(ci + 1) * CHUNK, cols] for ci in range(n_ch)], axis=1)
        sg = jnp.dot(wg, vg, preferred_element_type=f32) + bst[:, gi:gi + 1]
        per_group.append(sg)
    s = jnp.concatenate(
        [jnp.concatenate([per_group[gi][:, ci * B_GROUP_DIM:(ci + 1) * B_GROUP_DIM]
                          for gi in range(B_GROUPS)], axis=1)
         for ci in range(n_ch)], axis=0)

    gb = proj(2 * B_WIDTH, 3 * B_WIDTH)
    yb = (u * s * (gb * jax.nn.sigmoid(gb))).astype(bf16)
    yd_a = jnp.dot(ya_ref[...], wbr_ref[0], preferred_element_type=f32)
    yd_b = jnp.dot(yb, wbr_ref[1], preferred_element_type=f32)
    ma = proj(3 * B_WIDTH, 3 * B_WIDTH + D_MODEL)
    mb = proj(3 * B_WIDTH + D_MODEL, 3 * B_WIDTH + 2 * D_MODEL)
    merged = jax.nn.sigmoid(ma) * yd_a + jax.nn.sigmoid(mb) * yd_b
    out_ref[...] = x + jnp.dot(merged.astype(bf16), wo_ref[...], preferred_element_type=f32)


def _mix(x2, ya2, norm_g, w_b, ln_g, ln_b, w_sp, b_sp_t, w_br, w_o):
    N = x2.shape[0]
    T = TOKEN_TILE
    n_b = w_b.shape[1]
    c2 = lambda i: (0, 0)
    c3 = lambda i: (0, 0, 0)
    return pl.pallas_call(
        _mix_kernel,
        grid=(N // T,),
        in_specs=[pl.BlockSpec((T, D_MODEL), lambda i: (i, 0)),
                  pl.BlockSpec((T, A_WIDTH), lambda i: (i, 0)),
                  pl.BlockSpec((1, D_MODEL), c2),
                  pl.BlockSpec((D_MODEL, n_b), c2),
                  pl.BlockSpec((1, B_WIDTH), c2),
                  pl.BlockSpec((1, B_WIDTH), c2),
                  pl.BlockSpec((B_GROUPS, CHUNK, CHUNK), c3),
                  pl.BlockSpec((CHUNK, B_GROUPS), c2),
                  pl.BlockSpec((2, A_WIDTH, D_MODEL), c3),
                  pl.BlockSpec((D_MODEL, D_MODEL), c2)],
        out_specs=pl.BlockSpec((T, D_MODEL), lambda i: (i, 0)),
        out_shape=jax.ShapeDtypeStruct((N, D_MODEL), f32),
        compiler_params=pltpu.CompilerParams(
            dimension_semantics=("arbitrary",),
            vmem_limit_bytes=VMEM_LIMIT_BYTES),
        name="mix",
    )(x2, ya2, norm_g, w_b, ln_g, ln_b, w_sp, b_sp_t, w_br, w_o)


def _attn_side_weights(w):
    pad = jnp.zeros((D_MODEL, LANES - A_HEAD_DIM - IDX_HEADS), w.dtype)
    return jnp.concatenate(
        [w[:, _Q0:_K0], w[:, _QI0:_KI0], w[:, _GA0:_QI0],
         w[:, _K0:_V0], w[:, _KI0:_WI0],
         w[:, _V0:_GA0], w[:, _WI0:_U0], pad], axis=1).astype(bf16)


def kernel(x, norm_g, w_in, q_norm_g, k_norm_g, rel_bias, sgu_ln_g, sgu_ln_b,
           w_spatial, b_spatial, w_branch, w_out):
    B, S, D = x.shape
    depth = w_in.shape[0]
    band = _bias_band(rel_bias)
    head_of = np.arange(A_WIDTH) // A_HEAD_DIM
    bd = jnp.asarray(head_of[:, None] == head_of[None, :], dtype=bf16)
    for l in range(depth):
        w_a = _attn_side_weights(w_in[l])
        w_b = w_in[l][:, _U0:_END].astype(bf16)
        g = norm_g[l][None, :]
        qg = jnp.tile(q_norm_g[l], A_HEADS)[None, :]
        kg = jnp.concatenate([k_norm_g[l], jnp.ones((LANES - A_HEAD_DIM,), f32)])[None, :]
        q, qi, kkt, vext, widx, gate = _proj_a(x, g, w_a, qg, kg, bd)
        ya = _attn(q, qi, kkt, vext, widx, gate, band)
        x2 = _mix(x.reshape(B * S, D), ya.reshape(B * S, A_WIDTH), g, w_b,
                  sgu_ln_g[l][None, :], sgu_ln_b[l][None, :], w_spatial[l],
                  b_spatial[l].T, w_branch[l].astype(bf16), w_out[l].astype(bf16))
        x = x2.reshape(B, S, D)
    return x
```

```python
import functools
import math

import numpy as np
import jax
import jax.numpy as jnp
from jax import lax
from jax.experimental import pallas as pl
from jax.experimental.pallas import tpu as pltpu

D_MODEL = 1024
A_HEADS = 8
A_HEAD_DIM = 64
A_WIDTH = A_HEADS * A_HEAD_DIM
IDX_HEADS = 8
IDX_DIM = 64
TOPK_MAX = 256
B_GROUPS = 4
B_GROUP_DIM = 128
B_WIDTH = B_GROUPS * B_GROUP_DIM
CHUNK = 128
REL_BUCKETS = 32
REL_MAX_DIST = 128
EPS = 1e-6

LANES = 128
VMEM_LIMIT_BYTES = 56 * 1024 * 1024

TOKEN_TILE = 512
Q_TILE = 256
BISECT_ITERS = 24
BISECT_UNROLL = 4
HEAD_UNROLL = 1

_SPLITS = (A_WIDTH, A_HEAD_DIM, A_HEAD_DIM, A_WIDTH, IDX_HEADS * IDX_DIM, IDX_DIM,
           IDX_HEADS, B_WIDTH, B_WIDTH, B_WIDTH, D_MODEL, D_MODEL)
_OFFS = np.concatenate([[0], np.cumsum(_SPLITS)])
(_Q0, _K0, _V0, _GA0, _QI0, _KI0, _WI0, _U0, _VB0, _GB0, _MA0, _MB0, _END) = [int(o) for o in _OFFS]

f32 = jnp.float32
bf16 = jnp.bfloat16


def _t5_bucket_np(rel):
    max_exact = REL_BUCKETS // 2
    nf = np.maximum(rel, 1).astype(np.float32)
    large = max_exact + (np.log(nf / np.float32(max_exact))
                         / np.float32(math.log(REL_MAX_DIST / max_exact))
                         * np.float32(REL_BUCKETS - max_exact)).astype(np.int32)
    large = np.minimum(large, REL_BUCKETS - 1)
    return np.where(rel < max_exact, rel, large).astype(np.int32)


def _near_bucket_map():
    tq = np.arange(Q_TILE)[:, None]
    tk = np.arange(2 * Q_TILE)[None, :]
    rel = Q_TILE + tq - tk
    return _t5_bucket_np(np.maximum(rel, 0))


def _rms(x, g):
    ms = jnp.mean(x * x, axis=-1, keepdims=True)
    return x * lax.rsqrt(ms + EPS) * g


def _bias_band_kernel(rb_ref, bmap_ref, out_ref):
    h = pl.program_id(0)
    bmap = bmap_ref[...]
    acc = jnp.zeros(bmap.shape, f32)
    for b in range(REL_BUCKETS):
        acc = jnp.where(bmap == b, rb_ref[b, h], acc)
    out_ref[0] = acc - rb_ref[REL_BUCKETS - 1, h]


def _bias_band(rel_bias):
    bmap = jnp.asarray(_near_bucket_map())
    return pl.pallas_call(
        _bias_band_kernel,
        grid=(A_HEADS,),
        in_specs=[pl.BlockSpec(memory_space=pltpu.SMEM),
                  pl.BlockSpec((Q_TILE, 2 * Q_TILE), lambda h: (0, 0))],
        out_specs=pl.BlockSpec((1, Q_TILE, 2 * Q_TILE), lambda h: (h, 0, 0)),
        out_shape=jax.ShapeDtypeStruct((A_HEADS, Q_TILE, 2 * Q_TILE), f32),
        name="bias_band",
    )(rel_bias, bmap)


def _split3(s):
    hi = s.astype(bf16)
    r1 = s - hi.astype(f32)
    mid = r1.astype(bf16)
    lo = (r1 - mid.astype(f32)).astype(bf16)
    return hi, mid, lo


def _proj_a_kernel(x_ref, g_ref, w_ref, qg_ref, kg_ref, bd_ref,
                   q_out, qi_out, kkt_out, vext_out, widx_out, gate_out):
    h = _rms(x_ref[0], g_ref[...]).astype(bf16)

    def proj(lo, hi):
        return jnp.dot(h, w_ref[:, lo:hi], preferred_element_type=f32)

    zq = proj(0, A_WIDTH)
    bd = bd_ref[...]
    ssq = sum(jnp.dot(t, bd, preferred_element_type=f32) for t in _split3(zq * zq))
    qn = zq * lax.rsqrt(ssq * (1.0 / A_HEAD_DIM) + EPS) * qg_ref[...]
    qn = (qn * (A_HEAD_DIM ** -0.5)).astype(bf16)
    zqi = proj(A_WIDTH, 2 * A_WIDTH).astype(bf16)
    for hd in range(A_HEADS):
        sl = slice(hd * A_HEAD_DIM, (hd + 1) * A_HEAD_DIM)
        q_out[0, hd] = qn[:, sl]
        qi_out[0, hd] = zqi[:, sl]

    gate_out[0] = proj(2 * A_WIDTH, 3 * A_WIDTH)

    zkk = proj(3 * A_WIDTH, 3 * A_WIDTH + LANES)
    lane = lax.broadcasted_iota(jnp.int32, zkk.shape, 1)
    is_k = lane < A_HEAD_DIM
    ssk = jnp.sum(jnp.where(is_k, zkk * zkk, 0.0), axis=-1, keepdims=True)
    kn = zkk * lax.rsqrt(ssk * (1.0 / A_HEAD_DIM) + EPS) * kg_ref[...]
    kk = jnp.where(is_k, kn, zkk)
    kkt_out[0] = kk.T.astype(bf16)

    zvw = proj(3 * A_WIDTH + LANES, 3 * A_WIDTH + 2 * LANES)
    vext = jnp.where(lane < A_HEAD_DIM, zvw, jnp.where(lane == A_HEAD_DIM, 1.0, 0.0))
    vext_out[0] = vext.astype(bf16)
    widx_out[0] = zvw[:, A_HEAD_DIM:A_HEAD_DIM + IDX_HEADS] * (IDX_HEADS ** -0.5 * IDX_DIM ** -0.5)


def _proj_a(x, norm_g, w_a, qg, kg, bd):
    B, S, _ = x.shape
    T = TOKEN_TILE
    n_a = w_a.shape[1]
    const = lambda b, i: (0, 0)
    return pl.pallas_call(
        _proj_a_kernel,
        grid=(B, S // T),
        in_specs=[pl.BlockSpec((1, T, D_MODEL), lambda b, i: (b, i, 0)),
                  pl.BlockSpec((1, D_MODEL), const),
                  pl.BlockSpec((D_MODEL, n_a), const),
                  pl.BlockSpec((1, A_WIDTH), const),
                  pl.BlockSpec((1, LANES), const),
                  pl.BlockSpec((A_WIDTH, A_WIDTH), const)],
        out_specs=[pl.BlockSpec((1, A_HEADS, T, A_HEAD_DIM), lambda b, i: (b, 0, i, 0)),
                   pl.BlockSpec((1, IDX_HEADS, T, IDX_DIM), lambda b, i: (b, 0, i, 0)),
                   pl.BlockSpec((1, LANES, T), lambda b, i: (b, 0, i)),
                   pl.BlockSpec((1, T, LANES), lambda b, i: (b, i, 0)),
                   pl.BlockSpec((1, T, IDX_HEADS), lambda b, i: (b, i, 0)),
                   pl.BlockSpec((1, T, A_WIDTH), lambda b, i: (b, i, 0))],
        out_shape=[jax.ShapeDtypeStruct((B, A_HEADS, S, A_HEAD_DIM), bf16),
                   jax.ShapeDtypeStruct((B, IDX_HEADS, S, IDX_DIM), bf16),
                   jax.ShapeDtypeStruct((B, LANES, S), bf16),
                   jax.ShapeDtypeStruct((B, S, LANES), bf16),
                   jax.ShapeDtypeStruct((B, S, IDX_HEADS), f32),
                   jax.ShapeDtypeStruct((B, S, A_WIDTH), f32)],
        compiler_params=pltpu.CompilerParams(
            dimension_semantics=("arbitrary", "arbitrary"),
            vmem_limit_bytes=VMEM_LIMIT_BYTES),
        name="proj_a",
    )(x, norm_g, w_a, qg, kg, bd)


def _count(pred):
    return jnp.sum(jnp.where(pred, 1.0, 0.0), axis=1, keepdims=True)


def _attn_block(c, q_ref, qi_ref, kkt_ref, vext_ref, widx_ref, gate_ref, band_ref,
                y_out, sc_ref, cut_ref, yh_ref):
    nk = (c + 1) * Q_TILE
    k_top = float(TOPK_MAX)
    kt = kkt_ref[0, 0:A_HEAD_DIM, 0:nk]
    vext = vext_ref[0, 0:nk, :]
    row = lax.broadcasted_iota(jnp.int32, (Q_TILE, nk), 0) + c * Q_TILE
    col = lax.broadcasted_iota(jnp.int32, (Q_TILE, nk), 1)
    causal = col <= row

    if nk > TOPK_MAX:
        kit = kkt_ref[0, A_HEAD_DIM:2 * A_HEAD_DIM, 0:nk]
        widx = widx_ref[0]

        def idx_term(hd):
            s = jnp.dot(qi_ref[0, hd], kit, preferred_element_type=f32)
            return widx[:, hd:hd + 1] * jnp.maximum(s, 0.0)

        sc = idx_term(0)
        for hd in range(1, IDX_HEADS):
            sc = sc + idx_term(hd)
        sc_ref[:, 0:nk] = jnp.where(causal, sc, -jnp.inf)

        def scores():
            return sc_ref[:, 0:nk]

        sc = scores()
        lo0 = jnp.min(jnp.where(causal, sc, jnp.inf), axis=1, keepdims=True)
        hi0 = jnp.max(sc, axis=1, keepdims=True)
        clo0 = (lax.broadcasted_iota(jnp.int32, (Q_TILE, 1), 0) + (c * Q_TILE + 1)).astype(f32)

        def halve(_, st):
            lo, hi, clo = st
            mid = lo + (hi - lo) * 0.5
            cnt = _count(scores() >= mid)
            ge = cnt >= k_top
            return jnp.where(ge, mid, lo), jnp.where(ge, hi, mid), jnp.where(ge, cnt, clo)

        def candidate(lo):
            s = scores()
            a = jnp.min(jnp.where(s >= lo, s, jnp.inf), axis=1, keepdims=True)
            return a, _count(s > a)

        lo, _, clo = lax.fori_loop(0, BISECT_ITERS, halve, (lo0, hi0, clo0), unroll=BISECT_UNROLL)
        a, cgt = candidate(lo)

        def unfinished(st):
            return jnp.max(st[2]) >= k_top

        def refine(st):
            clo, a, cgt = st
            s = scores()
            nxt = jnp.min(jnp.where(s > a, s, jnp.inf), axis=1, keepdims=True)
            todo = cgt >= k_top
            lo = jnp.where(todo, nxt, a)
            clo = jnp.where(todo, cgt, clo)
            a, cgt = candidate(lo)
            return clo, a, cgt

        cge, thr, cgt = lax.while_loop(unfinished, refine, (clo, a, cgt))

        cut_ref[...] = jnp.full((Q_TILE, 1), nk, jnp.int32)

        @pl.when(jnp.max(cge) > k_top)
        def _():
            eq = scores() == thr
            need = k_top - cgt

            def step(_, lh):
                lo_i, hi_i = lh
                mid = lax.shift_right_arithmetic(lo_i + hi_i, 1)
                ok = _count(eq & (col <= mid)) >= need
                return jnp.where(ok, lo_i, mid), jnp.where(ok, mid, hi_i)

            lo_i = jnp.full((Q_TILE, 1), -1, jnp.int32)
            hi_i = jnp.full((Q_TILE, 1), nk - 1, jnp.int32)
            _, hi_i = lax.fori_loop(0, int(math.ceil(math.log2(nk))), step, (lo_i, hi_i))
            cut_ref[...] = hi_i

        s = scores()
        keep = (s > thr) | ((s == thr) & (col <= cut_ref[...]))
    else:
        keep = causal
    sc_ref[:, 0:nk] = jnp.where(keep, 0.0, -jnp.inf)
    near_w = min(nk, 2 * Q_TILE)

    def head(hd, carry):
        logit = jnp.dot(q_ref[0, hd], kt, preferred_element_type=f32)
        near = logit[:, nk - near_w:] + (band_ref[hd][:, 2 * Q_TILE - near_w:] + sc_ref[:, nk - near_w:nk])
        if nk > near_w:
            far = logit[:, :nk - near_w] + sc_ref[:, 0:nk - near_w]
            logit = jnp.concatenate([far, near], axis=1)
        else:
            logit = near
        m = jnp.max(logit, axis=1, keepdims=True)
        p = jnp.exp(logit - m).astype(bf16)
        pv = jnp.dot(p, vext, preferred_element_type=f32)
        yh_ref[hd] = pv[:, 0:A_HEAD_DIM] / pv[:, A_HEAD_DIM:A_HEAD_DIM + 1]
        return carry

    lax.fori_loop(0, A_HEADS, head, 0, unroll=HEAD_UNROLL)
    y = jnp.concatenate([yh_ref[hd] for hd in range(A_HEADS)], axis=1)
    g = gate_ref[0]
    y_out[0] = (y * (g * jax.nn.sigmoid(g))).astype(bf16)


def _attn_kernel(q_ref, qi_ref, kkt_ref, vext_ref, widx_ref, gate_ref, band_ref,
                 y_out, sc_ref, cut_ref, yh_ref):
    i = pl.program_id(0)
    for c in range(kkt_ref.shape[2] // Q_TILE):
        @pl.when(i == c)
        def _(c=c):
            _attn_block(c, q_ref, qi_ref, kkt_ref, vext_ref, widx_ref, gate_ref, band_ref,
                        y_out, sc_ref, cut_ref, yh_ref)


def _attn(q, qi, kkt, vext, widx, gate, band):
    B, _, S, _ = q.shape
    return pl.pallas_call(
        _attn_kernel,
        grid=(S // Q_TILE, B),
        in_specs=[pl.BlockSpec((1, A_HEADS, Q_TILE, A_HEAD_DIM), lambda i, b: (b, 0, i, 0)),
                  pl.BlockSpec((1, IDX_HEADS, Q_TILE, IDX_DIM), lambda i, b: (b, 0, i, 0)),
                  pl.BlockSpec((1, LANES, S), lambda i, b: (b, 0, 0)),
                  pl.BlockSpec((1, S, LANES), lambda i, b: (b, 0, 0)),
                  pl.BlockSpec((1, Q_TILE, IDX_HEADS), lambda i, b: (b, i, 0)),
                  pl.BlockSpec((1, Q_TILE, A_WIDTH), lambda i, b: (b, i, 0)),
                  pl.BlockSpec((A_HEADS, Q_TILE, 2 * Q_TILE), lambda i, b: (0, 0, 0))],
        out_specs=pl.BlockSpec((1, Q_TILE, A_WIDTH), lambda i, b: (b, i, 0)),
        out_shape=jax.ShapeDtypeStruct((B, S, A_WIDTH), bf16),
        scratch_shapes=[pltpu.VMEM((Q_TILE, S), f32),
                        pltpu.VMEM((Q_TILE, 1), jnp.int32),
                        pltpu.VMEM((A_HEADS, Q_TILE, A_HEAD_DIM), f32)],
        compiler_params=pltpu.CompilerParams(
            dimension_semantics=("arbitrary", "arbitrary"),
            vmem_limit_bytes=VMEM_LIMIT_BYTES),
        name="attn",
    )(q, qi, kkt, vext, widx, gate, band)


def _mix_kernel(x_ref, ya_ref, g_ref, w_ref, lng_ref, lnb_ref, ws_ref, bst_ref, wbr_ref, wo_ref,
                out_ref):
    x = x_ref[...]
    h = _rms(x, g_ref[...]).astype(bf16)
    T = x.shape[0]

    def proj(lo, hi):
        return jnp.dot(h, w_ref[:, lo:hi], preferred_element_type=f32)

    u = jax.nn.gelu(proj(0, B_WIDTH))
    vb = jax.nn.gelu(proj(B_WIDTH, 2 * B_WIDTH))
    mu = jnp.mean(vb, axis=-1, keepdims=True)
    var = jnp.mean(jnp.square(vb - mu), axis=-1, keepdims=True)
    vln = ((vb - mu) * lax.rsqrt(var + EPS) * lng_ref[...] + lnb_ref[...]).astype(bf16)

    n_ch = T // CHUNK
    tril = (lax.broadcasted_iota(jnp.int32, (CHUNK, CHUNK), 1)
            <= lax.broadcasted_iota(jnp.int32, (CHUNK, CHUNK), 0))
    bst = bst_ref[...]
    per_group = []
    for gi in range(B_GROUPS):
        wg = jnp.where(tril, ws_ref[gi], 0.0).astype(bf16)
        cols = slice(gi * B_GROUP_DIM, (gi + 1) * B_GROUP_DIM)
        vg = jnp.concatenate([vln[ci * CHUNK:(ci + 1) * CHUNK, cols] for ci in range(n_ch)], axis=1)
        sg = jnp.dot(wg, vg, preferred_element_type=f32) + bst[:, gi:gi + 1]
        per_group.append(sg)
    s = jnp.concatenate(
        [jnp.concatenate([per_group[gi][:, ci * B_GROUP_DIM:(ci + 1) * B_GROUP_DIM]
                          for gi in range(B_GROUPS)], axis=1)
         for ci in range(n_ch)], axis=0)

    gb = proj(2 * B_WIDTH, 3 * B_WIDTH)
    yb = (u * s * (gb * jax.nn.sigmoid(gb))).astype(bf16)
    yd_a = jnp.dot(ya_ref[...], wbr_ref[0], preferred_element_type=f32)
    yd_b = jnp.dot(yb, wbr_ref[1], preferred_element_type=f32)
    ma = proj(3 * B_WIDTH, 3 * B_WIDTH + D_MODEL)
    mb = proj(3 * B_WIDTH + D_MODEL, 3 * B_WIDTH + 2 * D_MODEL)
    merged = jax.nn.sigmoid(ma) * yd_a + jax.nn.sigmoid(mb) * yd_b
    out_ref[...] = x + jnp.dot(merged.astype(bf16), wo_ref[...], preferred_element_type=f32)


def _mix(x2, ya2, norm_g, w_b, ln_g, ln_b, w_sp, b_sp_t, w_br, w_o):
    N = x2.shape[0]
    T = TOKEN_TILE
    n_b = w_b.shape[1]
    c2 = lambda i: (0, 0)
    c3 = lambda i: (0, 0, 0)
    return pl.pallas_call(
        _mix_kernel,
        grid=(N // T,),
        in_specs=[pl.BlockSpec((T, D_MODEL), lambda i: (i, 0)),
                  pl.BlockSpec((T, A_WIDTH), lambda i: (i, 0)),
                  pl.BlockSpec((1, D_MODEL), c2),
                  pl.BlockSpec((D_MODEL, n_b), c2),
                  pl.BlockSpec((1, B_WIDTH), c2),
                  pl.BlockSpec((1, B_WIDTH), c2),
                  pl.BlockSpec((B_GROUPS, CHUNK, CHUNK), c3),
                  pl.BlockSpec((CHUNK, B_GROUPS), c2),
                  pl.BlockSpec((2, A_WIDTH, D_MODEL), c3),
                  pl.BlockSpec((D_MODEL, D_MODEL), c2)],
        out_specs=pl.BlockSpec((T, D_MODEL), lambda i: (i, 0)),
        out_shape=jax.ShapeDtypeStruct((N, D_MODEL), f32),
        compiler_params=pltpu.CompilerParams(
            dimension_semantics=("arbitrary",),
            vmem_limit_bytes=VMEM_LIMIT_BYTES),
        name="mix",
    )(x2, ya2, norm_g, w_b, ln_g, ln_b, w_sp, b_sp_t, w_br, w_o)


def _attn_side_weights(w):
    pad = jnp.zeros((D_MODEL, LANES - A_HEAD_DIM - IDX_HEADS), w.dtype)
    return jnp.concatenate(
        [w[:, _Q0:_K0], w[:, _QI0:_KI0], w[:, _GA0:_QI0],
         w[:, _K0:_V0], w[:, _KI0:_WI0],
         w[:, _V0:_GA0], w[:, _WI0:_U0], pad], axis=1).astype(bf16)


def kernel(x, norm_g, w_in, q_norm_g, k_norm_g, rel_bias, sgu_ln_g, sgu_ln_b,
           w_spatial, b_spatial, w_branch, w_out):
    B, S, D = x.shape
    depth = w_in.shape[0]
    band = _bias_band(rel_bias)
    head_of = np.arange(A_WIDTH) // A_HEAD_DIM
    bd = jnp.asarray(head_of[:, None] == head_of[None, :], dtype=bf16)
    for l in range(depth):
        w_a = _attn_side_weights(w_in[l])
        w_b = w_in[l][:, _U0:_END].astype(bf16)
        g = norm_g[l][None, :]
        qg = jnp.tile(q_norm_g[l], A_HEADS)[None, :]
        kg = jnp.concatenate([k_norm_g[l], jnp.ones((LANES - A_HEAD_DIM,), f32)])[None, :]
        q, qi, kkt, vext, widx, gate = _proj_a(x, g, w_a, qg, kg, bd)
        ya = _attn(q, qi, kkt, vext, widx, gate, band)
        x2 = _mix(x.reshape(B * S, D), ya.reshape(B * S, A_WIDTH), g, w_b,
                  sgu_ln_g[l][None, :], sgu_ln_b[l][None, :], w_spatial[l],
                  b_spatial[l].T, w_branch[l].astype(bf16), w_out[l].astype(bf16))
        x = x2.reshape(B, S, D)
    return x
```

```python
import functools
import math

import numpy as np
import jax
import jax.numpy as jnp
from jax import lax
from jax.experimental import pallas as pl
from jax.experimental.pallas import tpu as pltpu

D_MODEL = 1024
A_HEADS = 8
A_HEAD_DIM = 64
A_WIDTH = A_HEADS * A_HEAD_DIM
IDX_HEADS = 8
IDX_DIM = 64
TOPK_MAX = 256
B_GROUPS = 4
B_GROUP_DIM = 128
B_WIDTH = B_GROUPS * B_GROUP_DIM
CHUNK = 128
REL_BUCKETS = 32
REL_MAX_DIST = 128
EPS = 1e-6

LANES = 128
VMEM_LIMIT_BYTES = 56 * 1024 * 1024

TOKEN_TILE = 512
Q_TILE = 256
K_TILE = 256
BISECT_ITERS = 24
BISECT_UNROLL = 4

_SPLITS = (A_WIDTH, A_HEAD_DIM, A_HEAD_DIM, A_WIDTH, IDX_HEADS * IDX_DIM, IDX_DIM,
           IDX_HEADS, B_WIDTH, B_WIDTH, B_WIDTH, D_MODEL, D_MODEL)
_OFFS = np.concatenate([[0], np.cumsum(_SPLITS)])
(_Q0, _K0, _V0, _GA0, _QI0, _KI0, _WI0, _U0, _VB0, _GB0, _MA0, _MB0, _END) = [int(o) for o in _OFFS]

f32 = jnp.float32
bf16 = jnp.bfloat16


def _t5_bucket_np(rel):
    max_exact = REL_BUCKETS // 2
    nf = np.maximum(rel, 1).astype(np.float32)
    large = max_exact + (np.log(nf / np.float32(max_exact))
                         / np.float32(math.log(REL_MAX_DIST / max_exact))
                         * np.float32(REL_BUCKETS - max_exact)).astype(np.int32)
    large = np.minimum(large, REL_BUCKETS - 1)
    return np.where(rel < max_exact, rel, large).astype(np.int32)


def _near_bucket_map():
    tq = np.arange(Q_TILE)[:, None]
    tk = np.arange(2 * Q_TILE)[None, :]
    rel = Q_TILE + tq - tk
    return _t5_bucket_np(np.maximum(rel, 0))


def _rms(x, g):
    ms = jnp.mean(x * x, axis=-1, keepdims=True)
    return x * lax.rsqrt(ms + EPS) * g


def _bias_band_kernel(rb_ref, bmap_ref, out_ref):
    h = pl.program_id(0)
    bmap = bmap_ref[...]
    acc = jnp.zeros(bmap.shape, f32)
    for b in range(REL_BUCKETS):
        acc = jnp.where(bmap == b, rb_ref[b, h], acc)
    out_ref[0] = acc - rb_ref[REL_BUCKETS - 1, h]


def _bias_band(rel_bias):
    bmap = jnp.asarray(_near_bucket_map())
    return pl.pallas_call(
        _bias_band_kernel,
        grid=(A_HEADS,),
        in_specs=[pl.BlockSpec(memory_space=pltpu.SMEM),
                  pl.BlockSpec((Q_TILE, 2 * Q_TILE), lambda h: (0, 0))],
        out_specs=pl.BlockSpec((1, Q_TILE, 2 * Q_TILE), lambda h: (h, 0, 0)),
        out_shape=jax.ShapeDtypeStruct((A_HEADS, Q_TILE, 2 * Q_TILE), f32),
        name="bias_band",
    )(rel_bias, bmap)


def _split2(s):
    hi = s.astype(bf16)
    lo = (s - hi.astype(f32)).astype(bf16)
    return hi, lo


def _proj_a_kernel(x_ref, g_ref, w_ref, qg_ref, kg_ref, bd_ref,
                   q_out, qi_out, kkt_out, vext_out, widx_out, gate_out):
    h = _rms(x_ref[0], g_ref[...]).astype(bf16)

    def proj(lo, hi):
        return jnp.dot(h, w_ref[:, lo:hi], preferred_element_type=f32)

    zq = proj(0, A_WIDTH)
    bd = bd_ref[...]
    ssq = sum(jnp.dot(t, bd, preferred_element_type=f32) for t in _split2(zq * zq))
    qn = zq * lax.rsqrt(ssq * (1.0 / A_HEAD_DIM) + EPS) * qg_ref[...]
    qn = (qn * (A_HEAD_DIM ** -0.5)).astype(bf16)
    zqi = proj(A_WIDTH, 2 * A_WIDTH).astype(bf16)
    for hd in range(A_HEADS):
        sl = slice(hd * A_HEAD_DIM, (hd + 1) * A_HEAD_DIM)
        q_out[0, hd] = qn[:, sl]
        qi_out[0, hd] = zqi[:, sl]

    gate_out[0] = proj(2 * A_WIDTH, 3 * A_WIDTH)

    zkk = proj(3 * A_WIDTH, 3 * A_WIDTH + LANES)
    lane = lax.broadcasted_iota(jnp.int32, zkk.shape, 1)
    is_k = lane < A_HEAD_DIM
    ssk = jnp.sum(jnp.where(is_k, zkk * zkk, 0.0), axis=-1, keepdims=True)
    kn = zkk * lax.rsqrt(ssk * (1.0 / A_HEAD_DIM) + EPS) * kg_ref[...]
    kk = jnp.where(is_k, kn, zkk)
    kkt_out[0] = kk.T.astype(bf16)

    zvw = proj(3 * A_WIDTH + LANES, 3 * A_WIDTH + 2 * LANES)
    vext = jnp.where(lane < A_HEAD_DIM, zvw, jnp.where(lane == A_HEAD_DIM, 1.0, 0.0))
    vext_out[0] = vext.astype(bf16)
    widx_out[0] = zvw[:, A_HEAD_DIM:A_HEAD_DIM + IDX_HEADS] * (IDX_HEADS ** -0.5 * IDX_DIM ** -0.5)


def _proj_a(x, norm_g, w_a, qg, kg, bd):
    B, S, _ = x.shape
    T = TOKEN_TILE
    n_a = w_a.shape[1]
    const = lambda b, i: (0, 0)
    return pl.pallas_call(
        _proj_a_kernel,
        grid=(B, S // T),
        in_specs=[pl.BlockSpec((1, T, D_MODEL), lambda b, i: (b, i, 0)),
                  pl.BlockSpec((1, D_MODEL), const),
                  pl.BlockSpec((D_MODEL, n_a), const),
                  pl.BlockSpec((1, A_WIDTH), const),
                  pl.BlockSpec((1, LANES), const),
                  pl.BlockSpec((A_WIDTH, A_WIDTH), const)],
        out_specs=[pl.BlockSpec((1, A_HEADS, T, A_HEAD_DIM), lambda b, i: (b, 0, i, 0)),
                   pl.BlockSpec((1, IDX_HEADS, T, IDX_DIM), lambda b, i: (b, 0, i, 0)),
                   pl.BlockSpec((1, LANES, T), lambda b, i: (b, 0, i)),
                   pl.BlockSpec((1, T, LANES), lambda b, i: (b, i, 0)),
                   pl.BlockSpec((1, T, IDX_HEADS), lambda b, i: (b, i, 0)),
                   pl.BlockSpec((1, T, A_WIDTH), lambda b, i: (b, i, 0))],
        out_shape=[jax.ShapeDtypeStruct((B, A_HEADS, S, A_HEAD_DIM), bf16),
                   jax.ShapeDtypeStruct((B, IDX_HEADS, S, IDX_DIM), bf16),
                   jax.ShapeDtypeStruct((B, LANES, S), bf16),
                   jax.ShapeDtypeStruct((B, S, LANES), bf16),
                   jax.ShapeDtypeStruct((B, S, IDX_HEADS), f32),
                   jax.ShapeDtypeStruct((B, S, A_WIDTH), f32)],
        compiler_params=pltpu.CompilerParams(
            dimension_semantics=("arbitrary", "arbitrary"),
            vmem_limit_bytes=VMEM_LIMIT_BYTES),
        name="proj_a",
    )(x, norm_g, w_a, qg, kg, bd)


def _count(pred):
    return jnp.sum(jnp.where(pred, 1.0, 0.0), axis=1, keepdims=True)


def _attn_block(c, q_ref, qi_ref, kkt_ref, vext_ref, widx_ref, gate_ref, band_ref,
                y_out, sc_ref, cut_ref, yh_ref, lg_ref):
    nk = (c + 1) * Q_TILE
    k_top = float(TOPK_MAX)
    kt = kkt_ref[0, 0:A_HEAD_DIM, 0:nk]
    vext = vext_ref[0, 0:nk, :]
    row = lax.broadcasted_iota(jnp.int32, (Q_TILE, nk), 0) + c * Q_TILE
    col = lax.broadcasted_iota(jnp.int32, (Q_TILE, nk), 1)
    causal = col <= row

    if nk > TOPK_MAX:
        kit = kkt_ref[0, A_HEAD_DIM:2 * A_HEAD_DIM, 0:nk]
        widx = widx_ref[0]

        def idx_term(hd):
            s = jnp.dot(qi_ref[0, hd], kit, preferred_element_type=f32)
            return widx[:, hd:hd + 1] * jnp.maximum(s, 0.0)

        sc = idx_term(0)
        for hd in range(1, IDX_HEADS):
            sc = sc + idx_term(hd)
        sc_ref[:, 0:nk] = jnp.where(causal, sc, -jnp.inf)

        def scores():
            return sc_ref[:, 0:nk]

        sc = scores()
        lo0 = jnp.min(jnp.where(causal, sc, jnp.inf), axis=1, keepdims=True)
        hi0 = jnp.max(sc, axis=1, keepdims=True)
        clo0 = (lax.broadcasted_iota(jnp.int32, (Q_TILE, 1), 0) + (c * Q_TILE + 1)).astype(f32)

        def halve(_, st):
            lo, hi, clo = st
            mid = lo + (hi - lo) * 0.5
            cnt = _count(scores() >= mid)
            ge = cnt >= k_top
            return jnp.where(ge, mid, lo), jnp.where(ge, hi, mid), jnp.where(ge, cnt, clo)

        def candidate(lo):
            s = scores()
            a = jnp.min(jnp.where(s >= lo, s, jnp.inf), axis=1, keepdims=True)
            return a, _count(s > a)

        lo, _, clo = lax.fori_loop(0, BISECT_ITERS, halve, (lo0, hi0, clo0), unroll=BISECT_UNROLL)
        a, cgt = candidate(lo)

        def unfinished(st):
            return jnp.max(st[2]) >= k_top

        def refine(st):
            clo, a, cgt = st
            s = scores()
            nxt = jnp.min(jnp.where(s > a, s, jnp.inf), axis=1, keepdims=True)
            todo = cgt >= k_top
            lo = jnp.where(todo, nxt, a)
            clo = jnp.where(todo, cgt, clo)
            a, cgt = candidate(lo)
            return clo, a, cgt

        cge, thr, cgt = lax.while_loop(unfinished, refine, (clo, a, cgt))

        cut_ref[...] = jnp.full((Q_TILE, 1), nk, jnp.int32)

        @pl.when(jnp.max(cge) > k_top)
        def _():
            eq = scores() == thr
            need = k_top - cgt

            def step(_, lh):
                lo_i, hi_i = lh
                mid = lax.shift_right_arithmetic(lo_i + hi_i, 1)
                ok = _count(eq & (col <= mid)) >= need
                return jnp.where(ok, lo_i, mid), jnp.where(ok, mid, hi_i)

            lo_i = jnp.full((Q_TILE, 1), -1, jnp.int32)
            hi_i = jnp.full((Q_TILE, 1), nk - 1, jnp.int32)
            _, hi_i = lax.fori_loop(0, int(math.ceil(math.log2(nk))), step, (lo_i, hi_i))
            cut_ref[...] = hi_i

        s = scores()
        keep = (s > thr) | ((s == thr) & (col <= cut_ref[...]))
    else:
        keep = causal
    sc_ref[:, 0:nk] = jnp.where(keep, 0.0, -jnp.inf)
    n_kt = nk // K_TILE

    def head_step(hd_values, m_values, hd_logits):
        q_h = None if hd_logits is None else q_ref[0, hd_logits]
        mpart = None
        pv = None
        for j in range(n_kt):
            ks = slice(j * K_TILE, (j + 1) * K_TILE)
            if hd_values is not None:
                pj = jnp.exp(lg_ref[:, ks] - m_values).astype(bf16)
                dj = jnp.dot(pj, vext[ks, :], preferred_element_type=f32)
                pv = dj if pv is None else pv + dj
            if hd_logits is not None:
                add = sc_ref[:, ks]
                band_j = j - (n_kt - 2 * Q_TILE // K_TILE)
                if band_j >= 0:
                    add = add + band_ref[hd_logits][:, band_j * K_TILE:(band_j + 1) * K_TILE]
                lj = jnp.dot(q_h, kt[:, ks], preferred_element_type=f32) + add
                lg_ref[:, ks] = lj
                mj = lj[:, 0:LANES]
                for t in range(1, K_TILE // LANES):
                    mj = jnp.maximum(mj, lj[:, t * LANES:(t + 1) * LANES])
                mpart = mj if mpart is None else jnp.maximum(mpart, mj)
        if hd_values is not None:
            yh_ref[hd_values] = pv
        return None if mpart is None else jnp.max(mpart, axis=1, keepdims=True)

    m_last = lax.fori_loop(1, A_HEADS, lambda hd, m_prev: head_step(hd - 1, m_prev, hd),
                           head_step(None, None, 0))
    head_step(A_HEADS - 1, m_last, None)
    ys = []
    for hd in range(A_HEADS):
        pv = yh_ref[hd]
        ys.append(pv[:, 0:A_HEAD_DIM] / pv[:, A_HEAD_DIM:A_HEAD_DIM + 1])
    y = jnp.concatenate(ys, axis=1)
    g = gate_ref[0]
    y_out[0] = (y * (g * jax.nn.sigmoid(g))).astype(bf16)


def _attn_kernel(q_ref, qi_ref, kkt_ref, vext_ref, widx_ref, gate_ref, band_ref,
                 y_out, sc_ref, cut_ref, yh_ref, lg_ref):
    i = pl.program_id(0)
    for c in range(kkt_ref.shape[2] // Q_TILE):
        @pl.when(i == c)
        def _(c=c):
            _attn_block(c, q_ref, qi_ref, kkt_ref, vext_ref, widx_ref, gate_ref, band_ref,
                        y_out, sc_ref, cut_ref, yh_ref, lg_ref)


def _attn(q, qi, kkt, vext, widx, gate, band):
    B, _, S, _ = q.shape
    return pl.pallas_call(
        _attn_kernel,
        grid=(S // Q_TILE, B),
        in_specs=[pl.BlockSpec((1, A_HEADS, Q_TILE, A_HEAD_DIM), lambda i, b: (b, 0, i, 0)),
                  pl.BlockSpec((1, IDX_HEADS, Q_TILE, IDX_DIM), lambda i, b: (b, 0, i, 0)),
                  pl.BlockSpec((1, LANES, S), lambda i, b: (b, 0, 0)),
                  pl.BlockSpec((1, S, LANES), lambda i, b: (b, 0, 0)),
                  pl.BlockSpec((1, Q_TILE, IDX_HEADS), lambda i, b: (b, i, 0)),
                  pl.BlockSpec((1, Q_TILE, A_WIDTH), lambda i, b: (b, i, 0)),
                  pl.BlockSpec((A_HEADS, Q_TILE, 2 * Q_TILE), lambda i, b: (0, 0, 0))],
        out_specs=pl.BlockSpec((1, Q_TILE, A_WIDTH), lambda i, b: (b, i, 0)),
        out_shape=jax.ShapeDtypeStruct((B, S, A_WIDTH), bf16),
        scratch_shapes=[pltpu.VMEM((Q_TILE, S), f32),
                        pltpu.VMEM((Q_TILE, 1), jnp.int32),
                        pltpu.VMEM((A_HEADS, Q_TILE, LANES), f32),
                        pltpu.VMEM((Q_TILE, S), f32)],
        compiler_params=pltpu.CompilerParams(
            dimension_semantics=("arbitrary", "arbitrary"),
            vmem_limit_bytes=VMEM_LIMIT_BYTES),
        name="attn",
    )(q, qi, kkt, vext, widx, gate, band)


def _mix_kernel(x_ref, ya_ref, g_ref, w_ref, lng_ref, lnb_ref, ws_ref, bst_ref, wbr_ref, wo_ref,
                out_ref):
    x = x_ref[...]
    h = _rms(x, g_ref[...]).astype(bf16)
    T = x.shape[0]

    def proj(lo, hi):
        return jnp.dot(h, w_ref[:, lo:hi], preferred_element_type=f32)

    u = jax.nn.gelu(proj(0, B_WIDTH))
    vb = jax.nn.gelu(proj(B_WIDTH, 2 * B_WIDTH))
    mu = jnp.mean(vb, axis=-1, keepdims=True)
    var = jnp.mean(jnp.square(vb - mu), axis=-1, keepdims=True)
    vln = ((vb - mu) * lax.rsqrt(var + EPS) * lng_ref[...] + lnb_ref[...]).astype(bf16)

    n_ch = T // CHUNK
    tril = (lax.broadcasted_iota(jnp.int32, (CHUNK, CHUNK), 1)
            <= lax.broadcasted_iota(jnp.int32, (CHUNK, CHUNK), 0))
    bst = bst_ref[...]
    per_group = []
    for gi in range(B_GROUPS):
        wg = jnp.where(tril, ws_ref[gi], 0.0).astype(bf16)
        cols = slice(gi * B_GROUP_DIM, (gi + 1) * B_GROUP_DIM)
        vg = jnp.concatenate([vln[ci * CHUNK:(ci + 1) * CHUNK, cols] for ci in range(n_ch)], axis=1)
        sg = jnp.dot(wg, vg, preferred_element_type=f32) + bst[:, gi:gi + 1]
        per_group.append(sg)
    s = jnp.concatenate(
        [jnp.concatenate([per_group[gi][:, ci * B_GROUP_DIM:(ci + 1) * B_GROUP_DIM]
                          for gi in range(B_GROUPS)], axis=1)
         for ci in range(n_ch)], axis=0)

    gb = proj(2 * B_WIDTH, 3 * B_WIDTH)
    yb = (u * s * (gb * jax.nn.sigmoid(gb))).astype(bf16)
    yd_a = jnp.dot(ya_ref[...], wbr_ref[0], preferred_element_type=f32)
    yd_b = jnp.dot(yb, wbr_ref[1], preferred_element_type=f32)
    ma = proj(3 * B_WIDTH, 3 * B_WIDTH + D_MODEL)
    mb = proj(3 * B_WIDTH + D_MODEL, 3 * B_WIDTH + 2 * D_MODEL)
    merged = jax.nn.sigmoid(ma) * yd_a + jax.nn.sigmoid(mb) * yd_b
    out_ref[...] = x + jnp.dot(merged.astype(bf16), wo_ref[...], preferred_element_type=f32)


def _mix(x2, ya2, norm_g, w_b, ln_g, ln_b, w_sp, b_sp_t, w_br, w_o):
    N = x2.shape[0]
    T = TOKEN_TILE
    n_b = w_b.shape[1]
    c2 = lambda i: (0, 0)
    c3 = lambda i: (0, 0, 0)
    return pl.pallas_call(
        _mix_kernel,
        grid=(N // T,),
        in_specs=[pl.BlockSpec((T, D_MODEL), lambda i: (i, 0)),
                  pl.BlockSpec((T, A_WIDTH), lambda i: (i, 0)),
                  pl.BlockSpec((1, D_MODEL), c2),
                  pl.BlockSpec((D_MODEL, n_b), c2),
                  pl.BlockSpec((1, B_WIDTH), c2),
                  pl.BlockSpec((1, B_WIDTH), c2),
                  pl.BlockSpec((B_GROUPS, CHUNK, CHUNK), c3),
                  pl.BlockSpec((CHUNK, B_GROUPS), c2),
                  pl.BlockSpec((2, A_WIDTH, D_MODEL), c3),
                  pl.BlockSpec((D_MODEL, D_MODEL), c2)],
        out_specs=pl.BlockSpec((T, D_MODEL), lambda i: (i, 0)),
        out_shape=jax.ShapeDtypeStruct((N, D_MODEL), f32),
        compiler_params=pltpu.CompilerParams(
            dimension_semantics=("arbitrary",),
            vmem_limit_bytes=VMEM_LIMIT_BYTES),
        name="mix",
    )(x2, ya2, norm_g, w_b, ln_g, ln_b, w_sp, b_sp_t, w_br, w_o)


def _attn_side_weights(w):
    pad = jnp.zeros((D_MODEL, LANES - A_HEAD_DIM - IDX_HEADS), w.dtype)
    return jnp.concatenate(
        [w[:, _Q0:_K0], w[:, _QI0:_KI0], w[:, _GA0:_QI0],
         w[:, _K0:_V0], w[:, _KI0:_WI0],
         w[:, _V0:_GA0], w[:, _WI0:_U0], pad], axis=1).astype(bf16)


def kernel(x, norm_g, w_in, q_norm_g, k_norm_g, rel_bias, sgu_ln_g, sgu_ln_b,
           w_spatial, b_spatial, w_branch, w_out):
    B, S, D = x.shape
    depth = w_in.shape[0]
    band = _bias_band(rel_bias)
    head_of = np.arange(A_WIDTH) // A_HEAD_DIM
    bd = jnp.asarray(head_of[:, None] == head_of[None, :], dtype=bf16)
    for l in range(depth):
        w_a = _attn_side_weights(w_in[l])
        w_b = w_in[l][:, _U0:_END].astype(bf16)
        g = norm_g[l][None, :]
        qg = jnp.tile(q_norm_g[l], A_HEADS)[None, :]
        kg = jnp.concatenate([k_norm_g[l], jnp.ones((LANES - A_HEAD_DIM,), f32)])[None, :]
        q, qi, kkt, vext, widx, gate = _proj_a(x, g, w_a, qg, kg, bd)
        ya = _attn(q, qi, kkt, vext, widx, gate, band)
        x2 = _mix(x.reshape(B * S, D), ya.reshape(B * S, A_WIDTH), g, w_b,
                  sgu_ln_g[l][None, :], sgu_ln_b[l][None, :], w_spatial[l],
                  b_spatial[l].T, w_branch[l].astype(bf16), w_out[l].astype(bf16))
        x = x2.reshape(B, S, D)
    return x
```

```python
import functools
import math

import numpy as np
import jax
import jax.numpy as jnp
from jax import lax
from jax.experimental import pallas as pl
from jax.experimental.pallas import tpu as pltpu

D_MODEL = 1024
A_HEADS = 8
A_HEAD_DIM = 64
A_WIDTH = A_HEADS * A_HEAD_DIM
IDX_HEADS = 8
IDX_DIM = 64
TOPK_MAX = 256
B_GROUPS = 4
B_GROUP_DIM = 128
B_WIDTH = B_GROUPS * B_GROUP_DIM
CHUNK = 128
REL_BUCKETS = 32
REL_MAX_DIST = 128
EPS = 1e-6

LANES = 128
VMEM_LIMIT_BYTES = 56 * 1024 * 1024

TOKEN_TILE = 512
Q_TILE = 256
K_TILE = 256
BISECT_ITERS = 24
BISECT_UNROLL = 4

_SPLITS = (A_WIDTH, A_HEAD_DIM, A_HEAD_DIM, A_WIDTH, IDX_HEADS * IDX_DIM, IDX_DIM,
           IDX_HEADS, B_WIDTH, B_WIDTH, B_WIDTH, D_MODEL, D_MODEL)
_OFFS = np.concatenate([[0], np.cumsum(_SPLITS)])
(_Q0, _K0, _V0, _GA0, _QI0, _KI0, _WI0, _U0, _VB0, _GB0, _MA0, _MB0, _END) = [int(o) for o in _OFFS]

f32 = jnp.float32
bf16 = jnp.bfloat16


def _t5_bucket_np(rel):
    max_exact = REL_BUCKETS // 2
    nf = np.maximum(rel, 1).astype(np.float32)
    large = max_exact + (np.log(nf / np.float32(max_exact))
                         / np.float32(math.log(REL_MAX_DIST / max_exact))
                         * np.float32(REL_BUCKETS - max_exact)).astype(np.int32)
    large = np.minimum(large, REL_BUCKETS - 1)
    return np.where(rel < max_exact, rel, large).astype(np.int32)


def _near_bucket_map():
    tq = np.arange(Q_TILE)[:, None]
    tk = np.arange(2 * Q_TILE)[None, :]
    rel = Q_TILE + tq - tk
    return _t5_bucket_np(np.maximum(rel, 0))


def _rms(x, g):
    ms = jnp.mean(x * x, axis=-1, keepdims=True)
    return x * lax.rsqrt(ms + EPS) * g


def _bias_band_kernel(rb_ref, bmap_ref, out_ref):
    h = pl.program_id(0)
    bmap = bmap_ref[...]
    acc = jnp.zeros(bmap.shape, f32)
    for b in range(REL_BUCKETS):
        acc = jnp.where(bmap == b, rb_ref[b, h], acc)
    out_ref[0] = acc - rb_ref[REL_BUCKETS - 1, h]


def _bias_band(rel_bias):
    bmap = jnp.asarray(_near_bucket_map())
    return pl.pallas_call(
        _bias_band_kernel,
        grid=(A_HEADS,),
        in_specs=[pl.BlockSpec(memory_space=pltpu.SMEM),
                  pl.BlockSpec((Q_TILE, 2 * Q_TILE), lambda h: (0, 0))],
        out_specs=pl.BlockSpec((1, Q_TILE, 2 * Q_TILE), lambda h: (h, 0, 0)),
        out_shape=jax.ShapeDtypeStruct((A_HEADS, Q_TILE, 2 * Q_TILE), f32),
        name="bias_band",
    )(rel_bias, bmap)


def _split2(s):
    hi = s.astype(bf16)
    lo = (s - hi.astype(f32)).astype(bf16)
    return hi, lo


def _proj_a_kernel(x_ref, g_ref, w_ref, qg_ref, kg_ref, bd_ref,
                   q_out, qi_out, kkt_out, vext_out, widx_out, gate_out):
    h = _rms(x_ref[0], g_ref[...]).astype(bf16)

    def proj(lo, hi):
        return jnp.dot(h, w_ref[:, lo:hi], preferred_element_type=f32)

    zq = proj(0, A_WIDTH)
    bd = bd_ref[...]
    ssq = sum(jnp.dot(t, bd, preferred_element_type=f32) for t in _split2(zq * zq))
    qn = zq * lax.rsqrt(ssq * (1.0 / A_HEAD_DIM) + EPS) * qg_ref[...]
    qn = (qn * (A_HEAD_DIM ** -0.5)).astype(bf16)
    zqi = proj(A_WIDTH, 2 * A_WIDTH).astype(bf16)
    for hd in range(A_HEADS):
        sl = slice(hd * A_HEAD_DIM, (hd + 1) * A_HEAD_DIM)
        q_out[0, hd] = qn[:, sl]
        qi_out[0, hd] = zqi[:, sl]

    gate_out[0] = proj(2 * A_WIDTH, 3 * A_WIDTH)

    zkk = proj(3 * A_WIDTH, 3 * A_WIDTH + LANES)
    lane = lax.broadcasted_iota(jnp.int32, zkk.shape, 1)
    is_k = lane < A_HEAD_DIM
    ssk = jnp.sum(jnp.where(is_k, zkk * zkk, 0.0), axis=-1, keepdims=True)
    kn = zkk * lax.rsqrt(ssk * (1.0 / A_HEAD_DIM) + EPS) * kg_ref[...]
    kk = jnp.where(is_k, kn, zkk)
    kkt_out[0] = kk.T.astype(bf16)

    zvw = proj(3 * A_WIDTH + LANES, 3 * A_WIDTH + 2 * LANES)
    vext = jnp.where(lane < A_HEAD_DIM, zvw, jnp.where(lane == A_HEAD_DIM, 1.0, 0.0))
    vext_out[0] = vext.astype(bf16)
    widx_out[0] = zvw[:, A_HEAD_DIM:A_HEAD_DIM + IDX_HEADS] * (IDX_HEADS ** -0.5 * IDX_DIM ** -0.5)


def _proj_a(x, norm_g, w_a, qg, kg, bd):
    B, S, _ = x.shape
    T = TOKEN_TILE
    n_a = w_a.shape[1]
    const = lambda b, i: (0, 0)
    return pl.pallas_call(
        _proj_a_kernel,
        grid=(B, S // T),
        in_specs=[pl.BlockSpec((1, T, D_MODEL), lambda b, i: (b, i, 0)),
                  pl.BlockSpec((1, D_MODEL), const),
                  pl.BlockSpec((D_MODEL, n_a), const),
                  pl.BlockSpec((1, A_WIDTH), const),
                  pl.BlockSpec((1, LANES), const),
                  pl.BlockSpec((A_WIDTH, A_WIDTH), const)],
        out_specs=[pl.BlockSpec((1, A_HEADS, T, A_HEAD_DIM), lambda b, i: (b, 0, i, 0)),
                   pl.BlockSpec((1, IDX_HEADS, T, IDX_DIM), lambda b, i: (b, 0, i, 0)),
                   pl.BlockSpec((1, LANES, T), lambda b, i: (b, 0, i)),
                   pl.BlockSpec((1, T, LANES), lambda b, i: (b, i, 0)),
                   pl.BlockSpec((1, T, IDX_HEADS), lambda b, i: (b, i, 0)),
                   pl.BlockSpec((1, T, A_WIDTH), lambda b, i: (b, i, 0))],
        out_shape=[jax.ShapeDtypeStruct((B, A_HEADS, S, A_HEAD_DIM), bf16),
                   jax.ShapeDtypeStruct((B, IDX_HEADS, S, IDX_DIM), bf16),
                   jax.ShapeDtypeStruct((B, LANES, S), bf16),
                   jax.ShapeDtypeStruct((B, S, LANES), bf16),
                   jax.ShapeDtypeStruct((B, S, IDX_HEADS), f32),
                   jax.ShapeDtypeStruct((B, S, A_WIDTH), f32)],
        compiler_params=pltpu.CompilerParams(
            dimension_semantics=("arbitrary", "arbitrary"),
            vmem_limit_bytes=VMEM_LIMIT_BYTES),
        name="proj_a",
    )(x, norm_g, w_a, qg, kg, bd)


def _count(pred):
    return jnp.sum(jnp.where(pred, 1.0, 0.0), axis=1, keepdims=True)


def _attn_block(c, q_ref, qi_ref, kkt_ref, vext_ref, widx_ref, gate_ref, band_ref,
                y_out, sc_ref, cut_ref, yh_ref, lg_ref):
    nk = (c + 1) * Q_TILE
    k_top = float(TOPK_MAX)
    kt = kkt_ref[0, 0:A_HEAD_DIM, 0:nk]
    vext = vext_ref[0, 0:nk, :]
    row = lax.broadcasted_iota(jnp.int32, (Q_TILE, nk), 0) + c * Q_TILE
    col = lax.broadcasted_iota(jnp.int32, (Q_TILE, nk), 1)
    causal = col <= row

    if nk > TOPK_MAX:
        kit = kkt_ref[0, A_HEAD_DIM:2 * A_HEAD_DIM, 0:nk]
        widx = widx_ref[0]

        def idx_term(hd):
            s = jnp.dot(qi_ref[0, hd], kit, preferred_element_type=f32)
            return widx[:, hd:hd + 1] * jnp.maximum(s, 0.0)

        sc = idx_term(0)
        for hd in range(1, IDX_HEADS):
            sc = sc + idx_term(hd)
        sc_ref[:, 0:nk] = jnp.where(causal, sc, -jnp.inf)

        def scores():
            return sc_ref[:, 0:nk]

        sc = scores()
        lo0 = jnp.min(jnp.where(causal, sc, jnp.inf), axis=1, keepdims=True)
        hi0 = jnp.max(sc, axis=1, keepdims=True)
        clo0 = (lax.broadcasted_iota(jnp.int32, (Q_TILE, 1), 0) + (c * Q_TILE + 1)).astype(f32)

        def halve(_, st):
            lo, hi, clo = st
            mid = lo + (hi - lo) * 0.5
            cnt = _count(scores() >= mid)
            ge = cnt >= k_top
            return jnp.where(ge, mid, lo), jnp.where(ge, hi, mid), jnp.where(ge, cnt, clo)

        def candidate(lo):
            s = scores()
            a = jnp.min(jnp.where(s >= lo, s, jnp.inf), axis=1, keepdims=True)
            return a, _count(s > a)

        lo, _, clo = lax.fori_loop(0, BISECT_ITERS, halve, (lo0, hi0, clo0), unroll=BISECT_UNROLL)
        a, cgt = candidate(lo)

        def unfinished(st):
            return jnp.max(st[2]) >= k_top

        def refine(st):
            clo, a, cgt = st
            s = scores()
            nxt = jnp.min(jnp.where(s > a, s, jnp.inf), axis=1, keepdims=True)
            todo = cgt >= k_top
            lo = jnp.where(todo, nxt, a)
            clo = jnp.where(todo, cgt, clo)
            a, cgt = candidate(lo)
            return clo, a, cgt

        cge, thr, cgt = lax.while_loop(unfinished, refine, (clo, a, cgt))

        cut_ref[...] = jnp.full((Q_TILE, 1), nk, jnp.int32)

        @pl.when(jnp.max(cge) > k_top)
        def _():
            eq = scores() == thr
            need = k_top - cgt

            def step(_, lh):
                lo_i, hi_i = lh
                mid = lax.shift_right_arithmetic(lo_i + hi_i, 1)
                ok = _count(eq & (col <= mid)) >= need
                return jnp.where(ok, lo_i, mid), jnp.where(ok, mid, hi_i)

            lo_i = jnp.full((Q_TILE, 1), -1, jnp.int32)
            hi_i = jnp.full((Q_TILE, 1), nk - 1, jnp.int32)
            _, hi_i = lax.fori_loop(0, int(math.ceil(math.log2(nk))), step, (lo_i, hi_i))
            cut_ref[...] = hi_i

        s = scores()
        keep = (s > thr) | ((s == thr) & (col <= cut_ref[...]))
    else:
        keep = causal
    sc_ref[:, 0:nk] = jnp.where(keep, 0.0, -jnp.inf)
    n_kt = nk // K_TILE

    def head_step(hd_values, m_values, hd_logits):
        q_h = None if hd_logits is None else q_ref[0, hd_logits]
        mpart = None
        pv = None
        for j in range(n_kt):
            ks = slice(j * K_TILE, (j + 1) * K_TILE)
            if hd_values is not None:
                pj = jnp.exp(lg_ref[:, ks] - m_values).astype(bf16)
                dj = jnp.dot(pj, vext[ks, :], preferred_element_type=f32)
                pv = dj if pv is None else pv + dj
            if hd_logits is not None:
                add = sc_ref[:, ks]
                band_j = j - (n_kt - 2 * Q_TILE // K_TILE)
                if band_j >= 0:
                    add = add + band_ref[hd_logits][:, band_j * K_TILE:(band_j + 1) * K_TILE]
                lj = jnp.dot(q_h, kt[:, ks], preferred_element_type=f32) + add
                lg_ref[:, ks] = lj
                mj = lj[:, 0:LANES]
                for t in range(1, K_TILE // LANES):
                    mj = jnp.maximum(mj, lj[:, t * LANES:(t + 1) * LANES])
                mpart = mj if mpart is None else jnp.maximum(mpart, mj)
        if hd_values is not None:
            yh_ref[hd_values] = pv
        return None if mpart is None else jnp.max(mpart, axis=1, keepdims=True)

    def one_head(hd, carry):
        head_step(hd, head_step(None, None, hd), None)
        return carry

    lax.fori_loop(0, A_HEADS, one_head, 0)
    ys = []
    for hd in range(A_HEADS):
        pv = yh_ref[hd]
        ys.append(pv[:, 0:A_HEAD_DIM] / pv[:, A_HEAD_DIM:A_HEAD_DIM + 1])
    y = jnp.concatenate(ys, axis=1)
    g = gate_ref[0]
    y_out[0] = (y * (g * jax.nn.sigmoid(g))).astype(bf16)


def _attn_kernel(q_ref, qi_ref, kkt_ref, vext_ref, widx_ref, gate_ref, band_ref,
                 y_out, sc_ref, cut_ref, yh_ref, lg_ref):
    i = pl.program_id(0)
    for c in range(kkt_ref.shape[2] // Q_TILE):
        @pl.when(i == c)
        def _(c=c):
            _attn_block(c, q_ref, qi_ref, kkt_ref, vext_ref, widx_ref, gate_ref, band_ref,
                        y_out, sc_ref, cut_ref, yh_ref, lg_ref)


def _attn(q, qi, kkt, vext, widx, gate, band):
    B, _, S, _ = q.shape
    return pl.pallas_call(
        _attn_kernel,
        grid=(S // Q_TILE, B),
        in_specs=[pl.BlockSpec((1, A_HEADS, Q_TILE, A_HEAD_DIM), lambda i, b: (b, 0, i, 0)),
                  pl.BlockSpec((1, IDX_HEADS, Q_TILE, IDX_DIM), lambda i, b: (b, 0, i, 0)),
                  pl.BlockSpec((1, LANES, S), lambda i, b: (b, 0, 0)),
                  pl.BlockSpec((1, S, LANES), lambda i, b: (b, 0, 0)),
                  pl.BlockSpec((1, Q_TILE, IDX_HEADS), lambda i, b: (b, i, 0)),
                  pl.BlockSpec((1, Q_TILE, A_WIDTH), lambda i, b: (b, i, 0)),
                  pl.BlockSpec((A_HEADS, Q_TILE, 2 * Q_TILE), lambda i, b: (0, 0, 0))],
        out_specs=pl.BlockSpec((1, Q_TILE, A_WIDTH), lambda i, b: (b, i, 0)),
        out_shape=jax.ShapeDtypeStruct((B, S, A_WIDTH), bf16),
        scratch_shapes=[pltpu.VMEM((Q_TILE, S), f32),
                        pltpu.VMEM((Q_TILE, 1), jnp.int32),
                        pltpu.VMEM((A_HEADS, Q_TILE, LANES), f32),
                        pltpu.VMEM((Q_TILE, S), f32)],
        compiler_params=pltpu.CompilerParams(
            dimension_semantics=("arbitrary", "arbitrary"),
            vmem_limit_bytes=VMEM_LIMIT_BYTES),
        name="attn",
    )(q, qi, kkt, vext, widx, gate, band)


def _mix_kernel(x_ref, ya_ref, g_ref, w_ref, lng_ref, lnb_ref, ws_ref, bst_ref, wbr_ref, wo_ref,
                out_ref):
    x = x_ref[...]
    h = _rms(x, g_ref[...]).astype(bf16)
    T = x.shape[0]

    def proj(lo, hi):
        return jnp.dot(h, w_ref[:, lo:hi], preferred_element_type=f32)

    u = jax.nn.gelu(proj(0, B_WIDTH))
    vb = jax.nn.gelu(proj(B_WIDTH, 2 * B_WIDTH))
    mu = jnp.mean(vb, axis=-1, keepdims=True)
    var = jnp.mean(jnp.square(vb - mu), axis=-1, keepdims=True)
    vln = ((vb - mu) * lax.rsqrt(var + EPS) * lng_ref[...] + lnb_ref[...]).astype(bf16)

    n_ch = T // CHUNK
    tril = (lax.broadcasted_iota(jnp.int32, (CHUNK, CHUNK), 1)
            <= lax.broadcasted_iota(jnp.int32, (CHUNK, CHUNK), 0))
    bst = bst_ref[...]
    per_group = []
    for gi in range(B_GROUPS):
        wg = jnp.where(tril, ws_ref[gi], 0.0).astype(bf16)
        cols = slice(gi * B_GROUP_DIM, (gi + 1) * B_GROUP_DIM)
        vg = jnp.concatenate([vln[ci * CHUNK:(ci + 1) * CHUNK, cols] for ci in range(n_ch)], axis=1)
        sg = jnp.dot(wg, vg, preferred_element_type=f32) + bst[:, gi:gi + 1]
        per_group.append(sg)
    s = jnp.concatenate(
        [jnp.concatenate([per_group[gi][:, ci * B_GROUP_DIM:(ci + 1) * B_GROUP_DIM]
                          for gi in range(B_GROUPS)], axis=1)
         for ci in range(n_ch)], axis=0)

    gb = proj(2 * B_WIDTH, 3 * B_WIDTH)
    yb = (u * s * (gb * jax.nn.sigmoid(gb))).astype(bf16)
    yd_a = jnp.dot(ya_ref[...], wbr_ref[0], preferred_element_type=f32)
    yd_b = jnp.dot(yb, wbr_ref[1], preferred_element_type=f32)
    ma = proj(3 * B_WIDTH, 3 * B_WIDTH + D_MODEL)
    mb = proj(3 * B_WIDTH + D_MODEL, 3 * B_WIDTH + 2 * D_MODEL)
    merged = jax.nn.sigmoid(ma) * yd_a + jax.nn.sigmoid(mb) * yd_b
    out_ref[...] = x + jnp.dot(merged.astype(bf16), wo_ref[...], preferred_element_type=f32)


def _mix(x2, ya2, norm_g, w_b, ln_g, ln_b, w_sp, b_sp_t, w_br, w_o):
    N = x2.shape[0]
    T = TOKEN_TILE
    n_b = w_b.shape[1]
    c2 = lambda i: (0, 0)
    c3 = lambda i: (0, 0, 0)
    return pl.pallas_call(
        _mix_kernel,
        grid=(N // T,),
        in_specs=[pl.BlockSpec((T, D_MODEL), lambda i: (i, 0)),
                  pl.BlockSpec((T, A_WIDTH), lambda i: (i, 0)),
                  pl.BlockSpec((1, D_MODEL), c2),
                  pl.BlockSpec((D_MODEL, n_b), c2),
                  pl.BlockSpec((1, B_WIDTH), c2),
                  pl.BlockSpec((1, B_WIDTH), c2),
                  pl.BlockSpec((B_GROUPS, CHUNK, CHUNK), c3),
                  pl.BlockSpec((CHUNK, B_GROUPS), c2),
                  pl.BlockSpec((2, A_WIDTH, D_MODEL), c3),
                  pl.BlockSpec((D_MODEL, D_MODEL), c2)],
        out_specs=pl.BlockSpec((T, D_MODEL), lambda i: (i, 0)),
        out_shape=jax.ShapeDtypeStruct((N, D_MODEL), f32),
        compiler_params=pltpu.CompilerParams(
            dimension_semantics=("arbitrary",),
            vmem_limit_bytes=VMEM_LIMIT_BYTES),
        name="mix",
    )(x2, ya2, norm_g, w_b, ln_g, ln_b, w_sp, b_sp_t, w_br, w_o)


def _attn_side_weights(w):
    pad = jnp.zeros((D_MODEL, LANES - A_HEAD_DIM - IDX_HEADS), w.dtype)
    return jnp.concatenate(
        [w[:, _Q0:_K0], w[:, _QI0:_KI0], w[:, _GA0:_QI0],
         w[:, _K0:_V0], w[:, _KI0:_WI0],
         w[:, _V0:_GA0], w[:, _WI0:_U0], pad], axis=1).astype(bf16)


def kernel(x, norm_g, w_in, q_norm_g, k_norm_g, rel_bias, sgu_ln_g, sgu_ln_b,
           w_spatial, b_spatial, w_branch, w_out):
    B, S, D = x.shape
    depth = w_in.shape[0]
    band = _bias_band(rel_bias)
    head_of = np.arange(A_WIDTH) // A_HEAD_DIM
    bd = jnp.asarray(head_of[:, None] == head_of[None, :], dtype=bf16)
    for l in range(depth):
        w_a = _attn_side_weights(w_in[l])
        w_b = w_in[l][:, _U0:_END].astype(bf16)
        g = norm_g[l][None, :]
        qg = jnp.tile(q_norm_g[l], A_HEADS)[None, :]
        kg = jnp.concatenate([k_norm_g[l], jnp.ones((LANES - A_HEAD_DIM,), f32)])[None, :]
        q, qi, kkt, vext, widx, gate = _proj_a(x, g, w_a, qg, kg, bd)
        ya = _attn(q, qi, kkt, vext, widx, gate, band)
        x2 = _mix(x.reshape(B * S, D), ya.reshape(B * S, A_WIDTH), g, w_b,
                  sgu_ln_g[l][None, :], sgu_ln_b[l][None, :], w_spatial[l],
                  b_spatial[l].T, w_branch[l].astype(bf16), w_out[l].astype(bf16))
        x = x2.reshape(B, S, D)
    return x
```

```python
import functools
import math

import numpy as np
import jax
import jax.numpy as jnp
from jax import lax
from jax.experimental import pallas as pl
from jax.experimental.pallas import tpu as pltpu

D_MODEL = 1024
A_HEADS = 8
A_HEAD_DIM = 64
A_WIDTH = A_HEADS * A_HEAD_DIM
IDX_HEADS = 8
IDX_DIM = 64
TOPK_MAX = 256
B_GROUPS = 4
B_GROUP_DIM = 128
B_WIDTH = B_GROUPS * B_GROUP_DIM
CHUNK = 128
REL_BUCKETS = 32
REL_MAX_DIST = 128
EPS = 1e-6

LANES = 128
VMEM_LIMIT_BYTES = 56 * 1024 * 1024

TOKEN_TILE = 512
Q_TILE = 256
BISECT_ITERS = 24
BISECT_UNROLL = 4

_SPLITS = (A_WIDTH, A_HEAD_DIM, A_HEAD_DIM, A_WIDTH, IDX_HEADS * IDX_DIM, IDX_DIM,
           IDX_HEADS, B_WIDTH, B_WIDTH, B_WIDTH, D_MODEL, D_MODEL)
_OFFS = np.concatenate([[0], np.cumsum(_SPLITS)])
(_Q0, _K0, _V0, _GA0, _QI0, _KI0, _WI0, _U0, _VB0, _GB0, _MA0, _MB0, _END) = [int(o) for o in _OFFS]

f32 = jnp.float32
bf16 = jnp.bfloat16


def _t5_bucket_np(rel):
    max_exact = REL_BUCKETS // 2
    nf = np.maximum(rel, 1).astype(np.float32)
    large = max_exact + (np.log(nf / np.float32(max_exact))
                         / np.float32(math.log(REL_MAX_DIST / max_exact))
                         * np.float32(REL_BUCKETS - max_exact)).astype(np.int32)
    large = np.minimum(large, REL_BUCKETS - 1)
    return np.where(rel < max_exact, rel, large).astype(np.int32)


def _near_bucket_map():
    tq = np.arange(Q_TILE)[:, None]
    tk = np.arange(2 * Q_TILE)[None, :]
    rel = Q_TILE + tq - tk
    return _t5_bucket_np(np.maximum(rel, 0))


def _rms(x, g):
    ms = jnp.mean(x * x, axis=-1, keepdims=True)
    return x * lax.rsqrt(ms + EPS) * g


def _bias_band_kernel(rb_ref, bmap_ref, out_ref):
    h = pl.program_id(0)
    bmap = bmap_ref[...]
    acc = jnp.zeros(bmap.shape, f32)
    for b in range(REL_BUCKETS):
        acc = jnp.where(bmap == b, rb_ref[b, h], acc)
    out_ref[0] = acc - rb_ref[REL_BUCKETS - 1, h]


def _bias_band(rel_bias):
    bmap = jnp.asarray(_near_bucket_map())
    return pl.pallas_call(
        _bias_band_kernel,
        grid=(A_HEADS,),
        in_specs=[pl.BlockSpec(memory_space=pltpu.SMEM),
                  pl.BlockSpec((Q_TILE, 2 * Q_TILE), lambda h: (0, 0))],
        out_specs=pl.BlockSpec((1, Q_TILE, 2 * Q_TILE), lambda h: (h, 0, 0)),
        out_shape=jax.ShapeDtypeStruct((A_HEADS, Q_TILE, 2 * Q_TILE), f32),
        name="bias_band",
    )(rel_bias, bmap)


def _split2(s):
    hi = s.astype(bf16)
    lo = (s - hi.astype(f32)).astype(bf16)
    return hi, lo


def _proj_a_kernel(x_ref, g_ref, w_ref, qg_ref, kg_ref, bd_ref,
                   q_out, qi_out, kkt_out, vext_out, widx_out, gate_out):
    h = _rms(x_ref[0], g_ref[...]).astype(bf16)

    def proj(lo, hi):
        return jnp.dot(h, w_ref[:, lo:hi], preferred_element_type=f32)

    zq = proj(0, A_WIDTH)
    bd = bd_ref[...]
    ssq = sum(jnp.dot(t, bd, preferred_element_type=f32) for t in _split2(zq * zq))
    qn = zq * lax.rsqrt(ssq * (1.0 / A_HEAD_DIM) + EPS) * qg_ref[...]
    qn = (qn * (A_HEAD_DIM ** -0.5)).astype(bf16)
    zqi = proj(A_WIDTH, 2 * A_WIDTH).astype(bf16)
    for hd in range(A_HEADS):
        sl = slice(hd * A_HEAD_DIM, (hd + 1) * A_HEAD_DIM)
        q_out[0, hd] = qn[:, sl]
        qi_out[0, hd] = zqi[:, sl]

    gate_out[0] = proj(2 * A_WIDTH, 3 * A_WIDTH)

    zkk = proj(3 * A_WIDTH, 3 * A_WIDTH + LANES)
    lane = lax.broadcasted_iota(jnp.int32, zkk.shape, 1)
    is_k = lane < A_HEAD_DIM
    ssk = jnp.sum(jnp.where(is_k, zkk * zkk, 0.0), axis=-1, keepdims=True)
    kn = zkk * lax.rsqrt(ssk * (1.0 / A_HEAD_DIM) + EPS) * kg_ref[...]
    kk = jnp.where(is_k, kn, zkk)
    kkt_out[0] = kk.T.astype(bf16)

    zvw = proj(3 * A_WIDTH + LANES, 3 * A_WIDTH + 2 * LANES)
    vext = jnp.where(lane < A_HEAD_DIM, zvw, jnp.where(lane == A_HEAD_DIM, 1.0, 0.0))
    vext_out[0] = vext.astype(bf16)
    widx_out[0] = zvw[:, A_HEAD_DIM:A_HEAD_DIM + IDX_HEADS] * (IDX_HEADS ** -0.5 * IDX_DIM ** -0.5)


def _proj_a(x, norm_g, w_a, qg, kg, bd):
    B, S, _ = x.shape
    T = TOKEN_TILE
    n_a = w_a.shape[1]
    const = lambda b, i: (0, 0)
    return pl.pallas_call(
        _proj_a_kernel,
        grid=(B, S // T),
        in_specs=[pl.BlockSpec((1, T, D_MODEL), lambda b, i: (b, i, 0)),
                  pl.BlockSpec((1, D_MODEL), const),
                  pl.BlockSpec((D_MODEL, n_a), const),
                  pl.BlockSpec((1, A_WIDTH), const),
                  pl.BlockSpec((1, LANES), const),
                  pl.BlockSpec((A_WIDTH, A_WIDTH), const)],
        out_specs=[pl.BlockSpec((1, A_HEADS, T, A_HEAD_DIM), lambda b, i: (b, 0, i, 0)),
                   pl.BlockSpec((1, IDX_HEADS, T, IDX_DIM), lambda b, i: (b, 0, i, 0)),
                   pl.BlockSpec((1, LANES, T), lambda b, i: (b, 0, i)),
                   pl.BlockSpec((1, T, LANES), lambda b, i: (b, i, 0)),
                   pl.BlockSpec((1, T, IDX_HEADS), lambda b, i: (b, i, 0)),
                   pl.BlockSpec((1, T, A_WIDTH), lambda b, i: (b, i, 0))],
        out_shape=[jax.ShapeDtypeStruct((B, A_HEADS, S, A_HEAD_DIM), bf16),
                   jax.ShapeDtypeStruct((B, IDX_HEADS, S, IDX_DIM), bf16),
                   jax.ShapeDtypeStruct((B, LANES, S), bf16),
                   jax.ShapeDtypeStruct((B, S, LANES), bf16),
                   jax.ShapeDtypeStruct((B, S, IDX_HEADS), f32),
                   jax.ShapeDtypeStruct((B, S, A_WIDTH), f32)],
        compiler_params=pltpu.CompilerParams(
            dimension_semantics=("arbitrary", "arbitrary"),
            vmem_limit_bytes=VMEM_LIMIT_BYTES),
        name="proj_a",
    )(x, norm_g, w_a, qg, kg, bd)


def _count(pred):
    return jnp.sum(jnp.where(pred, 1.0, 0.0), axis=1, keepdims=True)


def _attn_block(c, q_ref, qi_ref, kkt_ref, vext_ref, widx_ref, gate_ref, band_ref,
                y_out, sc_ref, cut_ref, yh_ref, wb_ref):
    nk = (c + 1) * Q_TILE
    k_top = float(TOPK_MAX)
    kt = kkt_ref[0, 0:A_HEAD_DIM, 0:nk]
    vext = vext_ref[0, 0:nk, :]
    row = lax.broadcasted_iota(jnp.int32, (Q_TILE, nk), 0) + c * Q_TILE
    col = lax.broadcasted_iota(jnp.int32, (Q_TILE, nk), 1)
    causal = col <= row

    if nk > TOPK_MAX:
        kit = kkt_ref[0, A_HEAD_DIM:2 * A_HEAD_DIM, 0:nk]
        widx = widx_ref[0]

        for hd in range(IDX_HEADS):
            wb_ref[hd] = jnp.broadcast_to(widx[:, hd:hd + 1], (Q_TILE, LANES))
        sc_ref[:, 0:nk] = jnp.zeros((Q_TILE, nk), f32)

        def idx_head(hd, carry):
            s = jnp.dot(qi_ref[0, hd], kit, preferred_element_type=f32)
            w = jnp.concatenate([wb_ref[hd]] * (nk // LANES), axis=1)
            sc_ref[:, 0:nk] = sc_ref[:, 0:nk] + w * jnp.maximum(s, 0.0)
            return carry

        lax.fori_loop(0, IDX_HEADS, idx_head, 0)
        sc_ref[:, 0:nk] = jnp.where(causal, sc_ref[:, 0:nk], -jnp.inf)

        def scores():
            return sc_ref[:, 0:nk]

        sc = scores()
        lo0 = jnp.min(jnp.where(causal, sc, jnp.inf), axis=1, keepdims=True)
        hi0 = jnp.max(sc, axis=1, keepdims=True)
        clo0 = (lax.broadcasted_iota(jnp.int32, (Q_TILE, 1), 0) + (c * Q_TILE + 1)).astype(f32)

        def halve(_, st):
            lo, hi, clo = st
            mid = lo + (hi - lo) * 0.5
            cnt = _count(scores() >= mid)
            ge = cnt >= k_top
            return jnp.where(ge, mid, lo), jnp.where(ge, hi, mid), jnp.where(ge, cnt, clo)

        def candidate(lo):
            s = scores()
            a = jnp.min(jnp.where(s >= lo, s, jnp.inf), axis=1, keepdims=True)
            return a, _count(s > a)

        lo, _, clo = lax.fori_loop(0, BISECT_ITERS, halve, (lo0, hi0, clo0), unroll=BISECT_UNROLL)
        a, cgt = candidate(lo)

        def unfinished(st):
            return jnp.max(st[2]) >= k_top

        def refine(st):
            clo, a, cgt = st
            s = scores()
            nxt = jnp.min(jnp.where(s > a, s, jnp.inf), axis=1, keepdims=True)
            todo = cgt >= k_top
            lo = jnp.where(todo, nxt, a)
            clo = jnp.where(todo, cgt, clo)
            a, cgt = candidate(lo)
            return clo, a, cgt

        cge, thr, cgt = lax.while_loop(unfinished, refine, (clo, a, cgt))

        cut_ref[...] = jnp.full((Q_TILE, 1), nk, jnp.int32)

        @pl.when(jnp.max(cge) > k_top)
        def _():
            eq = scores() == thr
            need = k_top - cgt

            def step(_, lh):
                lo_i, hi_i = lh
                mid = lax.shift_right_arithmetic(lo_i + hi_i, 1)
                ok = _count(eq & (col <= mid)) >= need
                return jnp.where(ok, lo_i, mid), jnp.where(ok, mid, hi_i)

            lo_i = jnp.full((Q_TILE, 1), -1, jnp.int32)
            hi_i = jnp.full((Q_TILE, 1), nk - 1, jnp.int32)
            _, hi_i = lax.fori_loop(0, int(math.ceil(math.log2(nk))), step, (lo_i, hi_i))
            cut_ref[...] = hi_i

        s = scores()
        keep = (s > thr) | ((s == thr) & (col <= cut_ref[...]))
    else:
        keep = causal
    sc_ref[:, 0:nk] = jnp.where(keep, 0.0, -jnp.inf)
    near_w = min(nk, 2 * Q_TILE)

    def head(hd, carry):
        logit = jnp.dot(q_ref[0, hd], kt, preferred_element_type=f32)
        near = logit[:, nk - near_w:] + (band_ref[hd][:, 2 * Q_TILE - near_w:] + sc_ref[:, nk - near_w:nk])
        if nk > near_w:
            far = logit[:, :nk - near_w] + sc_ref[:, 0:nk - near_w]
            logit = jnp.concatenate([far, near], axis=1)
        else:
            logit = near
        m = jnp.max(logit, axis=1, keepdims=True)
        p = jnp.exp(logit - m).astype(bf16)
        yh_ref[hd] = jnp.dot(p, vext, preferred_element_type=f32)
        return carry

    lax.fori_loop(0, A_HEADS, head, 0)
    ys = []
    for hd in range(A_HEADS):
        pv = yh_ref[hd]
        ys.append(pv[:, 0:A_HEAD_DIM] / pv[:, A_HEAD_DIM:A_HEAD_DIM + 1])
    y = jnp.concatenate(ys, axis=1)
    g = gate_ref[0]
    y_out[0] = (y * (g * jax.nn.sigmoid(g))).astype(bf16)


def _attn_kernel(q_ref, qi_ref, kkt_ref, vext_ref, widx_ref, gate_ref, band_ref,
                 y_out, sc_ref, cut_ref, yh_ref, wb_ref):
    i = pl.program_id(0)
    for c in range(kkt_ref.shape[2] // Q_TILE):
        @pl.when(i == c)
        def _(c=c):
            _attn_block(c, q_ref, qi_ref, kkt_ref, vext_ref, widx_ref, gate_ref, band_ref,
                        y_out, sc_ref, cut_ref, yh_ref, wb_ref)


def _attn(q, qi, kkt, vext, widx, gate, band):
    B, _, S, _ = q.shape
    return pl.pallas_call(
        _attn_kernel,
        grid=(S // Q_TILE, B),
        in_specs=[pl.BlockSpec((1, A_HEADS, Q_TILE, A_HEAD_DIM), lambda i, b: (b, 0, i, 0)),
                  pl.BlockSpec((1, IDX_HEADS, Q_TILE, IDX_DIM), lambda i, b: (b, 0, i, 0)),
                  pl.BlockSpec((1, LANES, S), lambda i, b: (b, 0, 0)),
                  pl.BlockSpec((1, S, LANES), lambda i, b: (b, 0, 0)),
                  pl.BlockSpec((1, Q_TILE, IDX_HEADS), lambda i, b: (b, i, 0)),
                  pl.BlockSpec((1, Q_TILE, A_WIDTH), lambda i, b: (b, i, 0)),
                  pl.BlockSpec((A_HEADS, Q_TILE, 2 * Q_TILE), lambda i, b: (0, 0, 0))],
        out_specs=pl.BlockSpec((1, Q_TILE, A_WIDTH), lambda i, b: (b, i, 0)),
        out_shape=jax.ShapeDtypeStruct((B, S, A_WIDTH), bf16),
        scratch_shapes=[pltpu.VMEM((Q_TILE, S), f32),
                        pltpu.VMEM((Q_TILE, 1), jnp.int32),
                        pltpu.VMEM((A_HEADS, Q_TILE, LANES), f32),
                        pltpu.VMEM((IDX_HEADS, Q_TILE, LANES), f32)],
        compiler_params=pltpu.CompilerParams(
            dimension_semantics=("arbitrary", "arbitrary"),
            vmem_limit_bytes=VMEM_LIMIT_BYTES),
        name="attn",
    )(q, qi, kkt, vext, widx, gate, band)


def _mix_kernel(x_ref, ya_ref, g_ref, w_ref, lng_ref, lnb_ref, ws_ref, bst_ref, wbr_ref, wo_ref,
                out_ref):
    x = x_ref[...]
    h = _rms(x, g_ref[...]).astype(bf16)
    T = x.shape[0]

    def proj(lo, hi):
        return jnp.dot(h, w_ref[:, lo:hi], preferred_element_type=f32)

    u = jax.nn.gelu(proj(0, B_WIDTH))
    vb = jax.nn.gelu(proj(B_WIDTH, 2 * B_WIDTH))
    mu = jnp.mean(vb, axis=-1, keepdims=True)
    var = jnp.mean(jnp.square(vb - mu), axis=-1, keepdims=True)
    vln = ((vb - mu) * lax.rsqrt(var + EPS) * lng_ref[...] + lnb_ref[...]).astype(bf16)

    n_ch = T // CHUNK
    tril = (lax.broadcasted_iota(jnp.int32, (CHUNK, CHUNK), 1)
            <= lax.broadcasted_iota(jnp.int32, (CHUNK, CHUNK), 0))
    bst = bst_ref[...]
    per_group = []
    for gi in range(B_GROUPS):
        wg = jnp.where(tril, ws_ref[gi], 0.0).astype(bf16)
        cols = slice(gi * B_GROUP_DIM, (gi + 1) * B_GROUP_DIM)
        vg = jnp.concatenate([vln[ci * CHUNK:(ci + 1) * CHUNK, cols] for ci in range(n_ch)], axis=1)
        sg = jnp.dot(wg, vg, preferred_element_type=f32) + bst[:, gi:gi + 1]
        per_group.append(sg)
    s = jnp.concatenate(
        [jnp.concatenate([per_group[gi][:, ci * B_GROUP_DIM:(ci + 1) * B_GROUP_DIM]
                          for gi in range(B_GROUPS)], axis=1)
         for ci in range(n_ch)], axis=0)

    gb = proj(2 * B_WIDTH, 3 * B_WIDTH)
    yb = (u * s * (gb * jax.nn.sigmoid(gb))).astype(bf16)
    yd_a = jnp.dot(ya_ref[...], wbr_ref[0], preferred_element_type=f32)
    yd_b = jnp.dot(yb, wbr_ref[1], preferred_element_type=f32)
    ma = proj(3 * B_WIDTH, 3 * B_WIDTH + D_MODEL)
    mb = proj(3 * B_WIDTH + D_MODEL, 3 * B_WIDTH + 2 * D_MODEL)
    merged = jax.nn.sigmoid(ma) * yd_a + jax.nn.sigmoid(mb) * yd_b
    out_ref[...] = x + jnp.dot(merged.astype(bf16), wo_ref[...], preferred_element_type=f32)


def _mix(x2, ya2, norm_g, w_b, ln_g, ln_b, w_sp, b_sp_t, w_br, w_o):
    N = x2.shape[0]
    T = TOKEN_TILE
    n_b = w_b.shape[1]
    c2 = lambda i: (0, 0)
    c3 = lambda i: (0, 0, 0)
    return pl.pallas_call(
        _mix_kernel,
        grid=(N // T,),
        in_specs=[pl.BlockSpec((T, D_MODEL), lambda i: (i, 0)),
                  pl.BlockSpec((T, A_WIDTH), lambda i: (i, 0)),
                  pl.BlockSpec((1, D_MODEL), c2),
                  pl.BlockSpec((D_MODEL, n_b), c2),
                  pl.BlockSpec((1, B_WIDTH), c2),
                  pl.BlockSpec((1, B_WIDTH), c2),
                  pl.BlockSpec((B_GROUPS, CHUNK, CHUNK), c3),
                  pl.BlockSpec((CHUNK, B_GROUPS), c2),
                  pl.BlockSpec((2, A_WIDTH, D_MODEL), c3),
                  pl.BlockSpec((D_MODEL, D_MODEL), c2)],
        out_specs=pl.BlockSpec((T, D_MODEL), lambda i: (i, 0)),
        out_shape=jax.ShapeDtypeStruct((N, D_MODEL), f32),
        compiler_params=pltpu.CompilerParams(
            dimension_semantics=("arbitrary",),
            vmem_limit_bytes=VMEM_LIMIT_BYTES),
        name="mix",
    )(x2, ya2, norm_g, w_b, ln_g, ln_b, w_sp, b_sp_t, w_br, w_o)


def _attn_side_weights(w):
    pad = jnp.zeros((D_MODEL, LANES - A_HEAD_DIM - IDX_HEADS), w.dtype)
    return jnp.concatenate(
        [w[:, _Q0:_K0], w[:, _QI0:_KI0], w[:, _GA0:_QI0],
         w[:, _K0:_V0], w[:, _KI0:_WI0],
         w[:, _V0:_GA0], w[:, _WI0:_U0], pad], axis=1).astype(bf16)


def kernel(x, norm_g, w_in, q_norm_g, k_norm_g, rel_bias, sgu_ln_g, sgu_ln_b,
           w_spatial, b_spatial, w_branch, w_out):
    B, S, D = x.shape
    depth = w_in.shape[0]
    band = _bias_band(rel_bias)
    head_of = np.arange(A_WIDTH) // A_HEAD_DIM
    bd = jnp.asarray(head_of[:, None] == head_of[None, :], dtype=bf16)
    for l in range(depth):
        w_a = _attn_side_weights(w_in[l])
        w_b = w_in[l][:, _U0:_END].astype(bf16)
        g = norm_g[l][None, :]
        qg = jnp.tile(q_norm_g[l], A_HEADS)[None, :]
        kg = jnp.concatenate([k_norm_g[l], jnp.ones((LANES - A_HEAD_DIM,), f32)])[None, :]
        q, qi, kkt, vext, widx, gate = _proj_a(x, g, w_a, qg, kg, bd)
        ya = _attn(q, qi, kkt, vext, widx, gate, band)
        x2 = _mix(x.reshape(B * S, D), ya.reshape(B * S, A_WIDTH), g, w_b,
                  sgu_ln_g[l][None, :], sgu_ln_b[l][None, :], w_spatial[l],
                  b_spatial[l].T, w_branch[l].astype(bf16), w_out[l].astype(bf16))
        x = x2.reshape(B, S, D)
    return x
```

```python
import functools
import math

import numpy as np
import jax
import jax.numpy as jnp
from jax import lax
from jax.experimental import pallas as pl
from jax.experimental.pallas import tpu as pltpu

D_MODEL = 1024
A_HEADS = 8
A_HEAD_DIM = 64
A_WIDTH = A_HEADS * A_HEAD_DIM
IDX_HEADS = 8
IDX_DIM = 64
TOPK_MAX = 256
B_GROUPS = 4
B_GROUP_DIM = 128
B_WIDTH = B_GROUPS * B_GROUP_DIM
CHUNK = 128
REL_BUCKETS = 32
REL_MAX_DIST = 128
EPS = 1e-6

LANES = 128
VMEM_LIMIT_BYTES = 56 * 1024 * 1024

TOKEN_TILE = 512
Q_TILE = 256
BISECT_ITERS = 24
BISECT_UNROLL = 4

_SPLITS = (A_WIDTH, A_HEAD_DIM, A_HEAD_DIM, A_WIDTH, IDX_HEADS * IDX_DIM, IDX_DIM,
           IDX_HEADS, B_WIDTH, B_WIDTH, B_WIDTH, D_MODEL, D_MODEL)
_OFFS = np.concatenate([[0], np.cumsum(_SPLITS)])
(_Q0, _K0, _V0, _GA0, _QI0, _KI0, _WI0, _U0, _VB0, _GB0, _MA0, _MB0, _END) = [int(o) for o in _OFFS]

f32 = jnp.float32
bf16 = jnp.bfloat16


def _t5_bucket_np(rel):
    max_exact = REL_BUCKETS // 2
    nf = np.maximum(rel, 1).astype(np.float32)
    large = max_exact + (np.log(nf / np.float32(max_exact))
                         / np.float32(math.log(REL_MAX_DIST / max_exact))
                         * np.float32(REL_BUCKETS - max_exact)).astype(np.int32)
    large = np.minimum(large, REL_BUCKETS - 1)
    return np.where(rel < max_exact, rel, large).astype(np.int32)


def _near_bucket_map():
    tq = np.arange(Q_TILE)[:, None]
    tk = np.arange(2 * Q_TILE)[None, :]
    rel = Q_TILE + tq - tk
    return _t5_bucket_np(np.maximum(rel, 0))


def _rms(x, g):
    ms = jnp.mean(x * x, axis=-1, keepdims=True)
    return x * lax.rsqrt(ms + EPS) * g


def _bias_band_kernel(rb_ref, bmap_ref, out_ref):
    h = pl.program_id(0)
    bmap = bmap_ref[...]
    acc = jnp.zeros(bmap.shape, f32)
    for b in range(REL_BUCKETS):
        acc = jnp.where(bmap == b, rb_ref[b, h], acc)
    out_ref[0] = acc - rb_ref[REL_BUCKETS - 1, h]


def _bias_band(rel_bias):
    bmap = jnp.asarray(_near_bucket_map())
    return pl.pallas_call(
        _bias_band_kernel,
        grid=(A_HEADS,),
        in_specs=[pl.BlockSpec(memory_space=pltpu.SMEM),
                  pl.BlockSpec((Q_TILE, 2 * Q_TILE), lambda h: (0, 0))],
        out_specs=pl.BlockSpec((1, Q_TILE, 2 * Q_TILE), lambda h: (h, 0, 0)),
        out_shape=jax.ShapeDtypeStruct((A_HEADS, Q_TILE, 2 * Q_TILE), f32),
        name="bias_band",
    )(rel_bias, bmap)


def _split2(s):
    hi = s.astype(bf16)
    lo = (s - hi.astype(f32)).astype(bf16)
    return hi, lo


def _proj_a_kernel(x_ref, g_ref, w_ref, qg_ref, kg_ref, bd_ref,
                   q_out, qi_out, kkt_out, vext_out, widx_out, gate_out):
    h = _rms(x_ref[0], g_ref[...]).astype(bf16)

    def proj(lo, hi):
        return jnp.dot(h, w_ref[:, lo:hi], preferred_element_type=f32)

    zq = proj(0, A_WIDTH)
    bd = bd_ref[...]
    ssq = sum(jnp.dot(t, bd, preferred_element_type=f32) for t in _split2(zq * zq))
    qn = zq * lax.rsqrt(ssq * (1.0 / A_HEAD_DIM) + EPS) * qg_ref[...]
    qn = (qn * (A_HEAD_DIM ** -0.5)).astype(bf16)
    zqi = proj(A_WIDTH, 2 * A_WIDTH).astype(bf16)
    for hd in range(A_HEADS):
        sl = slice(hd * A_HEAD_DIM, (hd + 1) * A_HEAD_DIM)
        q_out[0, hd] = qn[:, sl]
        qi_out[0, hd] = zqi[:, sl]

    gate_out[0] = proj(2 * A_WIDTH, 3 * A_WIDTH)

    zkk = proj(3 * A_WIDTH, 3 * A_WIDTH + LANES)
    lane = lax.broadcasted_iota(jnp.int32, zkk.shape, 1)
    is_k = lane < A_HEAD_DIM
    ssk = jnp.sum(jnp.where(is_k, zkk * zkk, 0.0), axis=-1, keepdims=True)
    kn = zkk * lax.rsqrt(ssk * (1.0 / A_HEAD_DIM) + EPS) * kg_ref[...]
    kk = jnp.where(is_k, kn, zkk)
    kkt_out[0] = kk.T.astype(bf16)

    zvw = proj(3 * A_WIDTH + LANES, 3 * A_WIDTH + 2 * LANES)
    vext = jnp.where(lane < A_HEAD_DIM, zvw, jnp.where(lane == A_HEAD_DIM, 1.0, 0.0))
    vext_out[0] = vext.astype(bf16)
    widx_out[0] = zvw[:, A_HEAD_DIM:A_HEAD_DIM + IDX_HEADS] * (IDX_HEADS ** -0.5 * IDX_DIM ** -0.5)


def _proj_a(x, norm_g, w_a, qg, kg, bd):
    B, S, _ = x.shape
    T = TOKEN_TILE
    n_a = w_a.shape[1]
    const = lambda b, i: (0, 0)
    return pl.pallas_call(
        _proj_a_kernel,
        grid=(B, S // T),
        in_specs=[pl.BlockSpec((1, T, D_MODEL), lambda b, i: (b, i, 0)),
                  pl.BlockSpec((1, D_MODEL), const),
                  pl.BlockSpec((D_MODEL, n_a), const),
                  pl.BlockSpec((1, A_WIDTH), const),
                  pl.BlockSpec((1, LANES), const),
                  pl.BlockSpec((A_WIDTH, A_WIDTH), const)],
        out_specs=[pl.BlockSpec((1, A_HEADS, T, A_HEAD_DIM), lambda b, i: (b, 0, i, 0)),
                   pl.BlockSpec((1, IDX_HEADS, T, IDX_DIM), lambda b, i: (b, 0, i, 0)),
                   pl.BlockSpec((1, LANES, T), lambda b, i: (b, 0, i)),
                   pl.BlockSpec((1, T, LANES), lambda b, i: (b, i, 0)),
                   pl.BlockSpec((1, T, IDX_HEADS), lambda b, i: (b, i, 0)),
                   pl.BlockSpec((1, T, A_WIDTH), lambda b, i: (b, i, 0))],
        out_shape=[jax.ShapeDtypeStruct((B, A_HEADS, S, A_HEAD_DIM), bf16),
                   jax.ShapeDtypeStruct((B, IDX_HEADS, S, IDX_DIM), bf16),
                   jax.ShapeDtypeStruct((B, LANES, S), bf16),
                   jax.ShapeDtypeStruct((B, S, LANES), bf16),
                   jax.ShapeDtypeStruct((B, S, IDX_HEADS), f32),
                   jax.ShapeDtypeStruct((B, S, A_WIDTH), f32)],
        compiler_params=pltpu.CompilerParams(
            dimension_semantics=("arbitrary", "arbitrary"),
            vmem_limit_bytes=VMEM_LIMIT_BYTES),
        name="proj_a",
    )(x, norm_g, w_a, qg, kg, bd)


def _count(pred):
    return jnp.sum(jnp.where(pred, 1.0, 0.0), axis=1, keepdims=True)


def _attn_block(c, q_ref, qi_ref, kkt_ref, vext_ref, widx_ref, gate_ref, band_ref,
                y_out, sc_ref, cut_ref, yh_ref, wb_ref):
    nk = (c + 1) * Q_TILE
    k_top = float(TOPK_MAX)
    kt = kkt_ref[0, 0:A_HEAD_DIM, 0:nk]
    vext = vext_ref[0, 0:nk, :]
    row = lax.broadcasted_iota(jnp.int32, (Q_TILE, nk), 0) + c * Q_TILE
    col = lax.broadcasted_iota(jnp.int32, (Q_TILE, nk), 1)
    causal = col <= row

    if nk > TOPK_MAX:
        kit = kkt_ref[0, A_HEAD_DIM:2 * A_HEAD_DIM, 0:nk]
        widx = widx_ref[0]

        def idx_term(hd):
            s = jnp.dot(qi_ref[0, hd], kit, preferred_element_type=f32)
            return widx[:, hd:hd + 1] * jnp.maximum(s, 0.0)

        sc = idx_term(0)
        for hd in range(1, IDX_HEADS):
            sc = sc + idx_term(hd)
        sc_ref[:, 0:nk] = jnp.where(causal, sc, -jnp.inf)

        def scores():
            return sc_ref[:, 0:nk]

        sc = scores()
        lo0 = jnp.min(jnp.where(causal, sc, jnp.inf), axis=1, keepdims=True)
        hi0 = jnp.max(sc, axis=1, keepdims=True)
        clo0 = (lax.broadcasted_iota(jnp.int32, (Q_TILE, 1), 0) + (c * Q_TILE + 1)).astype(f32)

        def halve(_, st):
            lo, hi, clo = st
            mid = lo + (hi - lo) * 0.5
            cnt = _count(scores() >= mid)
            ge = cnt >= k_top
            return jnp.where(ge, mid, lo), jnp.where(ge, hi, mid), jnp.where(ge, cnt, clo)

        def candidate(lo):
            s = scores()
            a = jnp.min(jnp.where(s >= lo, s, jnp.inf), axis=1, keepdims=True)
            return a, _count(s > a)

        lo, _, clo = lax.fori_loop(0, BISECT_ITERS, halve, (lo0, hi0, clo0), unroll=BISECT_UNROLL)
        a, cgt = candidate(lo)

        def unfinished(st):
            return jnp.max(st[2]) >= k_top

        def refine(st):
            clo, a, cgt = st
            s = scores()
            nxt = jnp.min(jnp.where(s > a, s, jnp.inf), axis=1, keepdims=True)
            todo = cgt >= k_top
            lo = jnp.where(todo, nxt, a)
            clo = jnp.where(todo, cgt, clo)
            a, cgt = candidate(lo)
            return clo, a, cgt

        cge, thr, cgt = lax.while_loop(unfinished, refine, (clo, a, cgt))

        cut_ref[...] = jnp.full((Q_TILE, 1), nk, jnp.int32)

        @pl.when(jnp.max(cge) > k_top)
        def _():
            eq = scores() == thr
            need = k_top - cgt

            def step(_, lh):
                lo_i, hi_i = lh
                mid = lax.shift_right_arithmetic(lo_i + hi_i, 1)
                ok = _count(eq & (col <= mid)) >= need
                return jnp.where(ok, lo_i, mid), jnp.where(ok, mid, hi_i)

            lo_i = jnp.full((Q_TILE, 1), -1, jnp.int32)
            hi_i = jnp.full((Q_TILE, 1), nk - 1, jnp.int32)
            _, hi_i = lax.fori_loop(0, int(math.ceil(math.log2(nk))), step, (lo_i, hi_i))
            cut_ref[...] = hi_i

        s = scores()
        keep = (s > thr) | ((s == thr) & (col <= cut_ref[...]))
    else:
        keep = causal
    sc_ref[:, 0:nk] = jnp.where(keep, 0.0, -jnp.inf)
    near_w = min(nk, 2 * Q_TILE)

    ys = []
    for hd in range(A_HEADS):
        logit = jnp.dot(q_ref[0, hd], kt, preferred_element_type=f32)
        near = logit[:, nk - near_w:] + (band_ref[hd][:, 2 * Q_TILE - near_w:] + sc_ref[:, nk - near_w:nk])
        if nk > near_w:
            far = logit[:, :nk - near_w] + sc_ref[:, 0:nk - near_w]
            logit = jnp.concatenate([far, near], axis=1)
        else:
            logit = near
        m = jnp.max(logit, axis=1, keepdims=True)
        p = jnp.exp(logit - m).astype(bf16)
        pv = jnp.dot(p, vext, preferred_element_type=f32)
        ys.append(pv[:, 0:A_HEAD_DIM] / pv[:, A_HEAD_DIM:A_HEAD_DIM + 1])
    y = jnp.concatenate(ys, axis=1)
    g = gate_ref[0]
    y_out[0] = (y * (g * jax.nn.sigmoid(g))).astype(bf16)


def _attn_kernel(q_ref, qi_ref, kkt_ref, vext_ref, widx_ref, gate_ref, band_ref,
                 y_out, sc_ref, cut_ref, yh_ref, wb_ref):
    i = pl.program_id(0)
    for c in range(kkt_ref.shape[2] // Q_TILE):
        @pl.when(i == c)
        def _(c=c):
            _attn_block(c, q_ref, qi_ref, kkt_ref, vext_ref, widx_ref, gate_ref, band_ref,
                        y_out, sc_ref, cut_ref, yh_ref, wb_ref)


def _attn(q, qi, kkt, vext, widx, gate, band):
    B, _, S, _ = q.shape
    return pl.pallas_call(
        _attn_kernel,
        grid=(S // Q_TILE, B),
        in_specs=[pl.BlockSpec((1, A_HEADS, Q_TILE, A_HEAD_DIM), lambda i, b: (b, 0, i, 0)),
                  pl.BlockSpec((1, IDX_HEADS, Q_TILE, IDX_DIM), lambda i, b: (b, 0, i, 0)),
                  pl.BlockSpec((1, LANES, S), lambda i, b: (b, 0, 0)),
                  pl.BlockSpec((1, S, LANES), lambda i, b: (b, 0, 0)),
                  pl.BlockSpec((1, Q_TILE, IDX_HEADS), lambda i, b: (b, i, 0)),
                  pl.BlockSpec((1, Q_TILE, A_WIDTH), lambda i, b: (b, i, 0)),
                  pl.BlockSpec((A_HEADS, Q_TILE, 2 * Q_TILE), lambda i, b: (0, 0, 0))],
        out_specs=pl.BlockSpec((1, Q_TILE, A_WIDTH), lambda i, b: (b, i, 0)),
        out_shape=jax.ShapeDtypeStruct((B, S, A_WIDTH), bf16),
        scratch_shapes=[pltpu.VMEM((Q_TILE, S), f32),
                        pltpu.VMEM((Q_TILE, 1), jnp.int32),
                        pltpu.VMEM((A_HEADS, Q_TILE, LANES), f32),
                        pltpu.VMEM((IDX_HEADS, Q_TILE, LANES), f32)],
        compiler_params=pltpu.CompilerParams(
            dimension_semantics=("arbitrary", "arbitrary"),
            vmem_limit_bytes=VMEM_LIMIT_BYTES),
        name="attn",
    )(q, qi, kkt, vext, widx, gate, band)


def _mix_kernel(x_ref, ya_ref, g_ref, w_ref, lng_ref, lnb_ref, ws_ref, bst_ref, wbr_ref, wo_ref,
                out_ref):
    x = x_ref[...]
    h = _rms(x, g_ref[...]).astype(bf16)
    T = x.shape[0]

    def proj(lo, hi):
        return jnp.dot(h, w_ref[:, lo:hi], preferred_element_type=f32)

    u = jax.nn.gelu(proj(0, B_WIDTH))
    vb = jax.nn.gelu(proj(B_WIDTH, 2 * B_WIDTH))
    mu = jnp.mean(vb, axis=-1, keepdims=True)
    var = jnp.mean(jnp.square(vb - mu), axis=-1, keepdims=True)
    vln = ((vb - mu) * lax.rsqrt(var + EPS) * lng_ref[...] + lnb_ref[...]).astype(bf16)

    n_ch = T // CHUNK
    tril = (lax.broadcasted_iota(jnp.int32, (CHUNK, CHUNK), 1)
            <= lax.broadcasted_iota(jnp.int32, (CHUNK, CHUNK), 0))
    bst = bst_ref[...]
    per_group = []
    for gi in range(B_GROUPS):
        wg = jnp.where(tril, ws_ref[gi], 0.0).astype(bf16)
        cols = slice(gi * B_GROUP_DIM, (gi + 1) * B_GROUP_DIM)
        vg = jnp.concatenate([vln[ci * CHUNK:(ci + 1) * CHUNK, cols] for ci in range(n_ch)], axis=1)
        sg = jnp.dot(wg, vg, preferred_element_type=f32) + bst[:, gi:gi + 1]
        per_group.append(sg)
    s = jnp.concatenate(
        [jnp.concatenate([per_group[gi][:, ci * B_GROUP_DIM:(ci + 1) * B_GROUP_DIM]
                          for gi in range(B_GROUPS)], axis=1)
         for ci in range(n_ch)], axis=0)

    gb = proj(2 * B_WIDTH, 3 * B_WIDTH)
    yb = (u * s * (gb * jax.nn.sigmoid(gb))).astype(bf16)
    yd_a = jnp.dot(ya_ref[...], wbr_ref[0], preferred_element_type=f32)
    yd_b = jnp.dot(yb, wbr_ref[1], preferred_element_type=f32)
    ma = proj(3 * B_WIDTH, 3 * B_WIDTH + D_MODEL)
    mb = proj(3 * B_WIDTH + D_MODEL, 3 * B_WIDTH + 2 * D_MODEL)
    merged = jax.nn.sigmoid(ma) * yd_a + jax.nn.sigmoid(mb) * yd_b
    out_ref[...] = x + jnp.dot(merged.astype(bf16), wo_ref[...], preferred_element_type=f32)


def _mix(x2, ya2, norm_g, w_b, ln_g, ln_b, w_sp, b_sp_t, w_br, w_o):
    N = x2.shape[0]
    T = TOKEN_TILE
    n_b = w_b.shape[1]
    c2 = lambda i: (0, 0)
    c3 = lambda i: (0, 0, 0)
    return pl.pallas_call(
        _mix_kernel,
        grid=(N // T,),
        in_specs=[pl.BlockSpec((T, D_MODEL), lambda i: (i, 0)),
                  pl.BlockSpec((T, A_WIDTH), lambda i: (i, 0)),
                  pl.BlockSpec((1, D_MODEL), c2),
                  pl.BlockSpec((D_MODEL, n_b), c2),
                  pl.BlockSpec((1, B_WIDTH), c2),
                  pl.BlockSpec((1, B_WIDTH), c2),
                  pl.BlockSpec((B_GROUPS, CHUNK, CHUNK), c3),
                  pl.BlockSpec((CHUNK, B_GROUPS), c2),
                  pl.BlockSpec((2, A_WIDTH, D_MODEL), c3),
                  pl.BlockSpec((D_MODEL, D_MODEL), c2)],
        out_specs=pl.BlockSpec((T, D_MODEL), lambda i: (i, 0)),
        out_shape=jax.ShapeDtypeStruct((N, D_MODEL), f32),
        compiler_params=pltpu.CompilerParams(
            dimension_semantics=("arbitrary",),
            vmem_limit_bytes=VMEM_LIMIT_BYTES),
        name="mix",
    )(x2, ya2, norm_g, w_b, ln_g, ln_b, w_sp, b_sp_t, w_br, w_o)


def _attn_side_weights(w):
    pad = jnp.zeros((D_MODEL, LANES - A_HEAD_DIM - IDX_HEADS), w.dtype)
    return jnp.concatenate(
        [w[:, _Q0:_K0], w[:, _QI0:_KI0], w[:, _GA0:_QI0],
         w[:, _K0:_V0], w[:, _KI0:_WI0],
         w[:, _V0:_GA0], w[:, _WI0:_U0], pad], axis=1).astype(bf16)


def kernel(x, norm_g, w_in, q_norm_g, k_norm_g, rel_bias, sgu_ln_g, sgu_ln_b,
           w_spatial, b_spatial, w_branch, w_out):
    B, S, D = x.shape
    depth = w_in.shape[0]
    band = _bias_band(rel_bias)
    head_of = np.arange(A_WIDTH) // A_HEAD_DIM
    bd = jnp.asarray(head_of[:, None] == head_of[None, :], dtype=bf16)
    for l in range(depth):
        w_a = _attn_side_weights(w_in[l])
        w_b = w_in[l][:, _U0:_END].astype(bf16)
        g = norm_g[l][None, :]
        qg = jnp.tile(q_norm_g[l], A_HEADS)[None, :]
        kg = jnp.concatenate([k_norm_g[l], jnp.ones((LANES - A_HEAD_DIM,), f32)])[None, :]
        q, qi, kkt, vext, widx, gate = _proj_a(x, g, w_a, qg, kg, bd)
        ya = _attn(q, qi, kkt, vext, widx, gate, band)
        x2 = _mix(x.reshape(B * S, D), ya.reshape(B * S, A_WIDTH), g, w_b,
                  sgu_ln_g[l][None, :], sgu_ln_b[l][None, :], w_spatial[l],
                  b_spatial[l].T, w_branch[l].astype(bf16), w_out[l].astype(bf16))
        x = x2.reshape(B, S, D)
    return x
```

```python
import functools
import math

import numpy as np
import jax
import jax.numpy as jnp
from jax import lax
from jax.experimental import pallas as pl
from jax.experimental.pallas import tpu as pltpu

D_MODEL = 1024
A_HEADS = 8
A_HEAD_DIM = 64
A_WIDTH = A_HEADS * A_HEAD_DIM
IDX_HEADS = 8
IDX_DIM = 64
TOPK_MAX = 256
B_GROUPS = 4
B_GROUP_DIM = 128
B_WIDTH = B_GROUPS * B_GROUP_DIM
CHUNK = 128
REL_BUCKETS = 32
REL_MAX_DIST = 128
EPS = 1e-6

LANES = 128
VMEM_LIMIT_BYTES = 56 * 1024 * 1024

TOKEN_TILE = 512
Q_TILE = 256
BISECT_ITERS = 20
BISECT_UNROLL = 4

_SPLITS = (A_WIDTH, A_HEAD_DIM, A_HEAD_DIM, A_WIDTH, IDX_HEADS * IDX_DIM, IDX_DIM,
           IDX_HEADS, B_WIDTH, B_WIDTH, B_WIDTH, D_MODEL, D_MODEL)
_OFFS = np.concatenate([[0], np.cumsum(_SPLITS)])
(_Q0, _K0, _V0, _GA0, _QI0, _KI0, _WI0, _U0, _VB0, _GB0, _MA0, _MB0, _END) = [int(o) for o in _OFFS]

f32 = jnp.float32
bf16 = jnp.bfloat16


def _t5_bucket_np(rel):
    max_exact = REL_BUCKETS // 2
    nf = np.maximum(rel, 1).astype(np.float32)
    large = max_exact + (np.log(nf / np.float32(max_exact))
                         / np.float32(math.log(REL_MAX_DIST / max_exact))
                         * np.float32(REL_BUCKETS - max_exact)).astype(np.int32)
    large = np.minimum(large, REL_BUCKETS - 1)
    return np.where(rel < max_exact, rel, large).astype(np.int32)


def _near_bucket_map():
    tq = np.arange(Q_TILE)[:, None]
    tk = np.arange(2 * Q_TILE)[None, :]
    rel = Q_TILE + tq - tk
    return _t5_bucket_np(np.maximum(rel, 0))


def _rms(x, g):
    ms = jnp.mean(x * x, axis=-1, keepdims=True)
    return x * lax.rsqrt(ms + EPS) * g


def _bias_band_kernel(rb_ref, bmap_ref, out_ref):
    h = pl.program_id(0)
    bmap = bmap_ref[...]
    acc = jnp.zeros(bmap.shape, f32)
    for b in range(REL_BUCKETS):
        acc = jnp.where(bmap == b, rb_ref[b, h], acc)
    out_ref[0] = acc - rb_ref[REL_BUCKETS - 1, h]


def _bias_band(rel_bias):
    bmap = jnp.asarray(_near_bucket_map())
    return pl.pallas_call(
        _bias_band_kernel,
        grid=(A_HEADS,),
        in_specs=[pl.BlockSpec(memory_space=pltpu.SMEM),
                  pl.BlockSpec((Q_TILE, 2 * Q_TILE), lambda h: (0, 0))],
        out_specs=pl.BlockSpec((1, Q_TILE, 2 * Q_TILE), lambda h: (h, 0, 0)),
        out_shape=jax.ShapeDtypeStruct((A_HEADS, Q_TILE, 2 * Q_TILE), f32),
        name="bias_band",
    )(rel_bias, bmap)


def _split2(s):
    hi = s.astype(bf16)
    lo = (s - hi.astype(f32)).astype(bf16)
    return hi, lo


def _proj_a_kernel(x_ref, g_ref, w_ref, qg_ref, kg_ref, bd_ref,
                   q_out, qi_out, kkt_out, vext_out, widx_out, gate_out):
    h = _rms(x_ref[0], g_ref[...]).astype(bf16)

    def proj(lo, hi):
        return jnp.dot(h, w_ref[:, lo:hi], preferred_element_type=f32)

    zq = proj(0, A_WIDTH)
    bd = bd_ref[...]
    ssq = sum(jnp.dot(t, bd, preferred_element_type=f32) for t in _split2(zq * zq))
    qn = zq * lax.rsqrt(ssq * (1.0 / A_HEAD_DIM) + EPS) * qg_ref[...]
    qn = (qn * (A_HEAD_DIM ** -0.5)).astype(bf16)
    zqi = proj(A_WIDTH, 2 * A_WIDTH).astype(bf16)
    for hd in range(A_HEADS):
        sl = slice(hd * A_HEAD_DIM, (hd + 1) * A_HEAD_DIM)
        q_out[0, hd] = qn[:, sl]
        qi_out[0, hd] = zqi[:, sl]

    gate_out[0] = proj(2 * A_WIDTH, 3 * A_WIDTH)

    zkk = proj(3 * A_WIDTH, 3 * A_WIDTH + LANES)
    lane = lax.broadcasted_iota(jnp.int32, zkk.shape, 1)
    is_k = lane < A_HEAD_DIM
    ssk = jnp.sum(jnp.where(is_k, zkk * zkk, 0.0), axis=-1, keepdims=True)
    kn = zkk * lax.rsqrt(ssk * (1.0 / A_HEAD_DIM) + EPS) * kg_ref[...]
    kk = jnp.where(is_k, kn, zkk)
    kkt_out[0] = kk.T.astype(bf16)

    zvw = proj(3 * A_WIDTH + LANES, 3 * A_WIDTH + 2 * LANES)
    vext = jnp.where(lane < A_HEAD_DIM, zvw, jnp.where(lane == A_HEAD_DIM, 1.0, 0.0))
    vext_out[0] = vext.astype(bf16)
    widx_out[0] = zvw[:, A_HEAD_DIM:A_HEAD_DIM + IDX_HEADS] * (IDX_HEADS ** -0.5 * IDX_DIM ** -0.5)


def _proj_a(x, norm_g, w_a, qg, kg, bd):
    B, S, _ = x.shape
    T = TOKEN_TILE
    n_a = w_a.shape[1]
    const = lambda b, i: (0, 0)
    return pl.pallas_call(
        _proj_a_kernel,
        grid=(B, S // T),
        in_specs=[pl.BlockSpec((1, T, D_MODEL), lambda b, i: (b, i, 0)),
                  pl.BlockSpec((1, D_MODEL), const),
                  pl.BlockSpec((D_MODEL, n_a), const),
                  pl.BlockSpec((1, A_WIDTH), const),
                  pl.BlockSpec((1, LANES), const),
                  pl.BlockSpec((A_WIDTH, A_WIDTH), const)],
        out_specs=[pl.BlockSpec((1, A_HEADS, T, A_HEAD_DIM), lambda b, i: (b, 0, i, 0)),
                   pl.BlockSpec((1, IDX_HEADS, T, IDX_DIM), lambda b, i: (b, 0, i, 0)),
                   pl.BlockSpec((1, LANES, T), lambda b, i: (b, 0, i)),
                   pl.BlockSpec((1, T, LANES), lambda b, i: (b, i, 0)),
                   pl.BlockSpec((1, T, IDX_HEADS), lambda b, i: (b, i, 0)),
                   pl.BlockSpec((1, T, A_WIDTH), lambda b, i: (b, i, 0))],
        out_shape=[jax.ShapeDtypeStruct((B, A_HEADS, S, A_HEAD_DIM), bf16),
                   jax.ShapeDtypeStruct((B, IDX_HEADS, S, IDX_DIM), bf16),
                   jax.ShapeDtypeStruct((B, LANES, S), bf16),
                   jax.ShapeDtypeStruct((B, S, LANES), bf16),
                   jax.ShapeDtypeStruct((B, S, IDX_HEADS), f32),
                   jax.ShapeDtypeStruct((B, S, A_WIDTH), f32)],
        compiler_params=pltpu.CompilerParams(
            dimension_semantics=("arbitrary", "arbitrary"),
            vmem_limit_bytes=VMEM_LIMIT_BYTES),
        name="proj_a",
    )(x, norm_g, w_a, qg, kg, bd)


def _count(pred):
    return jnp.sum(jnp.where(pred, 1.0, 0.0), axis=1, keepdims=True)


def _attn_block(c, q_ref, qi_ref, kkt_ref, vext_ref, widx_ref, gate_ref, band_ref, upre_ref, ltri_ref,
                y_out, sc_ref, cut_ref, yh_ref):
    nk = (c + 1) * Q_TILE
    k_top = float(TOPK_MAX)
    kt = kkt_ref[0, 0:A_HEAD_DIM, 0:nk]
    vext = vext_ref[0, 0:nk, :]
    row = lax.broadcasted_iota(jnp.int32, (Q_TILE, nk), 0) + c * Q_TILE
    col = lax.broadcasted_iota(jnp.int32, (Q_TILE, nk), 1)
    causal = col <= row

    if nk > TOPK_MAX:
        kit = kkt_ref[0, A_HEAD_DIM:2 * A_HEAD_DIM, 0:nk]
        widx = widx_ref[0]

        def idx_term(hd):
            s = jnp.dot(qi_ref[0, hd], kit, preferred_element_type=f32)
            return widx[:, hd:hd + 1] * jnp.maximum(s, 0.0)

        sc = idx_term(0)
        for hd in range(1, IDX_HEADS):
            sc = sc + idx_term(hd)
        sc_ref[:, 0:nk] = jnp.where(causal, sc, -jnp.inf)

        def scores():
            return sc_ref[:, 0:nk]

        sc = scores()
        lo0 = jnp.min(jnp.where(causal, sc, jnp.inf), axis=1, keepdims=True)
        hi0 = jnp.max(sc, axis=1, keepdims=True)
        clo0 = (lax.broadcasted_iota(jnp.int32, (Q_TILE, 1), 0) + (c * Q_TILE + 1)).astype(f32)

        def halve(_, st):
            lo, hi, clo = st
            mid = lo + (hi - lo) * 0.5
            cnt = _count(scores() >= mid)
            ge = cnt >= k_top
            return jnp.where(ge, mid, lo), jnp.where(ge, hi, mid), jnp.where(ge, cnt, clo)

        def candidate(lo):
            s = scores()
            a = jnp.min(jnp.where(s >= lo, s, jnp.inf), axis=1, keepdims=True)
            return a, _count(s > a)

        lo, _, clo = lax.fori_loop(0, BISECT_ITERS, halve, (lo0, hi0, clo0), unroll=BISECT_UNROLL)
        a, cgt = candidate(lo)

        def unfinished(st):
            return jnp.max(st[2]) >= k_top

        def refine(st):
            clo, a, cgt = st
            s = scores()
            nxt = jnp.min(jnp.where(s > a, s, jnp.inf), axis=1, keepdims=True)
            todo = cgt >= k_top
            lo = jnp.where(todo, nxt, a)
            clo = jnp.where(todo, cgt, clo)
            a, cgt = candidate(lo)
            return clo, a, cgt

        cge, thr, cgt = lax.while_loop(unfinished, refine, (clo, a, cgt))

        cut_ref[...] = jnp.full((Q_TILE, 1), nk, jnp.int32)

        @pl.when(jnp.max(cge) > k_top)
        def _():
            need = k_top - cgt
            eqf = jnp.where(scores() == thr, 1.0, 0.0)
            n_grp = nk // LANES
            before = jnp.dot(eqf.astype(bf16), upre_ref[0:nk, :],
                             preferred_element_type=f32)
            lane = lax.broadcasted_iota(jnp.int32, (Q_TILE, LANES), 1)
            jstar = _count((lane < n_grp) & (before < need)) - 1.0
            base = jnp.sum(jnp.where(lane.astype(f32) == jstar, before, 0.0), axis=1, keepdims=True)
            grp = jnp.zeros((Q_TILE, LANES), f32)
            for j in range(n_grp):
                grp = jnp.where(jstar == float(j), eqf[:, j * LANES:(j + 1) * LANES], grp)
            upto = jnp.dot(grp.astype(bf16), ltri_ref[...],
                           preferred_element_type=f32)
            lstar = _count(upto < need - base)
            cut_ref[...] = (jstar * LANES + lstar).astype(jnp.int32)

        s = scores()
        keep = (s > thr) | ((s == thr) & (col <= cut_ref[...]))
    else:
        keep = causal
    sc_ref[:, 0:nk] = jnp.where(keep, 0.0, -jnp.inf)
    near_w = min(nk, 2 * Q_TILE)

    def head(hd, carry):
        logit = jnp.dot(q_ref[0, hd], kt, preferred_element_type=f32)
        near = logit[:, nk - near_w:] + (band_ref[hd][:, 2 * Q_TILE - near_w:] + sc_ref[:, nk - near_w:nk])
        if nk > near_w:
            far = logit[:, :nk - near_w] + sc_ref[:, 0:nk - near_w]
            logit = jnp.concatenate([far, near], axis=1)
        else:
            logit = near
        m = jnp.max(logit, axis=1, keepdims=True)
        p = jnp.exp(logit - m).astype(bf16)
        yh_ref[hd] = jnp.dot(p, vext, preferred_element_type=f32)
        return carry

    lax.fori_loop(0, A_HEADS, head, 0)
    ys = []
    for hd in range(A_HEADS):
        pv = yh_ref[hd]
        ys.append(pv[:, 0:A_HEAD_DIM] / pv[:, A_HEAD_DIM:A_HEAD_DIM + 1])
    y = jnp.concatenate(ys, axis=1)
    g = gate_ref[0]
    y_out[0] = (y * (g * jax.nn.sigmoid(g))).astype(bf16)


def _attn_kernel(q_ref, qi_ref, kkt_ref, vext_ref, widx_ref, gate_ref, band_ref, upre_ref, ltri_ref,
                 y_out, sc_ref, cut_ref, yh_ref):
    i = pl.program_id(0)
    for c in range(kkt_ref.shape[2] // Q_TILE):
        @pl.when(i == c)
        def _(c=c):
            _attn_block(c, q_ref, qi_ref, kkt_ref, vext_ref, widx_ref, gate_ref, band_ref,
                        upre_ref, ltri_ref, y_out, sc_ref, cut_ref, yh_ref)


def _attn(q, qi, kkt, vext, widx, gate, band):
    B, _, S, _ = q.shape
    key_group = np.arange(S) // LANES
    upre = jnp.asarray(key_group[:, None] < np.arange(LANES)[None, :], dtype=bf16)
    ltri = jnp.asarray(np.arange(LANES)[:, None] <= np.arange(LANES)[None, :], dtype=bf16)
    return pl.pallas_call(
        _attn_kernel,
        grid=(S // Q_TILE, B),
        in_specs=[pl.BlockSpec((1, A_HEADS, Q_TILE, A_HEAD_DIM), lambda i, b: (b, 0, i, 0)),
                  pl.BlockSpec((1, IDX_HEADS, Q_TILE, IDX_DIM), lambda i, b: (b, 0, i, 0)),
                  pl.BlockSpec((1, LANES, S), lambda i, b: (b, 0, 0)),
                  pl.BlockSpec((1, S, LANES), lambda i, b: (b, 0, 0)),
                  pl.BlockSpec((1, Q_TILE, IDX_HEADS), lambda i, b: (b, i, 0)),
                  pl.BlockSpec((1, Q_TILE, A_WIDTH), lambda i, b: (b, i, 0)),
                  pl.BlockSpec((A_HEADS, Q_TILE, 2 * Q_TILE), lambda i, b: (0, 0, 0)),
                  pl.BlockSpec((S, LANES), lambda i, b: (0, 0)),
                  pl.BlockSpec((LANES, LANES), lambda i, b: (0, 0))],
        out_specs=pl.BlockSpec((1, Q_TILE, A_WIDTH), lambda i, b: (b, i, 0)),
        out_shape=jax.ShapeDtypeStruct((B, S, A_WIDTH), bf16),
        scratch_shapes=[pltpu.VMEM((Q_TILE, S), f32),
                        pltpu.VMEM((Q_TILE, 1), jnp.int32),
                        pltpu.VMEM((A_HEADS, Q_TILE, LANES), f32)],
        compiler_params=pltpu.CompilerParams(
            dimension_semantics=("arbitrary", "arbitrary"),
            vmem_limit_bytes=VMEM_LIMIT_BYTES),
        name="attn",
    )(q, qi, kkt, vext, widx, gate, band, upre, ltri)


def _mix_kernel(x_ref, ya_ref, g_ref, w_ref, lng_ref, lnb_ref, ws_ref, bst_ref, wbr_ref, wo_ref,
                out_ref):
    x = x_ref[...]
    h = _rms(x, g_ref[...]).astype(bf16)
    T = x.shape[0]

    def proj(lo, hi):
        return jnp.dot(h, w_ref[:, lo:hi], preferred_element_type=f32)

    u = jax.nn.gelu(proj(0, B_WIDTH))
    vb = jax.nn.gelu(proj(B_WIDTH, 2 * B_WIDTH))
    mu = jnp.mean(vb, axis=-1, keepdims=True)
    var = jnp.mean(jnp.square(vb - mu), axis=-1, keepdims=True)
    vln = ((vb - mu) * lax.rsqrt(var + EPS) * lng_ref[...] + lnb_ref[...]).astype(bf16)

    n_ch = T // CHUNK
    tril = (lax.broadcasted_iota(jnp.int32, (CHUNK, CHUNK), 1)
            <= lax.broadcasted_iota(jnp.int32, (CHUNK, CHUNK), 0))
    bst = bst_ref[...]
    per_group = []
    for gi in range(B_GROUPS):
        wg = jnp.where(tril, ws_ref[gi], 0.0).astype(bf16)
        cols = slice(gi * B_GROUP_DIM, (gi + 1) * B_GROUP_DIM)
        vg = jnp.concatenate([vln[ci * CHUNK:(ci + 1) * CHUNK, cols] for ci in range(n_ch)], axis=1)
        sg = jnp.dot(wg, vg, preferred_element_type=f32) + bst[:, gi:gi + 1]
        per_group.append(sg)
    s = jnp.concatenate(
        [jnp.concatenate([per_group[gi][:, ci * B_GROUP_DIM:(ci + 1) * B_GROUP_DIM]
                          for gi in range(B_GROUPS)], axis=1)
         for ci in range(n_ch)], axis=0)

    gb = proj(2 * B_WIDTH, 3 * B_WIDTH)
    yb = (u * s * (gb * jax.nn.sigmoid(gb))).astype(bf16)
    yd_a = jnp.dot(ya_ref[...], wbr_ref[0], preferred_element_type=f32)
    yd_b = jnp.dot(yb, wbr_ref[1], preferred_element_type=f32)
    ma = proj(3 * B_WIDTH, 3 * B_WIDTH + D_MODEL)
    mb = proj(3 * B_WIDTH + D_MODEL, 3 * B_WIDTH + 2 * D_MODEL)
    merged = jax.nn.sigmoid(ma) * yd_a + jax.nn.sigmoid(mb) * yd_b
    out_ref[...] = x + jnp.dot(merged.astype(bf16), wo_ref[...], preferred_element_type=f32)


def _mix(x2, ya2, norm_g, w_b, ln_g, ln_b, w_sp, b_sp_t, w_br, w_o):
    N = x2.shape[0]
    T = TOKEN_TILE
    n_b = w_b.shape[1]
    c2 = lambda i: (0, 0)
    c3 = lambda i: (0, 0, 0)
    return pl.pallas_call(
        _mix_kernel,
        grid=(N // T,),
        in_specs=[pl.BlockSpec((T, D_MODEL), lambda i: (i, 0)),
                  pl.BlockSpec((T, A_WIDTH), lambda i: (i, 0)),
                  pl.BlockSpec((1, D_MODEL), c2),
                  pl.BlockSpec((D_MODEL, n_b), c2),
                  pl.BlockSpec((1, B_WIDTH), c2),
                  pl.BlockSpec((1, B_WIDTH), c2),
                  pl.BlockSpec((B_GROUPS, CHUNK, CHUNK), c3),
                  pl.BlockSpec((CHUNK, B_GROUPS), c2),
                  pl.BlockSpec((2, A_WIDTH, D_MODEL), c3),
                  pl.BlockSpec((D_MODEL, D_MODEL), c2)],
        out_specs=pl.BlockSpec((T, D_MODEL), lambda i: (i, 0)),
        out_shape=jax.ShapeDtypeStruct((N, D_MODEL), f32),
        compiler_params=pltpu.CompilerParams(
            dimension_semantics=("arbitrary",),
            vmem_limit_bytes=VMEM_LIMIT_BYTES),
        name="mix",
    )(x2, ya2, norm_g, w_b, ln_g, ln_b, w_sp, b_sp_t, w_br, w_o)


def _attn_side_weights(w):
    pad = jnp.zeros((D_MODEL, LANES - A_HEAD_DIM - IDX_HEADS), w.dtype)
    return jnp.concatenate(
        [w[:, _Q0:_K0], w[:, _QI0:_KI0], w[:, _GA0:_QI0],
         w[:, _K0:_V0], w[:, _KI0:_WI0],
         w[:, _V0:_GA0], w[:, _WI0:_U0], pad], axis=1).astype(bf16)


def kernel(x, norm_g, w_in, q_norm_g, k_norm_g, rel_bias, sgu_ln_g, sgu_ln_b,
           w_spatial, b_spatial, w_branch, w_out):
    B, S, D = x.shape
    depth = w_in.shape[0]
    band = _bias_band(rel_bias)
    head_of = np.arange(A_WIDTH) // A_HEAD_DIM
    bd = jnp.asarray(head_of[:, None] == head_of[None, :], dtype=bf16)
    for l in range(depth):
        w_a = _attn_side_weights(w_in[l])
        w_b = w_in[l][:, _U0:_END].astype(bf16)
        g = norm_g[l][None, :]
        qg = jnp.tile(q_norm_g[l], A_HEADS)[None, :]
        kg = jnp.concatenate([k_norm_g[l], jnp.ones((LANES - A_HEAD_DIM,), f32)])[None, :]
        q, qi, kkt, vext, widx, gate = _proj_a(x, g, w_a, qg, kg, bd)
        ya = _attn(q, qi, kkt, vext, widx, gate, band)
        x2 = _mix(x.reshape(B * S, D), ya.reshape(B * S, A_WIDTH), g, w_b,
                  sgu_ln_g[l][None, :], sgu_ln_b[l][None, :], w_spatial[l],
                  b_spatial[l].T, w_branch[l].astype(bf16), w_out[l].astype(bf16))
        x = x2.reshape(B, S, D)
    return x
```

```python
import functools
import math

import numpy as np
import jax
import jax.numpy as jnp
from jax import lax
from jax.experimental import pallas as pl
from jax.experimental.pallas import tpu as pltpu

D_MODEL = 1024
A_HEADS = 8
A_HEAD_DIM = 64
A_WIDTH = A_HEADS * A_HEAD_DIM
IDX_HEADS = 8
IDX_DIM = 64
TOPK_MAX = 256
B_GROUPS = 4
B_GROUP_DIM = 128
B_WIDTH = B_GROUPS * B_GROUP_DIM
CHUNK = 128
REL_BUCKETS = 32
REL_MAX_DIST = 128
EPS = 1e-6

LANES = 128
VMEM_LIMIT_BYTES = 56 * 1024 * 1024

TOKEN_TILE = 512
Q_TILE = 256
BISECT_ITERS = 20
BISECT_UNROLL = 4

_SPLITS = (A_WIDTH, A_HEAD_DIM, A_HEAD_DIM, A_WIDTH, IDX_HEADS * IDX_DIM, IDX_DIM,
           IDX_HEADS, B_WIDTH, B_WIDTH, B_WIDTH, D_MODEL, D_MODEL)
_OFFS = np.concatenate([[0], np.cumsum(_SPLITS)])
(_Q0, _K0, _V0, _GA0, _QI0, _KI0, _WI0, _U0, _VB0, _GB0, _MA0, _MB0, _END) = [int(o) for o in _OFFS]

f32 = jnp.float32
bf16 = jnp.bfloat16


def _t5_bucket_np(rel):
    max_exact = REL_BUCKETS // 2
    nf = np.maximum(rel, 1).astype(np.float32)
    large = max_exact + (np.log(nf / np.float32(max_exact))
                         / np.float32(math.log(REL_MAX_DIST / max_exact))
                         * np.float32(REL_BUCKETS - max_exact)).astype(np.int32)
    large = np.minimum(large, REL_BUCKETS - 1)
    return np.where(rel < max_exact, rel, large).astype(np.int32)


def _near_bucket_map():
    tq = np.arange(Q_TILE)[:, None]
    tk = np.arange(2 * Q_TILE)[None, :]
    rel = Q_TILE + tq - tk
    return _t5_bucket_np(np.maximum(rel, 0))


def _rms(x, g):
    ms = jnp.mean(x * x, axis=-1, keepdims=True)
    return x * lax.rsqrt(ms + EPS) * g


def _bias_band_kernel(rb_ref, bmap_ref, out_ref):
    h = pl.program_id(0)
    bmap = bmap_ref[...]
    acc = jnp.zeros(bmap.shape, f32)
    for b in range(REL_BUCKETS):
        acc = jnp.where(bmap == b, rb_ref[b, h], acc)
    out_ref[0] = acc - rb_ref[REL_BUCKETS - 1, h]


def _bias_band(rel_bias):
    bmap = jnp.asarray(_near_bucket_map())
    return pl.pallas_call(
        _bias_band_kernel,
        grid=(A_HEADS,),
        in_specs=[pl.BlockSpec(memory_space=pltpu.SMEM),
                  pl.BlockSpec((Q_TILE, 2 * Q_TILE), lambda h: (0, 0))],
        out_specs=pl.BlockSpec((1, Q_TILE, 2 * Q_TILE), lambda h: (h, 0, 0)),
        out_shape=jax.ShapeDtypeStruct((A_HEADS, Q_TILE, 2 * Q_TILE), f32),
        name="bias_band",
    )(rel_bias, bmap)


def _split2(s):
    hi = s.astype(bf16)
    lo = (s - hi.astype(f32)).astype(bf16)
    return hi, lo


def _proj_a_kernel(x_ref, g_ref, w_ref, qg_ref, kg_ref, bd_ref,
                   q_out, qi_out, kkt_out, vext_out, widx_out, gate_out):
    h = _rms(x_ref[0], g_ref[...]).astype(bf16)

    def proj(lo, hi):
        return jnp.dot(h, w_ref[:, lo:hi], preferred_element_type=f32)

    zq = proj(0, A_WIDTH)
    bd = bd_ref[...]
    ssq = sum(jnp.dot(t, bd, preferred_element_type=f32) for t in _split2(zq * zq))
    qn = zq * lax.rsqrt(ssq * (1.0 / A_HEAD_DIM) + EPS) * qg_ref[...]
    qn = (qn * (A_HEAD_DIM ** -0.5)).astype(bf16)
    zqi = proj(A_WIDTH, 2 * A_WIDTH).astype(bf16)
    for hd in range(A_HEADS):
        sl = slice(hd * A_HEAD_DIM, (hd + 1) * A_HEAD_DIM)
        q_out[0, hd] = qn[:, sl]
        qi_out[0, hd] = zqi[:, sl]

    gate_out[0] = proj(2 * A_WIDTH, 3 * A_WIDTH)

    zkk = proj(3 * A_WIDTH, 3 * A_WIDTH + LANES)
    lane = lax.broadcasted_iota(jnp.int32, zkk.shape, 1)
    is_k = lane < A_HEAD_DIM
    ssk = jnp.sum(jnp.where(is_k, zkk * zkk, 0.0), axis=-1, keepdims=True)
    kn = zkk * lax.rsqrt(ssk * (1.0 / A_HEAD_DIM) + EPS) * kg_ref[...]
    kk = jnp.where(is_k, kn, zkk)
    kkt_out[0] = kk.T.astype(bf16)

    zvw = proj(3 * A_WIDTH + LANES, 3 * A_WIDTH + 2 * LANES)
    vext = jnp.where(lane < A_HEAD_DIM, zvw, jnp.where(lane == A_HEAD_DIM, 1.0, 0.0))
    vext_out[0] = vext.astype(bf16)
    widx_out[0] = zvw[:, A_HEAD_DIM:A_HEAD_DIM + IDX_HEADS] * (IDX_HEADS ** -0.5 * IDX_DIM ** -0.5)


def _proj_a(x, norm_g, w_a, qg, kg, bd):
    B, S, _ = x.shape
    T = TOKEN_TILE
    n_a = w_a.shape[1]
    const = lambda b, i: (0, 0)
    return pl.pallas_call(
        _proj_a_kernel,
        grid=(B, S // T),
        in_specs=[pl.BlockSpec((1, T, D_MODEL), lambda b, i: (b, i, 0)),
                  pl.BlockSpec((1, D_MODEL), const),
                  pl.BlockSpec((D_MODEL, n_a), const),
                  pl.BlockSpec((1, A_WIDTH), const),
                  pl.BlockSpec((1, LANES), const),
                  pl.BlockSpec((A_WIDTH, A_WIDTH), const)],
        out_specs=[pl.BlockSpec((1, A_HEADS, T, A_HEAD_DIM), lambda b, i: (b, 0, i, 0)),
                   pl.BlockSpec((1, IDX_HEADS, T, IDX_DIM), lambda b, i: (b, 0, i, 0)),
                   pl.BlockSpec((1, LANES, T), lambda b, i: (b, 0, i)),
                   pl.BlockSpec((1, T, LANES), lambda b, i: (b, i, 0)),
                   pl.BlockSpec((1, T, IDX_HEADS), lambda b, i: (b, i, 0)),
                   pl.BlockSpec((1, T, A_WIDTH), lambda b, i: (b, i, 0))],
        out_shape=[jax.ShapeDtypeStruct((B, A_HEADS, S, A_HEAD_DIM), bf16),
                   jax.ShapeDtypeStruct((B, IDX_HEADS, S, IDX_DIM), bf16),
                   jax.ShapeDtypeStruct((B, LANES, S), bf16),
                   jax.ShapeDtypeStruct((B, S, LANES), bf16),
                   jax.ShapeDtypeStruct((B, S, IDX_HEADS), f32),
                   jax.ShapeDtypeStruct((B, S, A_WIDTH), f32)],
        compiler_params=pltpu.CompilerParams(
            dimension_semantics=("arbitrary", "arbitrary"),
            vmem_limit_bytes=VMEM_LIMIT_BYTES),
        name="proj_a",
    )(x, norm_g, w_a, qg, kg, bd)


def _count(pred):
    return jnp.sum(jnp.where(pred, 1.0, 0.0), axis=1, keepdims=True)


def _attn_block(c, q_ref, qi_ref, kkt_ref, vext_ref, widx_ref, gate_ref, band_ref, upre_ref, ltri_ref,
                y_out, sc_ref, cut_ref, yh_ref, wb_ref):
    nk = (c + 1) * Q_TILE
    k_top = float(TOPK_MAX)
    kt = kkt_ref[0, 0:A_HEAD_DIM, 0:nk]
    vext = vext_ref[0, 0:nk, :]
    row = lax.broadcasted_iota(jnp.int32, (Q_TILE, nk), 0) + c * Q_TILE
    col = lax.broadcasted_iota(jnp.int32, (Q_TILE, nk), 1)
    causal = col <= row

    if nk > TOPK_MAX:
        kit = kkt_ref[0, A_HEAD_DIM:2 * A_HEAD_DIM, 0:nk]
        widx = widx_ref[0]

        for hd in range(IDX_HEADS):
            wb_ref[hd] = jnp.broadcast_to(widx[:, hd:hd + 1], (Q_TILE, LANES))
        sc_ref[:, 0:nk] = jnp.zeros((Q_TILE, nk), f32)

        def idx_head(hd, carry):
            s = jnp.dot(qi_ref[0, hd], kit, preferred_element_type=f32)
            w = jnp.concatenate([wb_ref[hd]] * (nk // LANES), axis=1)
            sc_ref[:, 0:nk] = sc_ref[:, 0:nk] + w * jnp.maximum(s, 0.0)
            return carry

        lax.fori_loop(0, IDX_HEADS, idx_head, 0)
        sc_ref[:, 0:nk] = jnp.where(causal, sc_ref[:, 0:nk], -jnp.inf)

        def scores():
            return sc_ref[:, 0:nk]

        sc = scores()
        lo0 = jnp.min(jnp.where(causal, sc, jnp.inf), axis=1, keepdims=True)
        hi0 = jnp.max(sc, axis=1, keepdims=True)
        clo0 = (lax.broadcasted_iota(jnp.int32, (Q_TILE, 1), 0) + (c * Q_TILE + 1)).astype(f32)

        def halve(_, st):
            lo, hi, clo = st
            mid = lo + (hi - lo) * 0.5
            cnt = _count(scores() >= mid)
            ge = cnt >= k_top
            return jnp.where(ge, mid, lo), jnp.where(ge, hi, mid), jnp.where(ge, cnt, clo)

        def candidate(lo):
            s = scores()
            a = jnp.min(jnp.where(s >= lo, s, jnp.inf), axis=1, keepdims=True)
            return a, _count(s > a)

        lo, _, clo = lax.fori_loop(0, BISECT_ITERS, halve, (lo0, hi0, clo0), unroll=BISECT_UNROLL)
        a, cgt = candidate(lo)

        def unfinished(st):
            return jnp.max(st[2]) >= k_top

        def refine(st):
            clo, a, cgt = st
            s = scores()
            nxt = jnp.min(jnp.where(s > a, s, jnp.inf), axis=1, keepdims=True)
            todo = cgt >= k_top
            lo = jnp.where(todo, nxt, a)
            clo = jnp.where(todo, cgt, clo)
            a, cgt = candidate(lo)
            return clo, a, cgt

        cge, thr, cgt = lax.while_loop(unfinished, refine, (clo, a, cgt))

        cut_ref[...] = jnp.full((Q_TILE, 1), nk, jnp.int32)

        @pl.when(jnp.max(cge) > k_top)
        def _():
            need = k_top - cgt
            eqf = jnp.where(scores() == thr, 1.0, 0.0)
            n_grp = nk // LANES
            before = jnp.dot(eqf.astype(bf16), upre_ref[0:nk, :],
                             preferred_element_type=f32)
            lane = lax.broadcasted_iota(jnp.int32, (Q_TILE, LANES), 1)
            jstar = _count((lane < n_grp) & (before < need)) - 1.0
            base = jnp.sum(jnp.where(lane.astype(f32) == jstar, before, 0.0), axis=1, keepdims=True)
            grp = jnp.zeros((Q_TILE, LANES), f32)
            for j in range(n_grp):
                grp = jnp.where(jstar == float(j), eqf[:, j * LANES:(j + 1) * LANES], grp)
            upto = jnp.dot(grp.astype(bf16), ltri_ref[...],
                           preferred_element_type=f32)
            lstar = _count(upto < need - base)
            cut_ref[...] = (jstar * LANES + lstar).astype(jnp.int32)

        s = scores()
        keep = (s > thr) | ((s == thr) & (col <= cut_ref[...]))
    else:
        keep = causal
    sc_ref[:, 0:nk] = jnp.where(keep, 0.0, -jnp.inf)
    near_w = min(nk, 2 * Q_TILE)

    def head(hd, carry):
        logit = jnp.dot(q_ref[0, hd], kt, preferred_element_type=f32)
        near = logit[:, nk - near_w:] + (band_ref[hd][:, 2 * Q_TILE - near_w:] + sc_ref[:, nk - near_w:nk])
        if nk > near_w:
            far = logit[:, :nk - near_w] + sc_ref[:, 0:nk - near_w]
            logit = jnp.concatenate([far, near], axis=1)
        else:
            logit = near
        m = jnp.max(logit, axis=1, keepdims=True)
        p = jnp.exp(logit - m).astype(bf16)
        yh_ref[hd] = jnp.dot(p, vext, preferred_element_type=f32)
        return carry

    lax.fori_loop(0, A_HEADS, head, 0)
    ys = []
    for hd in range(A_HEADS):
        pv = yh_ref[hd]
        ys.append(pv[:, 0:A_HEAD_DIM] / pv[:, A_HEAD_DIM:A_HEAD_DIM + 1])
    y = jnp.concatenate(ys, axis=1)
    g = gate_ref[0]
    y_out[0] = (y * (g * jax.nn.sigmoid(g))).astype(bf16)


def _attn_kernel(q_ref, qi_ref, kkt_ref, vext_ref, widx_ref, gate_ref, band_ref, upre_ref, ltri_ref,
                 y_out, sc_ref, cut_ref, yh_ref, wb_ref):
    i = pl.program_id(0)
    for c in range(kkt_ref.shape[2] // Q_TILE):
        @pl.when(i == c)
        def _(c=c):
            _attn_block(c, q_ref, qi_ref, kkt_ref, vext_ref, widx_ref, gate_ref, band_ref,
                        upre_ref, ltri_ref, y_out, sc_ref, cut_ref, yh_ref, wb_ref)


def _attn(q, qi, kkt, vext, widx, gate, band):
    B, _, S, _ = q.shape
    key_group = np.arange(S) // LANES
    upre = jnp.asarray(key_group[:, None] < np.arange(LANES)[None, :], dtype=bf16)
    ltri = jnp.asarray(np.arange(LANES)[:, None] <= np.arange(LANES)[None, :], dtype=bf16)
    return pl.pallas_call(
        _attn_kernel,
        grid=(S // Q_TILE, B),
        in_specs=[pl.BlockSpec((1, A_HEADS, Q_TILE, A_HEAD_DIM), lambda i, b: (b, 0, i, 0)),
                  pl.BlockSpec((1, IDX_HEADS, Q_TILE, IDX_DIM), lambda i, b: (b, 0, i, 0)),
                  pl.BlockSpec((1, LANES, S), lambda i, b: (b, 0, 0)),
                  pl.BlockSpec((1, S, LANES), lambda i, b: (b, 0, 0)),
                  pl.BlockSpec((1, Q_TILE, IDX_HEADS), lambda i, b: (b, i, 0)),
                  pl.BlockSpec((1, Q_TILE, A_WIDTH), lambda i, b: (b, i, 0)),
                  pl.BlockSpec((A_HEADS, Q_TILE, 2 * Q_TILE), lambda i, b: (0, 0, 0)),
                  pl.BlockSpec((S, LANES), lambda i, b: (0, 0)),
                  pl.BlockSpec((LANES, LANES), lambda i, b: (0, 0))],
        out_specs=pl.BlockSpec((1, Q_TILE, A_WIDTH), lambda i, b: (b, i, 0)),
        out_shape=jax.ShapeDtypeStruct((B, S, A_WIDTH), bf16),
        scratch_shapes=[pltpu.VMEM((Q_TILE, S), f32),
                        pltpu.VMEM((Q_TILE, 1), jnp.int32),
                        pltpu.VMEM((A_HEADS, Q_TILE, LANES), f32),
                        pltpu.VMEM((IDX_HEADS, Q_TILE, LANES), f32)],
        compiler_params=pltpu.CompilerParams(
            dimension_semantics=("arbitrary", "arbitrary"),
            vmem_limit_bytes=VMEM_LIMIT_BYTES),
        name="attn",
    )(q, qi, kkt, vext, widx, gate, band, upre, ltri)


def _mix_kernel(x_ref, ya_ref, g_ref, w_ref, lng_ref, lnb_ref, ws_ref, bst_ref, wbr_ref, wo_ref,
                out_ref):
    x = x_ref[...]
    h = _rms(x, g_ref[...]).astype(bf16)
    T = x.shape[0]

    def proj(lo, hi):
        return jnp.dot(h, w_ref[:, lo:hi], preferred_element_type=f32)

    u = jax.nn.gelu(proj(0, B_WIDTH))
    vb = jax.nn.gelu(proj(B_WIDTH, 2 * B_WIDTH))
    mu = jnp.mean(vb, axis=-1, keepdims=True)
    var = jnp.mean(jnp.square(vb - mu), axis=-1, keepdims=True)
    vln = ((vb - mu) * lax.rsqrt(var + EPS) * lng_ref[...] + lnb_ref[...]).astype(bf16)

    n_ch = T // CHUNK
    tril = (lax.broadcasted_iota(jnp.int32, (CHUNK, CHUNK), 1)
            <= lax.broadcasted_iota(jnp.int32, (CHUNK, CHUNK), 0))
    bst = bst_ref[...]
    per_group = []
    for gi in range(B_GROUPS):
        wg = jnp.where(tril, ws_ref[gi], 0.0).astype(bf16)
        cols = slice(gi * B_GROUP_DIM, (gi + 1) * B_GROUP_DIM)
        vg = jnp.concatenate([vln[ci * CHUNK:(ci + 1) * CHUNK, cols] for ci in range(n_ch)], axis=1)
        sg = jnp.dot(wg, vg, preferred_element_type=f32) + bst[:, gi:gi + 1]
        per_group.append(sg)
    s = jnp.concatenate(
        [jnp.concatenate([per_group[gi][:, ci * B_GROUP_DIM:(ci + 1) * B_GROUP_DIM]
                          for gi in range(B_GROUPS)], axis=1)
         for ci in range(n_ch)], axis=0)

    gb = proj(2 * B_WIDTH, 3 * B_WIDTH)
    yb = (u * s * (gb * jax.nn.sigmoid(gb))).astype(bf16)
    yd_a = jnp.dot(ya_ref[...], wbr_ref[0], preferred_element_type=f32)
    yd_b = jnp.dot(yb, wbr_ref[1], preferred_element_type=f32)
    ma = proj(3 * B_WIDTH, 3 * B_WIDTH + D_MODEL)
    mb = proj(3 * B_WIDTH + D_MODEL, 3 * B_WIDTH + 2 * D_MODEL)
    merged = jax.nn.sigmoid(ma) * yd_a + jax.nn.sigmoid(mb) * yd_b
    out_ref[...] = x + jnp.dot(merged.astype(bf16), wo_ref[...], preferred_element_type=f32)


def _mix(x2, ya2, norm_g, w_b, ln_g, ln_b, w_sp, b_sp_t, w_br, w_o):
    N = x2.shape[0]
    T = TOKEN_TILE
    n_b = w_b.shape[1]
    c2 = lambda i: (0, 0)
    c3 = lambda i: (0, 0, 0)
    return pl.pallas_call(
        _mix_kernel,
        grid=(N // T,),
        in_specs=[pl.BlockSpec((T, D_MODEL), lambda i: (i, 0)),
                  pl.BlockSpec((T, A_WIDTH), lambda i: (i, 0)),
                  pl.BlockSpec((1, D_MODEL), c2),
                  pl.BlockSpec((D_MODEL, n_b), c2),
                  pl.BlockSpec((1, B_WIDTH), c2),
                  pl.BlockSpec((1, B_WIDTH), c2),
                  pl.BlockSpec((B_GROUPS, CHUNK, CHUNK), c3),
                  pl.BlockSpec((CHUNK, B_GROUPS), c2),
                  pl.BlockSpec((2, A_WIDTH, D_MODEL), c3),
                  pl.BlockSpec((D_MODEL, D_MODEL), c2)],
        out_specs=pl.BlockSpec((T, D_MODEL), lambda i: (i, 0)),
        out_shape=jax.ShapeDtypeStruct((N, D_MODEL), f32),
        compiler_params=pltpu.CompilerParams(
            dimension_semantics=("arbitrary",),
            vmem_limit_bytes=VMEM_LIMIT_BYTES),
        name="mix",
    )(x2, ya2, norm_g, w_b, ln_g, ln_b, w_sp, b_sp_t, w_br, w_o)


def _attn_side_weights(w):
    pad = jnp.zeros((D_MODEL, LANES - A_HEAD_DIM - IDX_HEADS), w.dtype)
    return jnp.concatenate(
        [w[:, _Q0:_K0], w[:, _QI0:_KI0], w[:, _GA0:_QI0],
         w[:, _K0:_V0], w[:, _KI0:_WI0],
         w[:, _V0:_GA0], w[:, _WI0:_U0], pad], axis=1).astype(bf16)


def kernel(x, norm_g, w_in, q_norm_g, k_norm_g, rel_bias, sgu_ln_g, sgu_ln_b,
           w_spatial, b_spatial, w_branch, w_out):
    B, S, D = x.shape
    depth = w_in.shape[0]
    band = _bias_band(rel_bias)
    head_of = np.arange(A_WIDTH) // A_HEAD_DIM
    bd = jnp.asarray(head_of[:, None] == head_of[None, :], dtype=bf16)
    for l in range(depth):
        w_a = _attn_side_weights(w_in[l])
        w_b = w_in[l][:, _U0:_END].astype(bf16)
        g = norm_g[l][None, :]
        qg = jnp.tile(q_norm_g[l], A_HEADS)[None, :]
        kg = jnp.concatenate([k_norm_g[l], jnp.ones((LANES - A_HEAD_DIM,), f32)])[None, :]
        q, qi, kkt, vext, widx, gate = _proj_a(x, g, w_a, qg, kg, bd)
        ya = _attn(q, qi, kkt, vext, widx, gate, band)
        x2 = _mix(x.reshape(B * S, D), ya.reshape(B * S, A_WIDTH), g, w_b,
                  sgu_ln_g[l][None, :], sgu_ln_b[l][None, :], w_spatial[l],
                  b_spatial[l].T, w_branch[l].astype(bf16), w_out[l].astype(bf16))
        x = x2.reshape(B, S, D)
    return x
```

```python
import functools
import math

import numpy as np
import jax
import jax.numpy as jnp
from jax import lax
from jax.experimental import pallas as pl
from jax.experimental.pallas import tpu as pltpu

D_MODEL = 1024
A_HEADS = 8
A_HEAD_DIM = 64
A_WIDTH = A_HEADS * A_HEAD_DIM
IDX_HEADS = 8
IDX_DIM = 64
TOPK_MAX = 256
B_GROUPS = 4
B_GROUP_DIM = 128
B_WIDTH = B_GROUPS * B_GROUP_DIM
CHUNK = 128
REL_BUCKETS = 32
REL_MAX_DIST = 128
EPS = 1e-6

LANES = 128
VMEM_LIMIT_BYTES = 56 * 1024 * 1024

TOKEN_TILE = 512
Q_TILE = 256
BISECT_ITERS = 20
BISECT_UNROLL = 4

_SPLITS = (A_WIDTH, A_HEAD_DIM, A_HEAD_DIM, A_WIDTH, IDX_HEADS * IDX_DIM, IDX_DIM,
           IDX_HEADS, B_WIDTH, B_WIDTH, B_WIDTH, D_MODEL, D_MODEL)
_OFFS = np.concatenate([[0], np.cumsum(_SPLITS)])
(_Q0, _K0, _V0, _GA0, _QI0, _KI0, _WI0, _U0, _VB0, _GB0, _MA0, _MB0, _END) = [int(o) for o in _OFFS]

f32 = jnp.float32
bf16 = jnp.bfloat16


def _t5_bucket_np(rel):
    max_exact = REL_BUCKETS // 2
    nf = np.maximum(rel, 1).astype(np.float32)
    large = max_exact + (np.log(nf / np.float32(max_exact))
                         / np.float32(math.log(REL_MAX_DIST / max_exact))
                         * np.float32(REL_BUCKETS - max_exact)).astype(np.int32)
    large = np.minimum(large, REL_BUCKETS - 1)
    return np.where(rel < max_exact, rel, large).astype(np.int32)


def _near_bucket_map():
    tq = np.arange(Q_TILE)[:, None]
    tk = np.arange(2 * Q_TILE)[None, :]
    rel = Q_TILE + tq - tk
    return _t5_bucket_np(np.maximum(rel, 0))


def _rms(x, g):
    ms = jnp.mean(x * x, axis=-1, keepdims=True)
    return x * lax.rsqrt(ms + EPS) * g


def _bias_band_kernel(rb_ref, bmap_ref, out_ref):
    h = pl.program_id(0)
    bmap = bmap_ref[...]
    acc = jnp.zeros(bmap.shape, f32)
    for b in range(REL_BUCKETS):
        acc = jnp.where(bmap == b, rb_ref[b, h], acc)
    out_ref[0] = acc - rb_ref[REL_BUCKETS - 1, h]


def _bias_band(rel_bias):
    bmap = jnp.asarray(_near_bucket_map())
    return pl.pallas_call(
        _bias_band_kernel,
        grid=(A_HEADS,),
        in_specs=[pl.BlockSpec(memory_space=pltpu.SMEM),
                  pl.BlockSpec((Q_TILE, 2 * Q_TILE), lambda h: (0, 0))],
        out_specs=pl.BlockSpec((1, Q_TILE, 2 * Q_TILE), lambda h: (h, 0, 0)),
        out_shape=jax.ShapeDtypeStruct((A_HEADS, Q_TILE, 2 * Q_TILE), f32),
        name="bias_band",
    )(rel_bias, bmap)


def _split2(s):
    hi = s.astype(bf16)
    lo = (s - hi.astype(f32)).astype(bf16)
    return hi, lo


def _proj_a_kernel(x_ref, g_ref, w_ref, qg_ref, kg_ref, bd_ref,
                   q_out, qi_out, kkt_out, vext_out, widx_out, gate_out):
    h = _rms(x_ref[0], g_ref[...]).astype(bf16)

    def proj(lo, hi):
        return jnp.dot(h, w_ref[:, lo:hi], preferred_element_type=f32)

    zq = proj(0, A_WIDTH)
    bd = bd_ref[...]
    ssq = sum(jnp.dot(t, bd, preferred_element_type=f32) for t in _split2(zq * zq))
    qn = zq * lax.rsqrt(ssq * (1.0 / A_HEAD_DIM) + EPS) * qg_ref[...]
    qn = (qn * (A_HEAD_DIM ** -0.5)).astype(bf16)
    zqi = proj(A_WIDTH, 2 * A_WIDTH).astype(bf16)
    for hd in range(A_HEADS):
        sl = slice(hd * A_HEAD_DIM, (hd + 1) * A_HEAD_DIM)
        q_out[0, hd] = qn[:, sl]
        qi_out[0, hd] = zqi[:, sl]

    gate_out[0] = proj(2 * A_WIDTH, 3 * A_WIDTH)

    zkk = proj(3 * A_WIDTH, 3 * A_WIDTH + LANES)
    lane = lax.broadcasted_iota(jnp.int32, zkk.shape, 1)
    is_k = lane < A_HEAD_DIM
    ssk = jnp.sum(jnp.where(is_k, zkk * zkk, 0.0), axis=-1, keepdims=True)
    kn = zkk * lax.rsqrt(ssk * (1.0 / A_HEAD_DIM) + EPS) * kg_ref[...]
    kk = jnp.where(is_k, kn, zkk)
    kkt_out[0] = kk.T.astype(bf16)

    zvw = proj(3 * A_WIDTH + LANES, 3 * A_WIDTH + 2 * LANES)
    vext = jnp.where(lane < A_HEAD_DIM, zvw, jnp.where(lane == A_HEAD_DIM, 1.0, 0.0))
    vext_out[0] = vext.astype(bf16)
    widx_out[0] = zvw[:, A_HEAD_DIM:A_HEAD_DIM + IDX_HEADS] * (IDX_HEADS ** -0.5 * IDX_DIM ** -0.5)


def _proj_a(x, norm_g, w_a, qg, kg, bd):
    B, S, _ = x.shape
    T = TOKEN_TILE
    n_a = w_a.shape[1]
    const = lambda b, i: (0, 0)
    return pl.pallas_call(
        _proj_a_kernel,
        grid=(B, S // T),
        in_specs=[pl.BlockSpec((1, T, D_MODEL), lambda b, i: (b, i, 0)),
                  pl.BlockSpec((1, D_MODEL), const),
                  pl.BlockSpec((D_MODEL, n_a), const),
                  pl.BlockSpec((1, A_WIDTH), const),
                  pl.BlockSpec((1, LANES), const),
                  pl.BlockSpec((A_WIDTH, A_WIDTH), const)],
        out_specs=[pl.BlockSpec((1, A_HEADS, T, A_HEAD_DIM), lambda b, i: (b, 0, i, 0)),
                   pl.BlockSpec((1, IDX_HEADS, T, IDX_DIM), lambda b, i: (b, 0, i, 0)),
                   pl.BlockSpec((1, LANES, T), lambda b, i: (b, 0, i)),
                   pl.BlockSpec((1, T, LANES), lambda b, i: (b, i, 0)),
                   pl.BlockSpec((1, T, IDX_HEADS), lambda b, i: (b, i, 0)),
                   pl.BlockSpec((1, T, A_WIDTH), lambda b, i: (b, i, 0))],
        out_shape=[jax.ShapeDtypeStruct((B, A_HEADS, S, A_HEAD_DIM), bf16),
                   jax.ShapeDtypeStruct((B, IDX_HEADS, S, IDX_DIM), bf16),
                   jax.ShapeDtypeStruct((B, LANES, S), bf16),
                   jax.ShapeDtypeStruct((B, S, LANES), bf16),
                   jax.ShapeDtypeStruct((B, S, IDX_HEADS), f32),
                   jax.ShapeDtypeStruct((B, S, A_WIDTH), f32)],
        compiler_params=pltpu.CompilerParams(
            dimension_semantics=("arbitrary", "arbitrary"),
            vmem_limit_bytes=VMEM_LIMIT_BYTES),
        name="proj_a",
    )(x, norm_g, w_a, qg, kg, bd)


def _count(pred):
    return jnp.sum(jnp.where(pred, 1.0, 0.0), axis=1, keepdims=True)


def _attn_block(c, q_ref, qi_ref, kkt_ref, vext_ref, widx_ref, gate_ref, band_ref, upre_ref, ltri_ref,
                y_out, sc_ref, cut_ref, yh_ref, wb_ref, lg_ref):
    nk = (c + 1) * Q_TILE
    k_top = float(TOPK_MAX)
    kt = kkt_ref[0, 0:A_HEAD_DIM, 0:nk]
    vext = vext_ref[0, 0:nk, :]
    row = lax.broadcasted_iota(jnp.int32, (Q_TILE, nk), 0) + c * Q_TILE
    col = lax.broadcasted_iota(jnp.int32, (Q_TILE, nk), 1)
    causal = col <= row

    if nk > TOPK_MAX:
        kit = kkt_ref[0, A_HEAD_DIM:2 * A_HEAD_DIM, 0:nk]
        widx = widx_ref[0]

        for hd in range(IDX_HEADS):
            wb_ref[hd] = jnp.broadcast_to(widx[:, hd:hd + 1], (Q_TILE, LANES))
        sc_ref[:, 0:nk] = jnp.zeros((Q_TILE, nk), f32)

        def idx_head(hd, carry):
            s = jnp.dot(qi_ref[0, hd], kit, preferred_element_type=f32)
            w = jnp.concatenate([wb_ref[hd]] * (nk // LANES), axis=1)
            sc_ref[:, 0:nk] = sc_ref[:, 0:nk] + w * jnp.maximum(s, 0.0)
            return carry

        lax.fori_loop(0, IDX_HEADS, idx_head, 0)
        sc_ref[:, 0:nk] = jnp.where(causal, sc_ref[:, 0:nk], -jnp.inf)

        def scores():
            return sc_ref[:, 0:nk]

        sc = scores()
        lo0 = jnp.min(jnp.where(causal, sc, jnp.inf), axis=1, keepdims=True)
        hi0 = jnp.max(sc, axis=1, keepdims=True)
        clo0 = (lax.broadcasted_iota(jnp.int32, (Q_TILE, 1), 0) + (c * Q_TILE + 1)).astype(f32)

        def halve(_, st):
            lo, hi, clo = st
            mid = lo + (hi - lo) * 0.5
            cnt = _count(scores() >= mid)
            ge = cnt >= k_top
            return jnp.where(ge, mid, lo), jnp.where(ge, hi, mid), jnp.where(ge, cnt, clo)

        def candidate(lo):
            s = scores()
            a = jnp.min(jnp.where(s >= lo, s, jnp.inf), axis=1, keepdims=True)
            return a, _count(s > a)

        lo, _, clo = lax.fori_loop(0, BISECT_ITERS, halve, (lo0, hi0, clo0), unroll=BISECT_UNROLL)
        a, cgt = candidate(lo)

        def unfinished(st):
            return jnp.max(st[2]) >= k_top

        def refine(st):
            clo, a, cgt = st
            s = scores()
            nxt = jnp.min(jnp.where(s > a, s, jnp.inf), axis=1, keepdims=True)
            todo = cgt >= k_top
            lo = jnp.where(todo, nxt, a)
            clo = jnp.where(todo, cgt, clo)
            a, cgt = candidate(lo)
            return clo, a, cgt

        cge, thr, cgt = lax.while_loop(unfinished, refine, (clo, a, cgt))

        cut_ref[...] = jnp.full((Q_TILE, 1), nk, jnp.int32)

        @pl.when(jnp.max(cge) > k_top)
        def _():
            need = k_top - cgt
            eqf = jnp.where(scores() == thr, 1.0, 0.0)
            n_grp = nk // LANES
            before = jnp.dot(eqf.astype(bf16), upre_ref[0:nk, :],
                             preferred_element_type=f32)
            lane = lax.broadcasted_iota(jnp.int32, (Q_TILE, LANES), 1)
            jstar = _count((lane < n_grp) & (before < need)) - 1.0
            base = jnp.sum(jnp.where(lane.astype(f32) == jstar, before, 0.0), axis=1, keepdims=True)
            grp = jnp.zeros((Q_TILE, LANES), f32)
            for j in range(n_grp):
                grp = jnp.where(jstar == float(j), eqf[:, j * LANES:(j + 1) * LANES], grp)
            upto = jnp.dot(grp.astype(bf16), ltri_ref[...],
                           preferred_element_type=f32)
            lstar = _count(upto < need - base)
            cut_ref[...] = (jstar * LANES + lstar).astype(jnp.int32)

        s = scores()
        keep = (s > thr) | ((s == thr) & (col <= cut_ref[...]))
    else:
        keep = causal
    sc_ref[:, 0:nk] = jnp.where(keep, 0.0, -jnp.inf)
    near_w = min(nk, 2 * Q_TILE)

    def masked_logits(hd):
        logit = jnp.dot(q_ref[0, hd], kt, preferred_element_type=f32)
        near = logit[:, nk - near_w:] + (band_ref[hd][:, 2 * Q_TILE - near_w:] + sc_ref[:, nk - near_w:nk])
        if nk > near_w:
            far = logit[:, :nk - near_w] + sc_ref[:, 0:nk - near_w]
            logit = jnp.concatenate([far, near], axis=1)
        else:
            logit = near
        return logit, jnp.max(logit, axis=1, keepdims=True)

    def weighted_values(hd, m):
        p = jnp.exp(lg_ref[:, 0:nk] - m).astype(bf16)
        yh_ref[hd] = jnp.dot(p, vext, preferred_element_type=f32)

    def head(hd, m_prev):
        p = jnp.exp(lg_ref[:, 0:nk] - m_prev).astype(bf16)
        logit, m = masked_logits(hd)
        lg_ref[:, 0:nk] = logit
        yh_ref[hd - 1] = jnp.dot(p, vext, preferred_element_type=f32)
        return m

    logit0, m0 = masked_logits(0)
    lg_ref[:, 0:nk] = logit0
    m_last = lax.fori_loop(1, A_HEADS, head, m0)
    weighted_values(A_HEADS - 1, m_last)
    ys = []
    for hd in range(A_HEADS):
        pv = yh_ref[hd]
        ys.append(pv[:, 0:A_HEAD_DIM] / pv[:, A_HEAD_DIM:A_HEAD_DIM + 1])
    y = jnp.concatenate(ys, axis=1)
    g = gate_ref[0]
    y_out[0] = (y * (g * jax.nn.sigmoid(g))).astype(bf16)


def _attn_kernel(q_ref, qi_ref, kkt_ref, vext_ref, widx_ref, gate_ref, band_ref, upre_ref, ltri_ref,
                 y_out, sc_ref, cut_ref, yh_ref, wb_ref, lg_ref):
    i = pl.program_id(0)
    for c in range(kkt_ref.shape[2] // Q_TILE):
        @pl.when(i == c)
        def _(c=c):
            _attn_block(c, q_ref, qi_ref, kkt_ref, vext_ref, widx_ref, gate_ref, band_ref,
                        upre_ref, ltri_ref, y_out, sc_ref, cut_ref, yh_ref, wb_ref, lg_ref)


def _attn(q, qi, kkt, vext, widx, gate, band):
    B, _, S, _ = q.shape
    key_group = np.arange(S) // LANES
    upre = jnp.asarray(key_group[:, None] < np.arange(LANES)[None, :], dtype=bf16)
    ltri = jnp.asarray(np.arange(LANES)[:, None] <= np.arange(LANES)[None, :], dtype=bf16)
    return pl.pallas_call(
        _attn_kernel,
        grid=(S // Q_TILE, B),
        in_specs=[pl.BlockSpec((1, A_HEADS, Q_TILE, A_HEAD_DIM), lambda i, b: (b, 0, i, 0)),
                  pl.BlockSpec((1, IDX_HEADS, Q_TILE, IDX_DIM), lambda i, b: (b, 0, i, 0)),
                  pl.BlockSpec((1, LANES, S), lambda i, b: (b, 0, 0)),
                  pl.BlockSpec((1, S, LANES), lambda i, b: (b, 0, 0)),
                  pl.BlockSpec((1, Q_TILE, IDX_HEADS), lambda i, b: (b, i, 0)),
                  pl.BlockSpec((1, Q_TILE, A_WIDTH), lambda i, b: (b, i, 0)),
                  pl.BlockSpec((A_HEADS, Q_TILE, 2 * Q_TILE), lambda i, b: (0, 0, 0)),
                  pl.BlockSpec((S, LANES), lambda i, b: (0, 0)),
                  pl.BlockSpec((LANES, LANES), lambda i, b: (0, 0))],
        out_specs=pl.BlockSpec((1, Q_TILE, A_WIDTH), lambda i, b: (b, i, 0)),
        out_shape=jax.ShapeDtypeStruct((B, S, A_WIDTH), bf16),
        scratch_shapes=[pltpu.VMEM((Q_TILE, S), f32),
                        pltpu.VMEM((Q_TILE, 1), jnp.int32),
                        pltpu.VMEM((A_HEADS, Q_TILE, LANES), f32),
                        pltpu.VMEM((IDX_HEADS, Q_TILE, LANES), f32),
                        pltpu.VMEM((Q_TILE, S), f32)],
        compiler_params=pltpu.CompilerParams(
            dimension_semantics=("arbitrary", "arbitrary"),
            vmem_limit_bytes=VMEM_LIMIT_BYTES),
        name="attn",
    )(q, qi, kkt, vext, widx, gate, band, upre, ltri)


def _mix_kernel(x_ref, ya_ref, g_ref, w_ref, lng_ref, lnb_ref, ws_ref, bst_ref, wbr_ref, wo_ref,
                out_ref):
    x = x_ref[...]
    h = _rms(x, g_ref[...]).astype(bf16)
    T = x.shape[0]

    def proj(lo, hi):
        return jnp.dot(h, w_ref[:, lo:hi], preferred_element_type=f32)

    u = jax.nn.gelu(proj(0, B_WIDTH))
    vb = jax.nn.gelu(proj(B_WIDTH, 2 * B_WIDTH))
    mu = jnp.mean(vb, axis=-1, keepdims=True)
    var = jnp.mean(jnp.square(vb - mu), axis=-1, keepdims=True)
    vln = ((vb - mu) * lax.rsqrt(var + EPS) * lng_ref[...] + lnb_ref[...]).astype(bf16)

    n_ch = T // CHUNK
    tril = (lax.broadcasted_iota(jnp.int32, (CHUNK, CHUNK), 1)
            <= lax.broadcasted_iota(jnp.int32, (CHUNK, CHUNK), 0))
    bst = bst_ref[...]
    per_group = []
    for gi in range(B_GROUPS):
        wg = jnp.where(tril, ws_ref[gi], 0.0).astype(bf16)
        cols = slice(gi * B_GROUP_DIM, (gi + 1) * B_GROUP_DIM)
        vg = jnp.concatenate([vln[ci * CHUNK:(ci + 1) * CHUNK, cols] for ci in range(n_ch)], axis=1)
        sg = jnp.dot(wg, vg, preferred_element_type=f32) + bst[:, gi:gi + 1]
        per_group.append(sg)
    s = jnp.concatenate(
        [jnp.concatenate([per_group[gi][:, ci * B_GROUP_DIM:(ci + 1) * B_GROUP_DIM]
                          for gi in range(B_GROUPS)], axis=1)
         for ci in range(n_ch)], axis=0)

    gb = proj(2 * B_WIDTH, 3 * B_WIDTH)
    yb = (u * s * (gb * jax.nn.sigmoid(gb))).astype(bf16)
    yd_a = jnp.dot(ya_ref[...], wbr_ref[0], preferred_element_type=f32)
    yd_b = jnp.dot(yb, wbr_ref[1], preferred_element_type=f32)
    ma = proj(3 * B_WIDTH, 3 * B_WIDTH + D_MODEL)
    mb = proj(3 * B_WIDTH + D_MODEL, 3 * B_WIDTH + 2 * D_MODEL)
    merged = jax.nn.sigmoid(ma) * yd_a + jax.nn.sigmoid(mb) * yd_b
    out_ref[...] = x + jnp.dot(merged.astype(bf16), wo_ref[...], preferred_element_type=f32)


def _mix(x2, ya2, norm_g, w_b, ln_g, ln_b, w_sp, b_sp_t, w_br, w_o):
    N = x2.shape[0]
    T = TOKEN_TILE
    n_b = w_b.shape[1]
    c2 = lambda i: (0, 0)
    c3 = lambda i: (0, 0, 0)
    return pl.pallas_call(
        _mix_kernel,
        grid=(N // T,),
        in_specs=[pl.BlockSpec((T, D_MODEL), lambda i: (i, 0)),
                  pl.BlockSpec((T, A_WIDTH), lambda i: (i, 0)),
                  pl.BlockSpec((1, D_MODEL), c2),
                  pl.BlockSpec((D_MODEL, n_b), c2),
                  pl.BlockSpec((1, B_WIDTH), c2),
                  pl.BlockSpec((1, B_WIDTH), c2),
                  pl.BlockSpec((B_GROUPS, CHUNK, CHUNK), c3),
                  pl.BlockSpec((CHUNK, B_GROUPS), c2),
                  pl.BlockSpec((2, A_WIDTH, D_MODEL), c3),
                  pl.BlockSpec((D_MODEL, D_MODEL), c2)],
        out_specs=pl.BlockSpec((T, D_MODEL), lambda i: (i, 0)),
        out_shape=jax.ShapeDtypeStruct((N, D_MODEL), f32),
        compiler_params=pltpu.CompilerParams(
            dimension_semantics=("arbitrary",),
            vmem_limit_bytes=VMEM_LIMIT_BYTES),
        name="mix",
    )(x2, ya2, norm_g, w_b, ln_g, ln_b, w_sp, b_sp_t, w_br, w_o)


def _attn_side_weights(w):
    pad = jnp.zeros((D_MODEL, LANES - A_HEAD_DIM - IDX_HEADS), w.dtype)
    return jnp.concatenate(
        [w[:, _Q0:_K0], w[:, _QI0:_KI0], w[:, _GA0:_QI0],
         w[:, _K0:_V0], w[:, _KI0:_WI0],
         w[:, _V0:_GA0], w[:, _WI0:_U0], pad], axis=1).astype(bf16)


def kernel(x, norm_g, w_in, q_norm_g, k_norm_g, rel_bias, sgu_ln_g, sgu_ln_b,
           w_spatial, b_spatial, w_branch, w_out):
    B, S, D = x.shape
    depth = w_in.shape[0]
    band = _bias_band(rel_bias)
    head_of = np.arange(A_WIDTH) // A_HEAD_DIM
    bd = jnp.asarray(head_of[:, None] == head_of[None, :], dtype=bf16)
    for l in range(depth):
        w_a = _attn_side_weights(w_in[l])
        w_b = w_in[l][:, _U0:_END].astype(bf16)
        g = norm_g[l][None, :]
        qg = jnp.tile(q_norm_g[l], A_HEADS)[None, :]
        kg = jnp.concatenate([k_norm_g[l], jnp.ones((LANES - A_HEAD_DIM,), f32)])[None, :]
        q, qi, kkt, vext, widx, gate = _proj_a(x, g, w_a, qg, kg, bd)
        ya = _attn(q, qi, kkt, vext, widx, gate, band)
        x2 = _mix(x.reshape(B * S, D), ya.reshape(B * S, A_WIDTH), g, w_b,
                  sgu_ln_g[l][None, :], sgu_ln_b[l][None, :], w_spatial[l],
                  b_spatial[l].T, w_branch[l].astype(bf16), w_out[l].astype(bf16))
        x = x2.reshape(B, S, D)
    return x
```

```python
import functools
import math

import numpy as np
import jax
import jax.numpy as jnp
from jax import lax
from jax.experimental import pallas as pl
from jax.experimental.pallas import tpu as pltpu

D_MODEL = 1024
A_HEADS = 8
A_HEAD_DIM = 64
A_WIDTH = A_HEADS * A_HEAD_DIM
IDX_HEADS = 8
IDX_DIM = 64
TOPK_MAX = 256
B_GROUPS = 4
B_GROUP_DIM = 128
B_WIDTH = B_GROUPS * B_GROUP_DIM
CHUNK = 128
REL_BUCKETS = 32
REL_MAX_DIST = 128
EPS = 1e-6

LANES = 128
VMEM_LIMIT_BYTES = 56 * 1024 * 1024

TOKEN_TILE = 512
Q_TILE = 256
BISECT_ITERS = 20
BISECT_UNROLL = 4

_SPLITS = (A_WIDTH, A_HEAD_DIM, A_HEAD_DIM, A_WIDTH, IDX_HEADS * IDX_DIM, IDX_DIM,
           IDX_HEADS, B_WIDTH, B_WIDTH, B_WIDTH, D_MODEL, D_MODEL)
_OFFS = np.concatenate([[0], np.cumsum(_SPLITS)])
(_Q0, _K0, _V0, _GA0, _QI0, _KI0, _WI0, _U0, _VB0, _GB0, _MA0, _MB0, _END) = [int(o) for o in _OFFS]

f32 = jnp.float32
bf16 = jnp.bfloat16


def _t5_bucket_np(rel):
    max_exact = REL_BUCKETS // 2
    nf = np.maximum(rel, 1).astype(np.float32)
    large = max_exact + (np.log(nf / np.float32(max_exact))
                         / np.float32(math.log(REL_MAX_DIST / max_exact))
                         * np.float32(REL_BUCKETS - max_exact)).astype(np.int32)
    large = np.minimum(large, REL_BUCKETS - 1)
    return np.where(rel < max_exact, rel, large).astype(np.int32)


def _near_bucket_map():
    tq = np.arange(Q_TILE)[:, None]
    tk = np.arange(2 * Q_TILE)[None, :]
    rel = Q_TILE + tq - tk
    return _t5_bucket_np(np.maximum(rel, 0))


def _rms(x, g):
    ms = jnp.mean(x * x, axis=-1, keepdims=True)
    return x * lax.rsqrt(ms + EPS) * g


def _bias_band_kernel(rb_ref, bmap_ref, out_ref):
    h = pl.program_id(0)
    bmap = bmap_ref[...]
    acc = jnp.zeros(bmap.shape, f32)
    for b in range(REL_BUCKETS):
        acc = jnp.where(bmap == b, rb_ref[b, h], acc)
    out_ref[0] = acc - rb_ref[REL_BUCKETS - 1, h]


def _bias_band(rel_bias):
    bmap = jnp.asarray(_near_bucket_map())
    return pl.pallas_call(
        _bias_band_kernel,
        grid=(A_HEADS,),
        in_specs=[pl.BlockSpec(memory_space=pltpu.SMEM),
                  pl.BlockSpec((Q_TILE, 2 * Q_TILE), lambda h: (0, 0))],
        out_specs=pl.BlockSpec((1, Q_TILE, 2 * Q_TILE), lambda h: (h, 0, 0)),
        out_shape=jax.ShapeDtypeStruct((A_HEADS, Q_TILE, 2 * Q_TILE), f32),
        name="bias_band",
    )(rel_bias, bmap)


def _split2(s):
    hi = s.astype(bf16)
    lo = (s - hi.astype(f32)).astype(bf16)
    return hi, lo


def _proj_a_kernel(x_ref, g_ref, w_ref, qg_ref, kg_ref, bd_ref,
                   q_out, qi_out, kkt_out, vext_out, widx_out, gate_out):
    h = _rms(x_ref[0], g_ref[...]).astype(bf16)

    def proj(lo, hi):
        return jnp.dot(h, w_ref[:, lo:hi], preferred_element_type=f32)


    zkk = proj(3 * A_WIDTH, 3 * A_WIDTH + LANES)
    lane = lax.broadcasted_iota(jnp.int32, zkk.shape, 1)
    is_k = lane < A_HEAD_DIM
    ssk = jnp.sum(jnp.where(is_k, zkk * zkk, 0.0), axis=-1, keepdims=True)
    kn = zkk * lax.rsqrt(ssk * (1.0 / A_HEAD_DIM) + EPS) * kg_ref[...]
    kk = jnp.where(is_k, kn, zkk)
    kkt_out[0] = kk.T.astype(bf16)

    zvw = proj(3 * A_WIDTH + LANES, 3 * A_WIDTH + 2 * LANES)
    vext = jnp.where(lane < A_HEAD_DIM, zvw, jnp.where(lane == A_HEAD_DIM, 1.0, 0.0))
    vext_out[0] = vext.astype(bf16)
    widx_out[0] = zvw[:, A_HEAD_DIM:A_HEAD_DIM + IDX_HEADS] * (IDX_HEADS ** -0.5 * IDX_DIM ** -0.5)

    zq = proj(0, A_WIDTH)
    bd = bd_ref[...]
    ssq = sum(jnp.dot(t, bd, preferred_element_type=f32) for t in _split2(zq * zq))
    qn = zq * lax.rsqrt(ssq * (1.0 / A_HEAD_DIM) + EPS) * qg_ref[...]
    qn = (qn * (A_HEAD_DIM ** -0.5)).astype(bf16)
    zqi = proj(A_WIDTH, 2 * A_WIDTH).astype(bf16)
    for hd in range(A_HEADS):
        sl = slice(hd * A_HEAD_DIM, (hd + 1) * A_HEAD_DIM)
        q_out[0, hd] = qn[:, sl]
        qi_out[0, hd] = zqi[:, sl]

    gate_out[0] = proj(2 * A_WIDTH, 3 * A_WIDTH)


def _proj_a(x, norm_g, w_a, qg, kg, bd):
    B, S, _ = x.shape
    T = TOKEN_TILE
    n_a = w_a.shape[1]
    const = lambda b, i: (0, 0)
    return pl.pallas_call(
        _proj_a_kernel,
        grid=(B, S // T),
        in_specs=[pl.BlockSpec((1, T, D_MODEL), lambda b, i: (b, i, 0)),
                  pl.BlockSpec((1, D_MODEL), const),
                  pl.BlockSpec((D_MODEL, n_a), const),
                  pl.BlockSpec((1, A_WIDTH), const),
                  pl.BlockSpec((1, LANES), const),
                  pl.BlockSpec((A_WIDTH, A_WIDTH), const)],
        out_specs=[pl.BlockSpec((1, A_HEADS, T, A_HEAD_DIM), lambda b, i: (b, 0, i, 0)),
                   pl.BlockSpec((1, IDX_HEADS, T, IDX_DIM), lambda b, i: (b, 0, i, 0)),
                   pl.BlockSpec((1, LANES, T), lambda b, i: (b, 0, i)),
                   pl.BlockSpec((1, T, LANES), lambda b, i: (b, i, 0)),
                   pl.BlockSpec((1, T, IDX_HEADS), lambda b, i: (b, i, 0)),
                   pl.BlockSpec((1, T, A_WIDTH), lambda b, i: (b, i, 0))],
        out_shape=[jax.ShapeDtypeStruct((B, A_HEADS, S, A_HEAD_DIM), bf16),
                   jax.ShapeDtypeStruct((B, IDX_HEADS, S, IDX_DIM), bf16),
                   jax.ShapeDtypeStruct((B, LANES, S), bf16),
                   jax.ShapeDtypeStruct((B, S, LANES), bf16),
                   jax.ShapeDtypeStruct((B, S, IDX_HEADS), f32),
                   jax.ShapeDtypeStruct((B, S, A_WIDTH), f32)],
        compiler_params=pltpu.CompilerParams(
            dimension_semantics=("arbitrary", "arbitrary"),
            vmem_limit_bytes=VMEM_LIMIT_BYTES),
        name="proj_a",
    )(x, norm_g, w_a, qg, kg, bd)


def _count(pred):
    return jnp.sum(jnp.where(pred, 1.0, 0.0), axis=1, keepdims=True)


def _attn_block(c, q_ref, qi_ref, kkt_ref, vext_ref, widx_ref, gate_ref, band_ref, upre_ref, ltri_ref,
                y_out, sc_ref, cut_ref, yh_ref, wb_ref, lg_ref):
    nk = (c + 1) * Q_TILE
    k_top = float(TOPK_MAX)
    kt = kkt_ref[0, 0:A_HEAD_DIM, 0:nk]
    vext = vext_ref[0, 0:nk, :]
    row = lax.broadcasted_iota(jnp.int32, (Q_TILE, nk), 0) + c * Q_TILE
    col = lax.broadcasted_iota(jnp.int32, (Q_TILE, nk), 1)
    causal = col <= row

    if nk > TOPK_MAX:
        kit = kkt_ref[0, A_HEAD_DIM:2 * A_HEAD_DIM, 0:nk]
        widx = widx_ref[0]

        for hd in range(IDX_HEADS):
            wb_ref[hd] = jnp.broadcast_to(widx[:, hd:hd + 1], (Q_TILE, LANES))
        sc_ref[:, 0:nk] = jnp.zeros((Q_TILE, nk), f32)

        def idx_head(hd, carry):
            s = jnp.dot(qi_ref[0, hd], kit, preferred_element_type=f32)
            w = jnp.concatenate([wb_ref[hd]] * (nk // LANES), axis=1)
            sc_ref[:, 0:nk] = sc_ref[:, 0:nk] + w * jnp.maximum(s, 0.0)
            return carry

        lax.fori_loop(0, IDX_HEADS, idx_head, 0)
        sc_ref[:, 0:nk] = jnp.where(causal, sc_ref[:, 0:nk], -jnp.inf)

        def scores():
            return sc_ref[:, 0:nk]

        sc = scores()
        lo0 = jnp.min(jnp.where(causal, sc, jnp.inf), axis=1, keepdims=True)
        hi0 = jnp.max(sc, axis=1, keepdims=True)
        clo0 = (lax.broadcasted_iota(jnp.int32, (Q_TILE, 1), 0) + (c * Q_TILE + 1)).astype(f32)

        def halve(_, st):
            lo, hi, clo = st
            mid = lo + (hi - lo) * 0.5
            cnt = _count(scores() >= mid)
            ge = cnt >= k_top
            return jnp.where(ge, mid, lo), jnp.where(ge, hi, mid), jnp.where(ge, cnt, clo)

        def candidate(lo):
            s = scores()
            a = jnp.min(jnp.where(s >= lo, s, jnp.inf), axis=1, keepdims=True)
            return a, _count(s > a)

        lo, _, clo = lax.fori_loop(0, BISECT_ITERS, halve, (lo0, hi0, clo0), unroll=BISECT_UNROLL)
        a, cgt = candidate(lo)

        def unfinished(st):
            return jnp.max(st[2]) >= k_top

        def refine(st):
            clo, a, cgt = st
            s = scores()
            nxt = jnp.min(jnp.where(s > a, s, jnp.inf), axis=1, keepdims=True)
            todo = cgt >= k_top
            lo = jnp.where(todo, nxt, a)
            clo = jnp.where(todo, cgt, clo)
            a, cgt = candidate(lo)
            return clo, a, cgt

        cge, thr, cgt = lax.while_loop(unfinished, refine, (clo, a, cgt))

        cut_ref[...] = jnp.full((Q_TILE, 1), nk, jnp.int32)

        @pl.when(jnp.max(cge) > k_top)
        def _():
            need = k_top - cgt
            eqf = jnp.where(scores() == thr, 1.0, 0.0)
            n_grp = nk // LANES
            before = jnp.dot(eqf.astype(bf16), upre_ref[0:nk, :],
                             preferred_element_type=f32)
            lane = lax.broadcasted_iota(jnp.int32, (Q_TILE, LANES), 1)
            jstar = _count((lane < n_grp) & (before < need)) - 1.0
            base = jnp.sum(jnp.where(lane.astype(f32) == jstar, before, 0.0), axis=1, keepdims=True)
            grp = jnp.zeros((Q_TILE, LANES), f32)
            for j in range(n_grp):
                grp = jnp.where(jstar == float(j), eqf[:, j * LANES:(j + 1) * LANES], grp)
            upto = jnp.dot(grp.astype(bf16), ltri_ref[...],
                           preferred_element_type=f32)
            lstar = _count(upto < need - base)
            cut_ref[...] = (jstar * LANES + lstar).astype(jnp.int32)

        s = scores()
        keep = (s > thr) | ((s == thr) & (col <= cut_ref[...]))
    else:
        keep = causal
    sc_ref[:, 0:nk] = jnp.where(keep, 0.0, -jnp.inf)
    near_w = min(nk, 2 * Q_TILE)

    def masked_logits(hd):
        logit = jnp.dot(q_ref[0, hd], kt, preferred_element_type=f32)
        near = logit[:, nk - near_w:] + (band_ref[hd][:, 2 * Q_TILE - near_w:] + sc_ref[:, nk - near_w:nk])
        if nk > near_w:
            far = logit[:, :nk - near_w] + sc_ref[:, 0:nk - near_w]
            logit = jnp.concatenate([far, near], axis=1)
        else:
            logit = near
        return logit, jnp.max(logit, axis=1, keepdims=True)

    def weighted_values(hd, m):
        p = jnp.exp(lg_ref[:, 0:nk] - m).astype(bf16)
        yh_ref[hd] = jnp.dot(p, vext, preferred_element_type=f32)

    def head(hd, m_prev):
        p = jnp.exp(lg_ref[:, 0:nk] - m_prev).astype(bf16)
        logit, m = masked_logits(hd)
        lg_ref[:, 0:nk] = logit
        yh_ref[hd - 1] = jnp.dot(p, vext, preferred_element_type=f32)
        return m

    logit0, m0 = masked_logits(0)
    lg_ref[:, 0:nk] = logit0
    m_last = lax.fori_loop(1, A_HEADS, head, m0)
    weighted_values(A_HEADS - 1, m_last)
    ys = []
    for hd in range(A_HEADS):
        pv = yh_ref[hd]
        ys.append(pv[:, 0:A_HEAD_DIM] / pv[:, A_HEAD_DIM:A_HEAD_DIM + 1])
    y = jnp.concatenate(ys, axis=1)
    g = gate_ref[0]
    y_out[0] = (y * (g * jax.nn.sigmoid(g))).astype(bf16)


def _attn_kernel(q_ref, qi_ref, kkt_ref, vext_ref, widx_ref, gate_ref, band_ref, upre_ref, ltri_ref,
                 y_out, sc_ref, cut_ref, yh_ref, wb_ref, lg_ref):
    i = pl.program_id(0)
    for c in range(kkt_ref.shape[2] // Q_TILE):
        @pl.when(i == c)
        def _(c=c):
            _attn_block(c, q_ref, qi_ref, kkt_ref, vext_ref, widx_ref, gate_ref, band_ref,
                        upre_ref, ltri_ref, y_out, sc_ref, cut_ref, yh_ref, wb_ref, lg_ref)


def _attn(q, qi, kkt, vext, widx, gate, band):
    B, _, S, _ = q.shape
    key_group = np.arange(S) // LANES
    upre = jnp.asarray(key_group[:, None] < np.arange(LANES)[None, :], dtype=bf16)
    ltri = jnp.asarray(np.arange(LANES)[:, None] <= np.arange(LANES)[None, :], dtype=bf16)
    return pl.pallas_call(
        _attn_kernel,
        grid=(S // Q_TILE, B),
        in_specs=[pl.BlockSpec((1, A_HEADS, Q_TILE, A_HEAD_DIM), lambda i, b: (b, 0, i, 0)),
                  pl.BlockSpec((1, IDX_HEADS, Q_TILE, IDX_DIM), lambda i, b: (b, 0, i, 0)),
                  pl.BlockSpec((1, LANES, S), lambda i, b: (b, 0, 0)),
                  pl.BlockSpec((1, S, LANES), lambda i, b: (b, 0, 0)),
                  pl.BlockSpec((1, Q_TILE, IDX_HEADS), lambda i, b: (b, i, 0)),
                  pl.BlockSpec((1, Q_TILE, A_WIDTH), lambda i, b: (b, i, 0)),
                  pl.BlockSpec((A_HEADS, Q_TILE, 2 * Q_TILE), lambda i, b: (0, 0, 0)),
                  pl.BlockSpec((S, LANES), lambda i, b: (0, 0)),
                  pl.BlockSpec((LANES, LANES), lambda i, b: (0, 0))],
        out_specs=pl.BlockSpec((1, Q_TILE, A_WIDTH), lambda i, b: (b, i, 0)),
        out_shape=jax.ShapeDtypeStruct((B, S, A_WIDTH), bf16),
        scratch_shapes=[pltpu.VMEM((Q_TILE, S), f32),
                        pltpu.VMEM((Q_TILE, 1), jnp.int32),
                        pltpu.VMEM((A_HEADS, Q_TILE, LANES), f32),
                        pltpu.VMEM((IDX_HEADS, Q_TILE, LANES), f32),
                        pltpu.VMEM((Q_TILE, S), f32)],
        compiler_params=pltpu.CompilerParams(
            dimension_semantics=("arbitrary", "arbitrary"),
            vmem_limit_bytes=VMEM_LIMIT_BYTES),
        name="attn",
    )(q, qi, kkt, vext, widx, gate, band, upre, ltri)


def _mix_kernel(x_ref, ya_ref, g_ref, w_ref, lng_ref, lnb_ref, ws_ref, bst_ref, wbr_ref, wo_ref,
                out_ref):
    x = x_ref[...]
    h = _rms(x, g_ref[...]).astype(bf16)
    T = x.shape[0]

    def proj(lo, hi):
        return jnp.dot(h, w_ref[:, lo:hi], preferred_element_type=f32)

    u = jax.nn.gelu(proj(0, B_WIDTH))
    vb = jax.nn.gelu(proj(B_WIDTH, 2 * B_WIDTH))
    mu = jnp.mean(vb, axis=-1, keepdims=True)
    var = jnp.mean(jnp.square(vb - mu), axis=-1, keepdims=True)
    vln = ((vb - mu) * lax.rsqrt(var + EPS) * lng_ref[...] + lnb_ref[...]).astype(bf16)

    n_ch = T // CHUNK
    tril = (lax.broadcasted_iota(jnp.int32, (CHUNK, CHUNK), 1)
            <= lax.broadcasted_iota(jnp.int32, (CHUNK, CHUNK), 0))
    bst = bst_ref[...]
    per_group = []
    for gi in range(B_GROUPS):
        wg = jnp.where(tril, ws_ref[gi], 0.0).astype(bf16)
        cols = slice(gi * B_GROUP_DIM, (gi + 1) * B_GROUP_DIM)
        vg = jnp.concatenate([vln[ci * CHUNK:(ci + 1) * CHUNK, cols] for ci in range(n_ch)], axis=1)
        sg = jnp.dot(wg, vg, preferred_element_type=f32) + bst[:, gi:gi + 1]
        per_group.append(sg)
    s = jnp.concatenate(
        [jnp.concatenate([per_group[gi][:, ci * B_GROUP_DIM:(ci + 1) * B_GROUP_DIM]
                          for gi in range(B_GROUPS)], axis=1)
         for ci in range(n_ch)], axis=0)

    gb = proj(2 * B_WIDTH, 3 * B_WIDTH)
    yb = (u * s * (gb * jax.nn.sigmoid(gb))).astype(bf16)
    yd_a = jnp.dot(ya_ref[...], wbr_ref[0], preferred_element_type=f32)
    yd_b = jnp.dot(yb, wbr_ref[1], preferred_element_type=f32)
    ma = proj(3 * B_WIDTH, 3 * B_WIDTH + D_MODEL)
    mb = proj(3 * B_WIDTH + D_MODEL, 3 * B_WIDTH + 2 * D_MODEL)
    merged = jax.nn.sigmoid(ma) * yd_a + jax.nn.sigmoid(mb) * yd_b
    out_ref[...] = x + jnp.dot(merged.astype(bf16), wo_ref[...], preferred_element_type=f32)


def _mix(x2, ya2, norm_g, w_b, ln_g, ln_b, w_sp, b_sp_t, w_br, w_o):
    N = x2.shape[0]
    T = TOKEN_TILE
    n_b = w_b.shape[1]
    c2 = lambda i: (0, 0)
    c3 = lambda i: (0, 0, 0)
    return pl.pallas_call(
        _mix_kernel,
        grid=(N // T,),
        in_specs=[pl.BlockSpec((T, D_MODEL), lambda i: (i, 0)),
                  pl.BlockSpec((T, A_WIDTH), lambda i: (i, 0)),
                  pl.BlockSpec((1, D_MODEL), c2),
                  pl.BlockSpec((D_MODEL, n_b), c2),
                  pl.BlockSpec((1, B_WIDTH), c2),
                  pl.BlockSpec((1, B_WIDTH), c2),
                  pl.BlockSpec((B_GROUPS, CHUNK, CHUNK), c3),
                  pl.BlockSpec((CHUNK, B_GROUPS), c2),
                  pl.BlockSpec((2, A_WIDTH, D_MODEL), c3),
                  pl.BlockSpec((D_MODEL, D_MODEL), c2)],
        out_specs=pl.BlockSpec((T, D_MODEL), lambda i: (i, 0)),
        out_shape=jax.ShapeDtypeStruct((N, D_MODEL), f32),
        compiler_params=pltpu.CompilerParams(
            dimension_semantics=("arbitrary",),
            vmem_limit_bytes=VMEM_LIMIT_BYTES),
        name="mix",
    )(x2, ya2, norm_g, w_b, ln_g, ln_b, w_sp, b_sp_t, w_br, w_o)


def _attn_side_weights(w):
    pad = jnp.zeros((D_MODEL, LANES - A_HEAD_DIM - IDX_HEADS), w.dtype)
    return jnp.concatenate(
        [w[:, _Q0:_K0], w[:, _QI0:_KI0], w[:, _GA0:_QI0],
         w[:, _K0:_V0], w[:, _KI0:_WI0],
         w[:, _V0:_GA0], w[:, _WI0:_U0], pad], axis=1).astype(bf16)


def kernel(x, norm_g, w_in, q_norm_g, k_norm_g, rel_bias, sgu_ln_g, sgu_ln_b,
           w_spatial, b_spatial, w_branch, w_out):
    B, S, D = x.shape
    depth = w_in.shape[0]
    band = _bias_band(rel_bias)
    head_of = np.arange(A_WIDTH) // A_HEAD_DIM
    bd = jnp.asarray(head_of[:, None] == head_of[None, :], dtype=bf16)
    for l in range(depth):
        w_a = _attn_side_weights(w_in[l])
        w_b = w_in[l][:, _U0:_END].astype(bf16)
        g = norm_g[l][None, :]
        qg = jnp.tile(q_norm_g[l], A_HEADS)[None, :]
        kg = jnp.concatenate([k_norm_g[l], jnp.ones((LANES - A_HEAD_DIM,), f32)])[None, :]
        q, qi, kkt, vext, widx, gate = _proj_a(x, g, w_a, qg, kg, bd)
        ya = _attn(q, qi, kkt, vext, widx, gate, band)
        x2 = _mix(x.reshape(B * S, D), ya.reshape(B * S, A_WIDTH), g, w_b,
                  sgu_ln_g[l][None, :], sgu_ln_b[l][None, :], w_spatial[l],
                  b_spatial[l].T, w_branch[l].astype(bf16), w_out[l].astype(bf16))
        x = x2.reshape(B, S, D)
    return x
```

```python
import functools
import math

import numpy as np
import jax
import jax.numpy as jnp
from jax import lax
from jax.experimental import pallas as pl
from jax.experimental.pallas import tpu as pltpu

D_MODEL = 1024
A_HEADS = 8
A_HEAD_DIM = 64
A_WIDTH = A_HEADS * A_HEAD_DIM
IDX_HEADS = 8
IDX_DIM = 64
TOPK_MAX = 256
B_GROUPS = 4
B_GROUP_DIM = 128
B_WIDTH = B_GROUPS * B_GROUP_DIM
CHUNK = 128
REL_BUCKETS = 32
REL_MAX_DIST = 128
EPS = 1e-6

LANES = 128
VMEM_LIMIT_BYTES = 56 * 1024 * 1024

TOKEN_TILE = 512
Q_TILE = 256
BISECT_ITERS = 20
BISECT_UNROLL = 4

_SPLITS = (A_WIDTH, A_HEAD_DIM, A_HEAD_DIM, A_WIDTH, IDX_HEADS * IDX_DIM, IDX_DIM,
           IDX_HEADS, B_WIDTH, B_WIDTH, B_WIDTH, D_MODEL, D_MODEL)
_OFFS = np.concatenate([[0], np.cumsum(_SPLITS)])
(_Q0, _K0, _V0, _GA0, _QI0, _KI0, _WI0, _U0, _VB0, _GB0, _MA0, _MB0, _END) = [int(o) for o in _OFFS]

f32 = jnp.float32
bf16 = jnp.bfloat16


def _t5_bucket_np(rel):
    max_exact = REL_BUCKETS // 2
    nf = np.maximum(rel, 1).astype(np.float32)
    large = max_exact + (np.log(nf / np.float32(max_exact))
                         / np.float32(math.log(REL_MAX_DIST / max_exact))
                         * np.float32(REL_BUCKETS - max_exact)).astype(np.int32)
    large = np.minimum(large, REL_BUCKETS - 1)
    return np.where(rel < max_exact, rel, large).astype(np.int32)


def _near_bucket_map():
    tq = np.arange(Q_TILE)[:, None]
    tk = np.arange(2 * Q_TILE)[None, :]
    rel = Q_TILE + tq - tk
    return _t5_bucket_np(np.maximum(rel, 0))


def _rms(x, g):
    ms = jnp.mean(x * x, axis=-1, keepdims=True)
    return x * lax.rsqrt(ms + EPS) * g


def _bias_band_kernel(rb_ref, bmap_ref, out_ref):
    h = pl.program_id(0)
    bmap = bmap_ref[...]
    acc = jnp.zeros(bmap.shape, f32)
    for b in range(REL_BUCKETS):
        acc = jnp.where(bmap == b, rb_ref[b, h], acc)
    out_ref[0] = acc - rb_ref[REL_BUCKETS - 1, h]


def _bias_band(rel_bias):
    bmap = jnp.asarray(_near_bucket_map())
    return pl.pallas_call(
        _bias_band_kernel,
        grid=(A_HEADS,),
        in_specs=[pl.BlockSpec(memory_space=pltpu.SMEM),
                  pl.BlockSpec((Q_TILE, 2 * Q_TILE), lambda h: (0, 0))],
        out_specs=pl.BlockSpec((1, Q_TILE, 2 * Q_TILE), lambda h: (h, 0, 0)),
        out_shape=jax.ShapeDtypeStruct((A_HEADS, Q_TILE, 2 * Q_TILE), f32),
        name="bias_band",
    )(rel_bias, bmap)


def _split2(s):
    hi = s.astype(bf16)
    lo = (s - hi.astype(f32)).astype(bf16)
    return hi, lo


def _proj_a_kernel(x_ref, g_ref, w_ref, qg_ref, kg_ref, bd_ref,
                   q_out, qi_out, kkt_out, vext_out, widx_out, gate_out):
    h = _rms(x_ref[0], g_ref[...]).astype(bf16)

    def proj(lo, hi):
        return jnp.dot(h, w_ref[:, lo:hi], preferred_element_type=f32)


    zkk = proj(3 * A_WIDTH, 3 * A_WIDTH + LANES)
    lane = lax.broadcasted_iota(jnp.int32, zkk.shape, 1)
    is_k = lane < A_HEAD_DIM
    ssk = jnp.sum(jnp.where(is_k, zkk * zkk, 0.0), axis=-1, keepdims=True)
    kn = zkk * lax.rsqrt(ssk * (1.0 / A_HEAD_DIM) + EPS) * kg_ref[...]
    kk = jnp.where(is_k, kn, zkk)
    kkt_out[0] = kk.T.astype(bf16)

    zvw = proj(3 * A_WIDTH + LANES, 3 * A_WIDTH + 2 * LANES)
    vext = jnp.where(lane < A_HEAD_DIM, zvw, jnp.where(lane == A_HEAD_DIM, 1.0, 0.0))
    vext_out[0] = vext.astype(bf16)
    widx_out[0] = zvw[:, A_HEAD_DIM:A_HEAD_DIM + IDX_HEADS] * (IDX_HEADS ** -0.5 * IDX_DIM ** -0.5)

    zq = proj(0, A_WIDTH)
    bd = bd_ref[...]
    ssq = sum(jnp.dot(t, bd, preferred_element_type=f32) for t in _split2(zq * zq))
    qn = zq * lax.rsqrt(ssq * (1.0 / A_HEAD_DIM) + EPS) * qg_ref[...]
    qn = (qn * (A_HEAD_DIM ** -0.5)).astype(bf16)
    zqi = proj(A_WIDTH, 2 * A_WIDTH).astype(bf16)
    for hd in range(A_HEADS):
        sl = slice(hd * A_HEAD_DIM, (hd + 1) * A_HEAD_DIM)
        q_out[0, hd] = qn[:, sl]
        qi_out[0, hd] = zqi[:, sl]

    gate_out[0] = proj(2 * A_WIDTH, 3 * A_WIDTH)


def _proj_a(x, norm_g, w_a, qg, kg, bd):
    B, S, _ = x.shape
    T = TOKEN_TILE
    n_a = w_a.shape[1]
    const = lambda b, i: (0, 0)
    return pl.pallas_call(
        _proj_a_kernel,
        grid=(B, S // T),
        in_specs=[pl.BlockSpec((1, T, D_MODEL), lambda b, i: (b, i, 0)),
                  pl.BlockSpec((1, D_MODEL), const),
                  pl.BlockSpec((D_MODEL, n_a), const),
                  pl.BlockSpec((1, A_WIDTH), const),
                  pl.BlockSpec((1, LANES), const),
                  pl.BlockSpec((A_WIDTH, A_WIDTH), const)],
        out_specs=[pl.BlockSpec((1, A_HEADS, T, A_HEAD_DIM), lambda b, i: (b, 0, i, 0)),
                   pl.BlockSpec((1, IDX_HEADS, T, IDX_DIM), lambda b, i: (b, 0, i, 0)),
                   pl.BlockSpec((1, LANES, T), lambda b, i: (b, 0, i)),
                   pl.BlockSpec((1, T, LANES), lambda b, i: (b, i, 0)),
                   pl.BlockSpec((1, T, IDX_HEADS), lambda b, i: (b, i, 0)),
                   pl.BlockSpec((1, T, A_WIDTH), lambda b, i: (b, i, 0))],
        out_shape=[jax.ShapeDtypeStruct((B, A_HEADS, S, A_HEAD_DIM), bf16),
                   jax.ShapeDtypeStruct((B, IDX_HEADS, S, IDX_DIM), bf16),
                   jax.ShapeDtypeStruct((B, LANES, S), bf16),
                   jax.ShapeDtypeStruct((B, S, LANES), bf16),
                   jax.ShapeDtypeStruct((B, S, IDX_HEADS), f32),
                   jax.ShapeDtypeStruct((B, S, A_WIDTH), f32)],
        compiler_params=pltpu.CompilerParams(
            dimension_semantics=("arbitrary", "arbitrary"),
            vmem_limit_bytes=VMEM_LIMIT_BYTES),
        name="proj_a",
    )(x, norm_g, w_a, qg, kg, bd)


def _count(pred):
    return jnp.sum(jnp.where(pred, 1.0, 0.0), axis=1, keepdims=True)


def _attn_block(c, q_ref, qi_ref, kkt_ref, vext_ref, widx_ref, gate_ref, band_ref, upre_ref, ltri_ref,
                y_out, sc_ref, cut_ref, yh_ref, wb_ref, lg_ref):
    nk = (c + 1) * Q_TILE
    k_top = float(TOPK_MAX)
    kt = kkt_ref[0, 0:A_HEAD_DIM, 0:nk]
    vext = vext_ref[0, 0:nk, :]
    row = lax.broadcasted_iota(jnp.int32, (Q_TILE, nk), 0) + c * Q_TILE
    col = lax.broadcasted_iota(jnp.int32, (Q_TILE, nk), 1)
    causal = col <= row

    if nk > TOPK_MAX:
        kit = kkt_ref[0, A_HEAD_DIM:2 * A_HEAD_DIM, 0:nk]
        widx = widx_ref[0]

        for hd in range(IDX_HEADS):
            wb_ref[hd] = jnp.broadcast_to(widx[:, hd:hd + 1], (Q_TILE, LANES))
        sc_ref[:, 0:nk] = jnp.zeros((Q_TILE, nk), f32)

        def idx_head(hd, carry):
            s = jnp.dot(qi_ref[0, hd], kit, preferred_element_type=f32)
            w = jnp.concatenate([wb_ref[hd]] * (nk // LANES), axis=1)
            sc_ref[:, 0:nk] = sc_ref[:, 0:nk] + w * jnp.maximum(s, 0.0)
            return carry

        lax.fori_loop(0, IDX_HEADS, idx_head, 0)
        sc_ref[:, 0:nk] = jnp.where(causal, sc_ref[:, 0:nk], -jnp.inf)

        def scores():
            return sc_ref[:, 0:nk]

        sc = scores()
        lo0 = jnp.min(jnp.where(causal, sc, jnp.inf), axis=1, keepdims=True)
        hi0 = jnp.max(sc, axis=1, keepdims=True)
        clo0 = (lax.broadcasted_iota(jnp.int32, (Q_TILE, 1), 0) + (c * Q_TILE + 1)).astype(f32)

        def halve(_, st):
            lo, hi, clo = st
            mid = lo + (hi - lo) * 0.5
            cnt = _count(scores() >= mid)
            ge = cnt >= k_top
            return jnp.where(ge, mid, lo), jnp.where(ge, hi, mid), jnp.where(ge, cnt, clo)

        def candidate(lo):
            s = scores()
            a = jnp.min(jnp.where(s >= lo, s, jnp.inf), axis=1, keepdims=True)
            return a, _count(s > a)

        lo, _, clo = lax.fori_loop(0, BISECT_ITERS, halve, (lo0, hi0, clo0), unroll=BISECT_UNROLL)
        a, cgt = candidate(lo)

        def unfinished(st):
            return jnp.max(st[2]) >= k_top

        def refine(st):
            clo, a, cgt = st
            s = scores()
            nxt = jnp.min(jnp.where(s > a, s, jnp.inf), axis=1, keepdims=True)
            todo = cgt >= k_top
            lo = jnp.where(todo, nxt, a)
            clo = jnp.where(todo, cgt, clo)
            a, cgt = candidate(lo)
            return clo, a, cgt

        cge, thr, cgt = lax.while_loop(unfinished, refine, (clo, a, cgt))

        cut_ref[...] = jnp.full((Q_TILE, 1), nk, jnp.int32)

        @pl.when(jnp.max(cge) > k_top)
        def _():
            need = k_top - cgt
            eqf = jnp.where(scores() == thr, 1.0, 0.0)
            n_grp = nk // LANES
            before = jnp.dot(eqf.astype(bf16), upre_ref[0:nk, :],
                             preferred_element_type=f32)
            lane = lax.broadcasted_iota(jnp.int32, (Q_TILE, LANES), 1)
            jstar = _count((lane < n_grp) & (before < need)) - 1.0
            base = jnp.sum(jnp.where(lane.astype(f32) == jstar, before, 0.0), axis=1, keepdims=True)
            grp = jnp.zeros((Q_TILE, LANES), f32)
            for j in range(n_grp):
                grp = jnp.where(jstar == float(j), eqf[:, j * LANES:(j + 1) * LANES], grp)
            upto = jnp.dot(grp.astype(bf16), ltri_ref[...],
                           preferred_element_type=f32)
            lstar = _count(upto < need - base)
            cut_ref[...] = (jstar * LANES + lstar).astype(jnp.int32)

        s = scores()
        keep = (s > thr) | ((s == thr) & (col <= cut_ref[...]))
    else:
        keep = causal
    sc_ref[:, 0:nk] = jnp.where(keep, 0.0, -jnp.inf)
    near_w = min(nk, 2 * Q_TILE)

    def masked_logits(hd):
        logit = jnp.dot(q_ref[0, hd], kt, preferred_element_type=f32)
        near = logit[:, nk - near_w:] + (band_ref[hd][:, 2 * Q_TILE - near_w:] + sc_ref[:, nk - near_w:nk])
        if nk > near_w:
            far = logit[:, :nk - near_w] + sc_ref[:, 0:nk - near_w]
            logit = jnp.concatenate([far, near], axis=1)
        else:
            logit = near
        return logit, jnp.max(logit, axis=1, keepdims=True)

    def weighted_values(hd, m):
        p = jnp.exp(lg_ref[:, 0:nk] - m).astype(bf16)
        yh_ref[hd] = jnp.dot(p, vext, preferred_element_type=f32)

    def head(hd, m_prev):
        p = jnp.exp(lg_ref[:, 0:nk] - m_prev).astype(bf16)
        logit, m = masked_logits(hd)
        lg_ref[:, 0:nk] = logit
        yh_ref[hd - 1] = jnp.dot(p, vext, preferred_element_type=f32)
        return m

    logit0, m0 = masked_logits(0)
    lg_ref[:, 0:nk] = logit0
    m_last = lax.fori_loop(1, A_HEADS, head, m0)
    weighted_values(A_HEADS - 1, m_last)
    ys = []
    for hd in range(A_HEADS):
        pv = yh_ref[hd]
        ys.append(pv[:, 0:A_HEAD_DIM] / pv[:, A_HEAD_DIM:A_HEAD_DIM + 1])
    y = jnp.concatenate(ys, axis=1)
    g = gate_ref[0]
    y_out[0] = (y * (g * jax.nn.sigmoid(g))).astype(bf16)


def _attn_kernel(q_ref, qi_ref, kkt_ref, vext_ref, widx_ref, gate_ref, band_ref, upre_ref, ltri_ref,
                 y_out, sc_ref, cut_ref, yh_ref, wb_ref, lg_ref):
    i = pl.program_id(1)
    for c in range(kkt_ref.shape[2] // Q_TILE):
        @pl.when(i == c)
        def _(c=c):
            _attn_block(c, q_ref, qi_ref, kkt_ref, vext_ref, widx_ref, gate_ref, band_ref,
                        upre_ref, ltri_ref, y_out, sc_ref, cut_ref, yh_ref, wb_ref, lg_ref)


def _attn(q, qi, kkt, vext, widx, gate, band):
    B, _, S, _ = q.shape
    key_group = np.arange(S) // LANES
    upre = jnp.asarray(key_group[:, None] < np.arange(LANES)[None, :], dtype=bf16)
    ltri = jnp.asarray(np.arange(LANES)[:, None] <= np.arange(LANES)[None, :], dtype=bf16)
    return pl.pallas_call(
        _attn_kernel,
        grid=(B, S // Q_TILE),
        in_specs=[pl.BlockSpec((1, A_HEADS, Q_TILE, A_HEAD_DIM), lambda b, i: (b, 0, i, 0)),
                  pl.BlockSpec((1, IDX_HEADS, Q_TILE, IDX_DIM), lambda b, i: (b, 0, i, 0)),
                  pl.BlockSpec((1, LANES, S), lambda b, i: (b, 0, 0)),
                  pl.BlockSpec((1, S, LANES), lambda b, i: (b, 0, 0)),
                  pl.BlockSpec((1, Q_TILE, IDX_HEADS), lambda b, i: (b, i, 0)),
                  pl.BlockSpec((1, Q_TILE, A_WIDTH), lambda b, i: (b, i, 0)),
                  pl.BlockSpec((A_HEADS, Q_TILE, 2 * Q_TILE), lambda b, i: (0, 0, 0)),
                  pl.BlockSpec((S, LANES), lambda b, i: (0, 0)),
                  pl.BlockSpec((LANES, LANES), lambda b, i: (0, 0))],
        out_specs=pl.BlockSpec((1, Q_TILE, A_WIDTH), lambda b, i: (b, i, 0)),
        out_shape=jax.ShapeDtypeStruct((B, S, A_WIDTH), bf16),
        scratch_shapes=[pltpu.VMEM((Q_TILE, S), f32),
                        pltpu.VMEM((Q_TILE, 1), jnp.int32),
                        pltpu.VMEM((A_HEADS, Q_TILE, LANES), f32),
                        pltpu.VMEM((IDX_HEADS, Q_TILE, LANES), f32),
                        pltpu.VMEM((Q_TILE, S), f32)],
        compiler_params=pltpu.CompilerParams(
            dimension_semantics=("arbitrary", "arbitrary"),
            vmem_limit_bytes=VMEM_LIMIT_BYTES),
        name="attn",
    )(q, qi, kkt, vext, widx, gate, band, upre, ltri)


def _mix_kernel(x_ref, ya_ref, g_ref, w_ref, lng_ref, lnb_ref, ws_ref, bst_ref, wbr_ref, wo_ref,
                out_ref):
    x = x_ref[...]
    h = _rms(x, g_ref[...]).astype(bf16)
    T = x.shape[0]

    def proj(lo, hi):
        return jnp.dot(h, w_ref[:, lo:hi], preferred_element_type=f32)

    u = jax.nn.gelu(proj(0, B_WIDTH))
    vb = jax.nn.gelu(proj(B_WIDTH, 2 * B_WIDTH))
    mu = jnp.mean(vb, axis=-1, keepdims=True)
    var = jnp.mean(jnp.square(vb - mu), axis=-1, keepdims=True)
    vln = ((vb - mu) * lax.rsqrt(var + EPS) * lng_ref[...] + lnb_ref[...]).astype(bf16)

    n_ch = T // CHUNK
    tril = (lax.broadcasted_iota(jnp.int32, (CHUNK, CHUNK), 1)
            <= lax.broadcasted_iota(jnp.int32, (CHUNK, CHUNK), 0))
    bst = bst_ref[...]
    per_group = []
    for gi in range(B_GROUPS):
        wg = jnp.where(tril, ws_ref[gi], 0.0).astype(bf16)
        cols = slice(gi * B_GROUP_DIM, (gi + 1) * B_GROUP_DIM)
        vg = jnp.concatenate([vln[ci * CHUNK:(ci + 1) * CHUNK, cols] for ci in range(n_ch)], axis=1)
        sg = jnp.dot(wg, vg, preferred_element_type=f32) + bst[:, gi:gi + 1]
        per_group.append(sg)
    s = jnp.concatenate(
        [jnp.concatenate([per_group[gi][:, ci * B_GROUP_DIM:(ci + 1) * B_GROUP_DIM]
                          for gi in range(B_GROUPS)], axis=1)
         for ci in range(n_ch)], axis=0)

    gb = proj(2 * B_WIDTH, 3 * B_WIDTH)
    yb = (u * s * (gb * jax.nn.sigmoid(gb))).astype(bf16)
    yd_a = jnp.dot(ya_ref[...], wbr_ref[0], preferred_element_type=f32)
    yd_b = jnp.dot(yb, wbr_ref[1], preferred_element_type=f32)
    ma = proj(3 * B_WIDTH, 3 * B_WIDTH + D_MODEL)
    mb = proj(3 * B_WIDTH + D_MODEL, 3 * B_WIDTH + 2 * D_MODEL)
    merged = jax.nn.sigmoid(ma) * yd_a + jax.nn.sigmoid(mb) * yd_b
    out_ref[...] = x + jnp.dot(merged.astype(bf16), wo_ref[...], preferred_element_type=f32)


def _mix(x2, ya2, norm_g, w_b, ln_g, ln_b, w_sp, b_sp_t, w_br, w_o):
    N = x2.shape[0]
    T = TOKEN_TILE
    n_b = w_b.shape[1]
    c2 = lambda i: (0, 0)
    c3 = lambda i: (0, 0, 0)
    return pl.pallas_call(
        _mix_kernel,
        grid=(N // T,),
        in_specs=[pl.BlockSpec((T, D_MODEL), lambda i: (i, 0)),
                  pl.BlockSpec((T, A_WIDTH), lambda i: (i, 0)),
                  pl.BlockSpec((1, D_MODEL), c2),
                  pl.BlockSpec((D_MODEL, n_b), c2),
                  pl.BlockSpec((1, B_WIDTH), c2),
                  pl.BlockSpec((1, B_WIDTH), c2),
                  pl.BlockSpec((B_GROUPS, CHUNK, CHUNK), c3),
                  pl.BlockSpec((CHUNK, B_GROUPS), c2),
                  pl.BlockSpec((2, A_WIDTH, D_MODEL), c3),
                  pl.BlockSpec((D_MODEL, D_MODEL), c2)],
        out_specs=pl.BlockSpec((T, D_MODEL), lambda i: (i, 0)),
        out_shape=jax.ShapeDtypeStruct((N, D_MODEL), f32),
        compiler_params=pltpu.CompilerParams(
            dimension_semantics=("arbitrary",),
            vmem_limit_bytes=VMEM_LIMIT_BYTES),
        name="mix",
    )(x2, ya2, norm_g, w_b, ln_g, ln_b, w_sp, b_sp_t, w_br, w_o)


def _attn_side_weights(w):
    pad = jnp.zeros((D_MODEL, LANES - A_HEAD_DIM - IDX_HEADS), w.dtype)
    return jnp.concatenate(
        [w[:, _Q0:_K0], w[:, _QI0:_KI0], w[:, _GA0:_QI0],
         w[:, _K0:_V0], w[:, _KI0:_WI0],
         w[:, _V0:_GA0], w[:, _WI0:_U0], pad], axis=1).astype(bf16)


def kernel(x, norm_g, w_in, q_norm_g, k_norm_g, rel_bias, sgu_ln_g, sgu_ln_b,
           w_spatial, b_spatial, w_branch, w_out):
    B, S, D = x.shape
    depth = w_in.shape[0]
    band = _bias_band(rel_bias)
    head_of = np.arange(A_WIDTH) // A_HEAD_DIM
    bd = jnp.asarray(head_of[:, None] == head_of[None, :], dtype=bf16)
    for l in range(depth):
        w_a = _attn_side_weights(w_in[l])
        w_b = w_in[l][:, _U0:_END].astype(bf16)
        g = norm_g[l][None, :]
        qg = jnp.tile(q_norm_g[l], A_HEADS)[None, :]
        kg = jnp.concatenate([k_norm_g[l], jnp.ones((LANES - A_HEAD_DIM,), f32)])[None, :]
        q, qi, kkt, vext, widx, gate = _proj_a(x, g, w_a, qg, kg, bd)
        ya = _attn(q, qi, kkt, vext, widx, gate, band)
        x2 = _mix(x.reshape(B * S, D), ya.reshape(B * S, A_WIDTH), g, w_b,
                  sgu_ln_g[l][None, :], sgu_ln_b[l][None, :], w_spatial[l],
                  b_spatial[l].T, w_branch[l].astype(bf16), w_out[l].astype(bf16))
        x = x2.reshape(B, S, D)
    return x
```

```python
import functools
import math

import numpy as np
import jax
import jax.numpy as jnp
from jax import lax
from jax.experimental import pallas as pl
from jax.experimental.pallas import tpu as pltpu

D_MODEL = 1024
A_HEADS = 8
A_HEAD_DIM = 64
A_WIDTH = A_HEADS * A_HEAD_DIM
IDX_HEADS = 8
IDX_DIM = 64
TOPK_MAX = 256
B_GROUPS = 4
B_GROUP_DIM = 128
B_WIDTH = B_GROUPS * B_GROUP_DIM
CHUNK = 128
REL_BUCKETS = 32
REL_MAX_DIST = 128
EPS = 1e-6

LANES = 128
VMEM_LIMIT_BYTES = 56 * 1024 * 1024

TOKEN_TILE = 512
Q_TILE = 256
BISECT_ITERS = 20
BISECT_UNROLL = 4

_SPLITS = (A_WIDTH, A_HEAD_DIM, A_HEAD_DIM, A_WIDTH, IDX_HEADS * IDX_DIM, IDX_DIM,
           IDX_HEADS, B_WIDTH, B_WIDTH, B_WIDTH, D_MODEL, D_MODEL)
_OFFS = np.concatenate([[0], np.cumsum(_SPLITS)])
(_Q0, _K0, _V0, _GA0, _QI0, _KI0, _WI0, _U0, _VB0, _GB0, _MA0, _MB0, _END) = [int(o) for o in _OFFS]

f32 = jnp.float32
bf16 = jnp.bfloat16


def _t5_bucket_np(rel):
    max_exact = REL_BUCKETS // 2
    nf = np.maximum(rel, 1).astype(np.float32)
    large = max_exact + (np.log(nf / np.float32(max_exact))
                         / np.float32(math.log(REL_MAX_DIST / max_exact))
                         * np.float32(REL_BUCKETS - max_exact)).astype(np.int32)
    large = np.minimum(large, REL_BUCKETS - 1)
    return np.where(rel < max_exact, rel, large).astype(np.int32)


def _near_bucket_map():
    tq = np.arange(Q_TILE)[:, None]
    tk = np.arange(2 * Q_TILE)[None, :]
    rel = Q_TILE + tq - tk
    return _t5_bucket_np(np.maximum(rel, 0))


def _rms(x, g):
    ms = jnp.mean(x * x, axis=-1, keepdims=True)
    return x * lax.rsqrt(ms + EPS) * g


def _bias_band_kernel(rb_ref, bmap_ref, out_ref):
    h = pl.program_id(0)
    bmap = bmap_ref[...]
    acc = jnp.zeros(bmap.shape, f32)
    for b in range(REL_BUCKETS):
        acc = jnp.where(bmap == b, rb_ref[b, h], acc)
    out_ref[0] = acc - rb_ref[REL_BUCKETS - 1, h]


def _bias_band(rel_bias):
    bmap = jnp.asarray(_near_bucket_map())
    return pl.pallas_call(
        _bias_band_kernel,
        grid=(A_HEADS,),
        in_specs=[pl.BlockSpec(memory_space=pltpu.SMEM),
                  pl.BlockSpec((Q_TILE, 2 * Q_TILE), lambda h: (0, 0))],
        out_specs=pl.BlockSpec((1, Q_TILE, 2 * Q_TILE), lambda h: (h, 0, 0)),
        out_shape=jax.ShapeDtypeStruct((A_HEADS, Q_TILE, 2 * Q_TILE), f32),
        name="bias_band",
    )(rel_bias, bmap)


def _split2(s):
    hi = s.astype(bf16)
    lo = (s - hi.astype(f32)).astype(bf16)
    return hi, lo


def _proj_a_kernel(x_ref, g_ref, w_ref, qg_ref, kg_ref, bd_ref,
                   q_out, qi_out, kkt_out, vext_out, widx_out, gate_out):
    h = _rms(x_ref[0], g_ref[...]).astype(bf16)

    def proj(lo, hi):
        return jnp.dot(h, w_ref[:, lo:hi], preferred_element_type=f32)


    zkk = proj(3 * A_WIDTH, 3 * A_WIDTH + LANES)
    lane = lax.broadcasted_iota(jnp.int32, zkk.shape, 1)
    is_k = lane < A_HEAD_DIM
    ssk = jnp.sum(jnp.where(is_k, zkk * zkk, 0.0), axis=-1, keepdims=True)
    kn = zkk * lax.rsqrt(ssk * (1.0 / A_HEAD_DIM) + EPS) * kg_ref[...]
    kk = jnp.where(is_k, kn, zkk)
    kkt_out[0] = kk.T.astype(bf16)

    zvw = proj(3 * A_WIDTH + LANES, 3 * A_WIDTH + 2 * LANES)
    vext = jnp.where(lane < A_HEAD_DIM, zvw, jnp.where(lane == A_HEAD_DIM, 1.0, 0.0))
    vext_out[0] = vext.astype(bf16)
    widx_out[0] = zvw[:, A_HEAD_DIM:A_HEAD_DIM + IDX_HEADS] * (IDX_HEADS ** -0.5 * IDX_DIM ** -0.5)

    zq = proj(0, A_WIDTH)
    bd = bd_ref[...]
    ssq = sum(jnp.dot(t, bd, preferred_element_type=f32) for t in _split2(zq * zq))
    qn = zq * lax.rsqrt(ssq * (1.0 / A_HEAD_DIM) + EPS) * qg_ref[...]
    qn = (qn * (A_HEAD_DIM ** -0.5)).astype(bf16)
    zqi = proj(A_WIDTH, 2 * A_WIDTH).astype(bf16)
    for hd in range(A_HEADS):
        sl = slice(hd * A_HEAD_DIM, (hd + 1) * A_HEAD_DIM)
        q_out[0, hd] = qn[:, sl]
        qi_out[0, hd] = zqi[:, sl]

    gate_out[0] = proj(2 * A_WIDTH, 3 * A_WIDTH)


def _proj_a(x, norm_g, w_a, qg, kg, bd):
    B, S, _ = x.shape
    T = TOKEN_TILE
    n_a = w_a.shape[1]
    const = lambda b, i: (0, 0)
    return pl.pallas_call(
        _proj_a_kernel,
        grid=(B, S // T),
        in_specs=[pl.BlockSpec((1, T, D_MODEL), lambda b, i: (b, i, 0)),
                  pl.BlockSpec((1, D_MODEL), const),
                  pl.BlockSpec((D_MODEL, n_a), const),
                  pl.BlockSpec((1, A_WIDTH), const),
                  pl.BlockSpec((1, LANES), const),
                  pl.BlockSpec((A_WIDTH, A_WIDTH), const)],
        out_specs=[pl.BlockSpec((1, A_HEADS, T, A_HEAD_DIM), lambda b, i: (b, 0, i, 0)),
                   pl.BlockSpec((1, IDX_HEADS, T, IDX_DIM), lambda b, i: (b, 0, i, 0)),
                   pl.BlockSpec((1, LANES, T), lambda b, i: (b, 0, i)),
                   pl.BlockSpec((1, T, LANES), lambda b, i: (b, i, 0)),
                   pl.BlockSpec((1, T, IDX_HEADS), lambda b, i: (b, i, 0)),
                   pl.BlockSpec((1, T, A_WIDTH), lambda b, i: (b, i, 0))],
        out_shape=[jax.ShapeDtypeStruct((B, A_HEADS, S, A_HEAD_DIM), bf16),
                   jax.ShapeDtypeStruct((B, IDX_HEADS, S, IDX_DIM), bf16),
                   jax.ShapeDtypeStruct((B, LANES, S), bf16),
                   jax.ShapeDtypeStruct((B, S, LANES), bf16),
                   jax.ShapeDtypeStruct((B, S, IDX_HEADS), f32),
                   jax.ShapeDtypeStruct((B, S, A_WIDTH), f32)],
        compiler_params=pltpu.CompilerParams(
            dimension_semantics=("arbitrary", "arbitrary"),
            vmem_limit_bytes=VMEM_LIMIT_BYTES),
        name="proj_a",
    )(x, norm_g, w_a, qg, kg, bd)


def _count(pred):
    return jnp.sum(jnp.where(pred, 1.0, 0.0), axis=1, keepdims=True)


def _attn_block(c, q_ref, qi_ref, kkt_ref, vext_ref, widx_ref, gate_ref, band_ref, upre_ref, ltri_ref,
                y_out, sc_ref, cut_ref, yh_ref, wb_ref, lg_ref):
    nk = (c + 1) * Q_TILE
    k_top = float(TOPK_MAX)
    kt = kkt_ref[0, 0:A_HEAD_DIM, 0:nk]
    vext = vext_ref[0, 0:nk, :]
    row = lax.broadcasted_iota(jnp.int32, (Q_TILE, nk), 0) + c * Q_TILE
    col = lax.broadcasted_iota(jnp.int32, (Q_TILE, nk), 1)
    causal = col <= row

    if nk > TOPK_MAX:
        kit = kkt_ref[0, A_HEAD_DIM:2 * A_HEAD_DIM, 0:nk]
        widx = widx_ref[0]

        s_all = jnp.dot(qi_ref[0].reshape(IDX_HEADS * Q_TILE, IDX_DIM), kit,
                        preferred_element_type=f32)
        sc = None
        for hd in range(IDX_HEADS):
            term = widx[:, hd:hd + 1] * jnp.maximum(s_all[hd * Q_TILE:(hd + 1) * Q_TILE], 0.0)
            sc = term if sc is None else sc + term
        sc_ref[:, 0:nk] = jnp.where(causal, sc, -jnp.inf)

        def scores():
            return sc_ref[:, 0:nk]

        sc = scores()
        lo0 = jnp.min(jnp.where(causal, sc, jnp.inf), axis=1, keepdims=True)
        hi0 = jnp.max(sc, axis=1, keepdims=True)
        clo0 = (lax.broadcasted_iota(jnp.int32, (Q_TILE, 1), 0) + (c * Q_TILE + 1)).astype(f32)

        def halve(_, st):
            lo, hi, clo = st
            mid = lo + (hi - lo) * 0.5
            cnt = _count(scores() >= mid)
            ge = cnt >= k_top
            return jnp.where(ge, mid, lo), jnp.where(ge, hi, mid), jnp.where(ge, cnt, clo)

        def candidate(lo):
            s = scores()
            a = jnp.min(jnp.where(s >= lo, s, jnp.inf), axis=1, keepdims=True)
            return a, _count(s > a)

        lo, _, clo = lax.fori_loop(0, BISECT_ITERS, halve, (lo0, hi0, clo0), unroll=BISECT_UNROLL)
        a, cgt = candidate(lo)

        def unfinished(st):
            return jnp.max(st[2]) >= k_top

        def refine(st):
            clo, a, cgt = st
            s = scores()
            nxt = jnp.min(jnp.where(s > a, s, jnp.inf), axis=1, keepdims=True)
            todo = cgt >= k_top
            lo = jnp.where(todo, nxt, a)
            clo = jnp.where(todo, cgt, clo)
            a, cgt = candidate(lo)
            return clo, a, cgt

        cge, thr, cgt = lax.while_loop(unfinished, refine, (clo, a, cgt))

        cut_ref[...] = jnp.full((Q_TILE, 1), nk, jnp.int32)

        @pl.when(jnp.max(cge) > k_top)
        def _():
            need = k_top - cgt
            eqf = jnp.where(scores() == thr, 1.0, 0.0)
            n_grp = nk // LANES
            before = jnp.dot(eqf.astype(bf16), upre_ref[0:nk, :],
                             preferred_element_type=f32)
            lane = lax.broadcasted_iota(jnp.int32, (Q_TILE, LANES), 1)
            jstar = _count((lane < n_grp) & (before < need)) - 1.0
            base = jnp.sum(jnp.where(lane.astype(f32) == jstar, before, 0.0), axis=1, keepdims=True)
            grp = jnp.zeros((Q_TILE, LANES), f32)
            for j in range(n_grp):
                grp = jnp.where(jstar == float(j), eqf[:, j * LANES:(j + 1) * LANES], grp)
            upto = jnp.dot(grp.astype(bf16), ltri_ref[...],
                           preferred_element_type=f32)
            lstar = _count(upto < need - base)
            cut_ref[...] = (jstar * LANES + lstar).astype(jnp.int32)

        s = scores()
        keep = (s > thr) | ((s == thr) & (col <= cut_ref[...]))
    else:
        keep = causal
    sc_ref[:, 0:nk] = jnp.where(keep, 0.0, -jnp.inf)
    near_w = min(nk, 2 * Q_TILE)

    def masked_logits(hd):
        logit = jnp.dot(q_ref[0, hd], kt, preferred_element_type=f32)
        near = logit[:, nk - near_w:] + (band_ref[hd][:, 2 * Q_TILE - near_w:] + sc_ref[:, nk - near_w:nk])
        if nk > near_w:
            far = logit[:, :nk - near_w] + sc_ref[:, 0:nk - near_w]
            logit = jnp.concatenate([far, near], axis=1)
        else:
            logit = near
        return logit, jnp.max(logit, axis=1, keepdims=True)

    def weighted_values(hd, m):
        p = jnp.exp(lg_ref[:, 0:nk] - m).astype(bf16)
        yh_ref[hd] = jnp.dot(p, vext, preferred_element_type=f32)

    def head(hd, m_prev):
        p = jnp.exp(lg_ref[:, 0:nk] - m_prev).astype(bf16)
        logit, m = masked_logits(hd)
        lg_ref[:, 0:nk] = logit
        yh_ref[hd - 1] = jnp.dot(p, vext, preferred_element_type=f32)
        return m

    logit0, m0 = masked_logits(0)
    lg_ref[:, 0:nk] = logit0
    m_last = lax.fori_loop(1, A_HEADS, head, m0)
    weighted_values(A_HEADS - 1, m_last)
    ys = []
    for hd in range(A_HEADS):
        pv = yh_ref[hd]
        ys.append(pv[:, 0:A_HEAD_DIM] / pv[:, A_HEAD_DIM:A_HEAD_DIM + 1])
    y = jnp.concatenate(ys, axis=1)
    g = gate_ref[0]
    y_out[0] = (y * (g * jax.nn.sigmoid(g))).astype(bf16)


def _attn_kernel(q_ref, qi_ref, kkt_ref, vext_ref, widx_ref, gate_ref, band_ref, upre_ref, ltri_ref,
                 y_out, sc_ref, cut_ref, yh_ref, wb_ref, lg_ref):
    i = pl.program_id(0)
    for c in range(kkt_ref.shape[2] // Q_TILE):
        @pl.when(i == c)
        def _(c=c):
            _attn_block(c, q_ref, qi_ref, kkt_ref, vext_ref, widx_ref, gate_ref, band_ref,
                        upre_ref, ltri_ref, y_out, sc_ref, cut_ref, yh_ref, wb_ref, lg_ref)


def _attn(q, qi, kkt, vext, widx, gate, band):
    B, _, S, _ = q.shape
    key_group = np.arange(S) // LANES
    upre = jnp.asarray(key_group[:, None] < np.arange(LANES)[None, :], dtype=bf16)
    ltri = jnp.asarray(np.arange(LANES)[:, None] <= np.arange(LANES)[None, :], dtype=bf16)
    return pl.pallas_call(
        _attn_kernel,
        grid=(S // Q_TILE, B),
        in_specs=[pl.BlockSpec((1, A_HEADS, Q_TILE, A_HEAD_DIM), lambda i, b: (b, 0, i, 0)),
                  pl.BlockSpec((1, IDX_HEADS, Q_TILE, IDX_DIM), lambda i, b: (b, 0, i, 0)),
                  pl.BlockSpec((1, LANES, S), lambda i, b: (b, 0, 0)),
                  pl.BlockSpec((1, S, LANES), lambda i, b: (b, 0, 0)),
                  pl.BlockSpec((1, Q_TILE, IDX_HEADS), lambda i, b: (b, i, 0)),
                  pl.BlockSpec((1, Q_TILE, A_WIDTH), lambda i, b: (b, i, 0)),
                  pl.BlockSpec((A_HEADS, Q_TILE, 2 * Q_TILE), lambda i, b: (0, 0, 0)),
                  pl.BlockSpec((S, LANES), lambda i, b: (0, 0)),
                  pl.BlockSpec((LANES, LANES), lambda i, b: (0, 0))],
        out_specs=pl.BlockSpec((1, Q_TILE, A_WIDTH), lambda i, b: (b, i, 0)),
        out_shape=jax.ShapeDtypeStruct((B, S, A_WIDTH), bf16),
        scratch_shapes=[pltpu.VMEM((Q_TILE, S), f32),
                        pltpu.VMEM((Q_TILE, 1), jnp.int32),
                        pltpu.VMEM((A_HEADS, Q_TILE, LANES), f32),
                        pltpu.VMEM((IDX_HEADS, Q_TILE, LANES), f32),
                        pltpu.VMEM((Q_TILE, S), f32)],
        compiler_params=pltpu.CompilerParams(
            dimension_semantics=("arbitrary", "arbitrary"),
            vmem_limit_bytes=VMEM_LIMIT_BYTES),
        name="attn",
    )(q, qi, kkt, vext, widx, gate, band, upre, ltri)


def _mix_kernel(x_ref, ya_ref, g_ref, w_ref, lng_ref, lnb_ref, ws_ref, bst_ref, wbr_ref, wo_ref,
                out_ref):
    x = x_ref[...]
    h = _rms(x, g_ref[...]).astype(bf16)
    T = x.shape[0]

    def proj(lo, hi):
        return jnp.dot(h, w_ref[:, lo:hi], preferred_element_type=f32)

    u = jax.nn.gelu(proj(0, B_WIDTH))
    vb = jax.nn.gelu(proj(B_WIDTH, 2 * B_WIDTH))
    mu = jnp.mean(vb, axis=-1, keepdims=True)
    var = jnp.mean(jnp.square(vb - mu), axis=-1, keepdims=True)
    vln = ((vb - mu) * lax.rsqrt(var + EPS) * lng_ref[...] + lnb_ref[...]).astype(bf16)

    n_ch = T // CHUNK
    tril = (lax.broadcasted_iota(jnp.int32, (CHUNK, CHUNK), 1)
            <= lax.broadcasted_iota(jnp.int32, (CHUNK, CHUNK), 0))
    bst = bst_ref[...]
    per_group = []
    for gi in range(B_GROUPS):
        wg = jnp.where(tril, ws_ref[gi], 0.0).astype(bf16)
        cols = slice(gi * B_GROUP_DIM, (gi + 1) * B_GROUP_DIM)
        vg = jnp.concatenate([vln[ci * CHUNK:(ci + 1) * CHUNK, cols] for ci in range(n_ch)], axis=1)
        sg = jnp.dot(wg, vg, preferred_element_type=f32) + bst[:, gi:gi + 1]
        per_group.append(sg)
    s = jnp.concatenate(
        [jnp.concatenate([per_group[gi][:, ci * B_GROUP_DIM:(ci + 1) * B_GROUP_DIM]
                          for gi in range(B_GROUPS)], axis=1)
         for ci in range(n_ch)], axis=0)

    gb = proj(2 * B_WIDTH, 3 * B_WIDTH)
    yb = (u * s * (gb * jax.nn.sigmoid(gb))).astype(bf16)
    yd_a = jnp.dot(ya_ref[...], wbr_ref[0], preferred_element_type=f32)
    yd_b = jnp.dot(yb, wbr_ref[1], preferred_element_type=f32)
    ma = proj(3 * B_WIDTH, 3 * B_WIDTH + D_MODEL)
    mb = proj(3 * B_WIDTH + D_MODEL, 3 * B_WIDTH + 2 * D_MODEL)
    merged = jax.nn.sigmoid(ma) * yd_a + jax.nn.sigmoid(mb) * yd_b
    out_ref[...] = x + jnp.dot(merged.astype(bf16), wo_ref[...], preferred_element_type=f32)


def _mix(x2, ya2, norm_g, w_b, ln_g, ln_b, w_sp, b_sp_t, w_br, w_o):
    N = x2.shape[0]
    T = TOKEN_TILE
    n_b = w_b.shape[1]
    c2 = lambda i: (0, 0)
    c3 = lambda i: (0, 0, 0)
    return pl.pallas_call(
        _mix_kernel,
        grid=(N // T,),
        in_specs=[pl.BlockSpec((T, D_MODEL), lambda i: (i, 0)),
                  pl.BlockSpec((T, A_WIDTH), lambda i: (i, 0)),
                  pl.BlockSpec((1, D_MODEL), c2),
                  pl.BlockSpec((D_MODEL, n_b), c2),
                  pl.BlockSpec((1, B_WIDTH), c2),
                  pl.BlockSpec((1, B_WIDTH), c2),
                  pl.BlockSpec((B_GROUPS, CHUNK, CHUNK), c3),
                  pl.BlockSpec((CHUNK, B_GROUPS), c2),
                  pl.BlockSpec((2, A_WIDTH, D_MODEL), c3),
                  pl.BlockSpec((D_MODEL, D_MODEL), c2)],
        out_specs=pl.BlockSpec((T, D_MODEL), lambda i: (i, 0)),
        out_shape=jax.ShapeDtypeStruct((N, D_MODEL), f32),
        compiler_params=pltpu.CompilerParams(
            dimension_semantics=("arbitrary",),
            vmem_limit_bytes=VMEM_LIMIT_BYTES),
        name="mix",
    )(x2, ya2, norm_g, w_b, ln_g, ln_b, w_sp, b_sp_t, w_br, w_o)


def _attn_side_weights(w):
    pad = jnp.zeros((D_MODEL, LANES - A_HEAD_DIM - IDX_HEADS), w.dtype)
    return jnp.concatenate(
        [w[:, _Q0:_K0], w[:, _QI0:_KI0], w[:, _GA0:_QI0],
         w[:, _K0:_V0], w[:, _KI0:_WI0],
         w[:, _V0:_GA0], w[:, _WI0:_U0], pad], axis=1).astype(bf16)


def kernel(x, norm_g, w_in, q_norm_g, k_norm_g, rel_bias, sgu_ln_g, sgu_ln_b,
           w_spatial, b_spatial, w_branch, w_out):
    B, S, D = x.shape
    depth = w_in.shape[0]
    band = _bias_band(rel_bias)
    head_of = np.arange(A_WIDTH) // A_HEAD_DIM
    bd = jnp.asarray(head_of[:, None] == head_of[None, :], dtype=bf16)
    for l in range(depth):
        w_a = _attn_side_weights(w_in[l])
        w_b = w_in[l][:, _U0:_END].astype(bf16)
        g = norm_g[l][None, :]
        qg = jnp.tile(q_norm_g[l], A_HEADS)[None, :]
        kg = jnp.concatenate([k_norm_g[l], jnp.ones((LANES - A_HEAD_DIM,), f32)])[None, :]
        q, qi, kkt, vext, widx, gate = _proj_a(x, g, w_a, qg, kg, bd)
        ya = _attn(q, qi, kkt, vext, widx, gate, band)
        x2 = _mix(x.reshape(B * S, D), ya.reshape(B * S, A_WIDTH), g, w_b,
                  sgu_ln_g[l][None, :], sgu_ln_b[l][None, :], w_spatial[l],
                  b_spatial[l].T, w_branch[l].astype(bf16), w_out[l].astype(bf16))
        x = x2.reshape(B, S, D)
    return x
```

```python
import functools
import math

import numpy as np
import jax
import jax.numpy as jnp
from jax import lax
from jax.experimental import pallas as pl
from jax.experimental.pallas import tpu as pltpu

D_MODEL = 1024
A_HEADS = 8
A_HEAD_DIM = 64
A_WIDTH = A_HEADS * A_HEAD_DIM
IDX_HEADS = 8
IDX_DIM = 64
TOPK_MAX = 256
B_GROUPS = 4
B_GROUP_DIM = 128
B_WIDTH = B_GROUPS * B_GROUP_DIM
CHUNK = 128
REL_BUCKETS = 32
REL_MAX_DIST = 128
EPS = 1e-6

LANES = 128
VMEM_LIMIT_BYTES = 56 * 1024 * 1024

TOKEN_TILE = 512
Q_TILE = 256
IDX_STACK = 4
BISECT_ITERS = 20
BISECT_UNROLL = 4

_SPLITS = (A_WIDTH, A_HEAD_DIM, A_HEAD_DIM, A_WIDTH, IDX_HEADS * IDX_DIM, IDX_DIM,
           IDX_HEADS, B_WIDTH, B_WIDTH, B_WIDTH, D_MODEL, D_MODEL)
_OFFS = np.concatenate([[0], np.cumsum(_SPLITS)])
(_Q0, _K0, _V0, _GA0, _QI0, _KI0, _WI0, _U0, _VB0, _GB0, _MA0, _MB0, _END) = [int(o) for o in _OFFS]

f32 = jnp.float32
bf16 = jnp.bfloat16


def _t5_bucket_np(rel):
    max_exact = REL_BUCKETS // 2
    nf = np.maximum(rel, 1).astype(np.float32)
    large = max_exact + (np.log(nf / np.float32(max_exact))
                         / np.float32(math.log(REL_MAX_DIST / max_exact))
                         * np.float32(REL_BUCKETS - max_exact)).astype(np.int32)
    large = np.minimum(large, REL_BUCKETS - 1)
    return np.where(rel < max_exact, rel, large).astype(np.int32)


def _near_bucket_map():
    tq = np.arange(Q_TILE)[:, None]
    tk = np.arange(2 * Q_TILE)[None, :]
    rel = Q_TILE + tq - tk
    return _t5_bucket_np(np.maximum(rel, 0))


def _rms(x, g):
    ms = jnp.mean(x * x, axis=-1, keepdims=True)
    return x * lax.rsqrt(ms + EPS) * g


def _bias_band_kernel(rb_ref, bmap_ref, out_ref):
    h = pl.program_id(0)
    bmap = bmap_ref[...]
    acc = jnp.zeros(bmap.shape, f32)
    for b in range(REL_BUCKETS):
        acc = jnp.where(bmap == b, rb_ref[b, h], acc)
    out_ref[0] = acc - rb_ref[REL_BUCKETS - 1, h]


def _bias_band(rel_bias):
    bmap = jnp.asarray(_near_bucket_map())
    return pl.pallas_call(
        _bias_band_kernel,
        grid=(A_HEADS,),
        in_specs=[pl.BlockSpec(memory_space=pltpu.SMEM),
                  pl.BlockSpec((Q_TILE, 2 * Q_TILE), lambda h: (0, 0))],
        out_specs=pl.BlockSpec((1, Q_TILE, 2 * Q_TILE), lambda h: (h, 0, 0)),
        out_shape=jax.ShapeDtypeStruct((A_HEADS, Q_TILE, 2 * Q_TILE), f32),
        name="bias_band",
    )(rel_bias, bmap)


def _split2(s):
    hi = s.astype(bf16)
    lo = (s - hi.astype(f32)).astype(bf16)
    return hi, lo


def _proj_a_kernel(x_ref, g_ref, w_ref, qg_ref, kg_ref, bd_ref,
                   q_out, qi_out, kkt_out, vext_out, widx_out, gate_out):
    h = _rms(x_ref[0], g_ref[...]).astype(bf16)

    def proj(lo, hi):
        return jnp.dot(h, w_ref[:, lo:hi], preferred_element_type=f32)


    zkk = proj(3 * A_WIDTH, 3 * A_WIDTH + LANES)
    lane = lax.broadcasted_iota(jnp.int32, zkk.shape, 1)
    is_k = lane < A_HEAD_DIM
    ssk = jnp.sum(jnp.where(is_k, zkk * zkk, 0.0), axis=-1, keepdims=True)
    kn = zkk * lax.rsqrt(ssk * (1.0 / A_HEAD_DIM) + EPS) * kg_ref[...]
    kk = jnp.where(is_k, kn, zkk)
    kkt_out[0] = kk.T.astype(bf16)

    zvw = proj(3 * A_WIDTH + LANES, 3 * A_WIDTH + 2 * LANES)
    vext = jnp.where(lane < A_HEAD_DIM, zvw, jnp.where(lane == A_HEAD_DIM, 1.0, 0.0))
    vext_out[0] = vext.astype(bf16)
    widx_out[0] = zvw[:, A_HEAD_DIM:A_HEAD_DIM + IDX_HEADS] * (IDX_HEADS ** -0.5 * IDX_DIM ** -0.5)

    zq = proj(0, A_WIDTH)
    bd = bd_ref[...]
    ssq = sum(jnp.dot(t, bd, preferred_element_type=f32) for t in _split2(zq * zq))
    qn = zq * lax.rsqrt(ssq * (1.0 / A_HEAD_DIM) + EPS) * qg_ref[...]
    qn = (qn * (A_HEAD_DIM ** -0.5)).astype(bf16)
    zqi = proj(A_WIDTH, 2 * A_WIDTH).astype(bf16)
    for hd in range(A_HEADS):
        sl = slice(hd * A_HEAD_DIM, (hd + 1) * A_HEAD_DIM)
        q_out[0, hd] = qn[:, sl]
        qi_out[0, hd] = zqi[:, sl]

    gate_out[0] = proj(2 * A_WIDTH, 3 * A_WIDTH)


def _proj_a(x, norm_g, w_a, qg, kg, bd):
    B, S, _ = x.shape
    T = TOKEN_TILE
    n_a = w_a.shape[1]
    const = lambda b, i: (0, 0)
    return pl.pallas_call(
        _proj_a_kernel,
        grid=(B, S // T),
        in_specs=[pl.BlockSpec((1, T, D_MODEL), lambda b, i: (b, i, 0)),
                  pl.BlockSpec((1, D_MODEL), const),
                  pl.BlockSpec((D_MODEL, n_a), const),
                  pl.BlockSpec((1, A_WIDTH), const),
                  pl.BlockSpec((1, LANES), const),
                  pl.BlockSpec((A_WIDTH, A_WIDTH), const)],
        out_specs=[pl.BlockSpec((1, A_HEADS, T, A_HEAD_DIM), lambda b, i: (b, 0, i, 0)),
                   pl.BlockSpec((1, IDX_HEADS, T, IDX_DIM), lambda b, i: (b, 0, i, 0)),
                   pl.BlockSpec((1, LANES, T), lambda b, i: (b, 0, i)),
                   pl.BlockSpec((1, T, LANES), lambda b, i: (b, i, 0)),
                   pl.BlockSpec((1, T, IDX_HEADS), lambda b, i: (b, i, 0)),
                   pl.BlockSpec((1, T, A_WIDTH), lambda b, i: (b, i, 0))],
        out_shape=[jax.ShapeDtypeStruct((B, A_HEADS, S, A_HEAD_DIM), bf16),
                   jax.ShapeDtypeStruct((B, IDX_HEADS, S, IDX_DIM), bf16),
                   jax.ShapeDtypeStruct((B, LANES, S), bf16),
                   jax.ShapeDtypeStruct((B, S, LANES), bf16),
                   jax.ShapeDtypeStruct((B, S, IDX_HEADS), f32),
                   jax.ShapeDtypeStruct((B, S, A_WIDTH), f32)],
        compiler_params=pltpu.CompilerParams(
            dimension_semantics=("arbitrary", "arbitrary"),
            vmem_limit_bytes=VMEM_LIMIT_BYTES),
        name="proj_a",
    )(x, norm_g, w_a, qg, kg, bd)


def _count(pred):
    return jnp.sum(jnp.where(pred, 1.0, 0.0), axis=1, keepdims=True)


def _attn_block(c, q_ref, qi_ref, kkt_ref, vext_ref, widx_ref, gate_ref, band_ref, ltri_ref,
                y_out, sc_ref, cut_ref, yh_ref, wb_ref, lg_ref):
    nk = (c + 1) * Q_TILE
    k_top = float(TOPK_MAX)
    kt = kkt_ref[0, 0:A_HEAD_DIM, 0:nk]
    vext = vext_ref[0, 0:nk, :]
    row = lax.broadcasted_iota(jnp.int32, (Q_TILE, nk), 0) + c * Q_TILE
    col = lax.broadcasted_iota(jnp.int32, (Q_TILE, nk), 1)
    causal = col <= row

    if nk > TOPK_MAX:
        kit = kkt_ref[0, A_HEAD_DIM:2 * A_HEAD_DIM, 0:nk]
        widx = widx_ref[0]

        for hd in range(IDX_HEADS):
            wb_ref[hd] = jnp.broadcast_to(widx[:, hd:hd + 1], (Q_TILE, LANES))
        sc_ref[:, 0:nk] = jnp.zeros((Q_TILE, nk), f32)

        def idx_heads(g, carry):
            h0 = g * IDX_STACK
            s = jnp.dot(qi_ref[0, pl.ds(h0, IDX_STACK)].reshape(IDX_STACK * Q_TILE, IDX_DIM), kit,
                        preferred_element_type=f32)
            acc = sc_ref[:, 0:nk]
            for t in range(IDX_STACK):
                w = jnp.concatenate([wb_ref[h0 + t]] * (nk // LANES), axis=1)
                acc = acc + w * jnp.maximum(s[t * Q_TILE:(t + 1) * Q_TILE], 0.0)
            sc_ref[:, 0:nk] = acc
            return carry

        lax.fori_loop(0, IDX_HEADS // IDX_STACK, idx_heads, 0)
        sc_ref[:, 0:nk] = jnp.where(causal, sc_ref[:, 0:nk], -jnp.inf)

        def scores():
            return sc_ref[:, 0:nk]

        sc = scores()
        lo0 = jnp.min(jnp.where(causal, sc, jnp.inf), axis=1, keepdims=True)
        hi0 = jnp.max(sc, axis=1, keepdims=True)
        clo0 = (lax.broadcasted_iota(jnp.int32, (Q_TILE, 1), 0) + (c * Q_TILE + 1)).astype(f32)

        def halve(_, st):
            lo, hi, clo = st
            mid = lo + (hi - lo) * 0.5
            cnt = _count(scores() >= mid)
            ge = cnt >= k_top
            return jnp.where(ge, mid, lo), jnp.where(ge, hi, mid), jnp.where(ge, cnt, clo)

        def candidate(lo):
            s = scores()
            a = jnp.min(jnp.where(s >= lo, s, jnp.inf), axis=1, keepdims=True)
            return a, _count(s > a)

        lo, _, clo = lax.fori_loop(0, BISECT_ITERS, halve, (lo0, hi0, clo0), unroll=BISECT_UNROLL)
        a, cgt = candidate(lo)

        def unfinished(st):
            return jnp.max(st[2]) >= k_top

        def refine(st):
            clo, a, cgt = st
            s = scores()
            nxt = jnp.min(jnp.where(s > a, s, jnp.inf), axis=1, keepdims=True)
            todo = cgt >= k_top
            lo = jnp.where(todo, nxt, a)
            clo = jnp.where(todo, cgt, clo)
            a, cgt = candidate(lo)
            return clo, a, cgt

        cge, thr, cgt = lax.while_loop(unfinished, refine, (clo, a, cgt))

        cut_ref[...] = jnp.full((Q_TILE, 1), nk, jnp.int32)

        @pl.when(jnp.max(cge) > k_top)
        def _():
            need = k_top - cgt
            eqf = jnp.where(scores() == thr, 1.0, 0.0)
            n_grp = nk // LANES
            before = jnp.zeros((Q_TILE, 1), f32)
            jstar = jnp.zeros((Q_TILE, 1), f32)
            base = jnp.zeros((Q_TILE, 1), f32)
            for j in range(n_grp):
                reached = before < need
                jstar = jnp.where(reached, float(j), jstar)
                base = jnp.where(reached, before, base)
                before = before + jnp.sum(eqf[:, j * LANES:(j + 1) * LANES], axis=1, keepdims=True)
            grp = jnp.zeros((Q_TILE, LANES), f32)
            for j in range(n_grp):
                grp = jnp.where(jstar == float(j), eqf[:, j * LANES:(j + 1) * LANES], grp)
            upto = jnp.dot(grp.astype(bf16), ltri_ref[...],
                           preferred_element_type=f32)
            lstar = _count(upto < need - base)
            cut_ref[...] = (jstar * LANES + lstar).astype(jnp.int32)

        s = scores()
        keep = (s > thr) | ((s == thr) & (col <= cut_ref[...]))
    else:
        keep = causal
    sc_ref[:, 0:nk] = jnp.where(keep, 0.0, -jnp.inf)
    near_w = min(nk, 2 * Q_TILE)

    def masked_logits(hd):
        logit = jnp.dot(q_ref[0, hd], kt, preferred_element_type=f32)
        near = logit[:, nk - near_w:] + (band_ref[hd][:, 2 * Q_TILE - near_w:] + sc_ref[:, nk - near_w:nk])
        if nk > near_w:
            far = logit[:, :nk - near_w] + sc_ref[:, 0:nk - near_w]
            logit = jnp.concatenate([far, near], axis=1)
        else:
            logit = near
        return logit, jnp.max(logit, axis=1, keepdims=True)

    def weighted_values(hd, m):
        p = jnp.exp(lg_ref[:, 0:nk] - m).astype(bf16)
        yh_ref[hd] = jnp.dot(p, vext, preferred_element_type=f32)

    def head(hd, m_prev):
        p = jnp.exp(lg_ref[:, 0:nk] - m_prev).astype(bf16)
        logit, m = masked_logits(hd)
        lg_ref[:, 0:nk] = logit
        yh_ref[hd - 1] = jnp.dot(p, vext, preferred_element_type=f32)
        return m

    logit0, m0 = masked_logits(0)
    lg_ref[:, 0:nk] = logit0
    m_last = lax.fori_loop(1, A_HEADS, head, m0)
    weighted_values(A_HEADS - 1, m_last)
    ys = []
    for hd in range(A_HEADS):
        pv = yh_ref[hd]
        ys.append(pv[:, 0:A_HEAD_DIM] / pv[:, A_HEAD_DIM:A_HEAD_DIM + 1])
    y = jnp.concatenate(ys, axis=1)
    g = gate_ref[0]
    y_out[0] = (y * (g * jax.nn.sigmoid(g))).astype(bf16)


def _attn_kernel(q_ref, qi_ref, kkt_ref, vext_ref, widx_ref, gate_ref, band_ref, ltri_ref,
                 y_out, sc_ref, cut_ref, yh_ref, wb_ref, lg_ref):
    i = pl.program_id(0)
    for c in range(kkt_ref.shape[2] // Q_TILE):
        @pl.when(i == c)
        def _(c=c):
            _attn_block(c, q_ref, qi_ref, kkt_ref, vext_ref, widx_ref, gate_ref, band_ref,
                        ltri_ref, y_out, sc_ref, cut_ref, yh_ref, wb_ref, lg_ref)


def _attn(q, qi, kkt, vext, widx, gate, band):
    B, _, S, _ = q.shape
    ltri = jnp.asarray(np.arange(LANES)[:, None] <= np.arange(LANES)[None, :], dtype=bf16)
    return pl.pallas_call(
        _attn_kernel,
        grid=(S // Q_TILE, B),
        in_specs=[pl.BlockSpec((1, A_HEADS, Q_TILE, A_HEAD_DIM), lambda i, b: (b, 0, i, 0)),
                  pl.BlockSpec((1, IDX_HEADS, Q_TILE, IDX_DIM), lambda i, b: (b, 0, i, 0)),
                  pl.BlockSpec((1, LANES, S), lambda i, b: (b, 0, 0)),
                  pl.BlockSpec((1, S, LANES), lambda i, b: (b, 0, 0)),
                  pl.BlockSpec((1, Q_TILE, IDX_HEADS), lambda i, b: (b, i, 0)),
                  pl.BlockSpec((1, Q_TILE, A_WIDTH), lambda i, b: (b, i, 0)),
                  pl.BlockSpec((A_HEADS, Q_TILE, 2 * Q_TILE), lambda i, b: (0, 0, 0)),
                  pl.BlockSpec((LANES, LANES), lambda i, b: (0, 0))],
        out_specs=pl.BlockSpec((1, Q_TILE, A_WIDTH), lambda i, b: (b, i, 0)),
        out_shape=jax.ShapeDtypeStruct((B, S, A_WIDTH), bf16),
        scratch_shapes=[pltpu.VMEM((Q_TILE, S), f32),
                        pltpu.VMEM((Q_TILE, 1), jnp.int32),
                        pltpu.VMEM((A_HEADS, Q_TILE, LANES), f32),
                        pltpu.VMEM((IDX_HEADS, Q_TILE, LANES), f32),
                        pltpu.VMEM((Q_TILE, S), f32)],
        compiler_params=pltpu.CompilerParams(
            dimension_semantics=("arbitrary", "arbitrary"),
            vmem_limit_bytes=VMEM_LIMIT_BYTES),
        name="attn",
    )(q, qi, kkt, vext, widx, gate, band, ltri)


def _mix_kernel(x_ref, ya_ref, g_ref, w_ref, lng_ref, lnb_ref, ws_ref, bst_ref, wbr_ref, wo_ref,
                out_ref):
    x = x_ref[...]
    h = _rms(x, g_ref[...]).astype(bf16)
    T = x.shape[0]

    def proj(lo, hi):
        return jnp.dot(h, w_ref[:, lo:hi], preferred_element_type=f32)

    u = jax.nn.gelu(proj(0, B_WIDTH))
    vb = jax.nn.gelu(proj(B_WIDTH, 2 * B_WIDTH))
    mu = jnp.mean(vb, axis=-1, keepdims=True)
    var = jnp.mean(jnp.square(vb - mu), axis=-1, keepdims=True)
    vln = ((vb - mu) * lax.rsqrt(var + EPS) * lng_ref[...] + lnb_ref[...]).astype(bf16)

    n_ch = T // CHUNK
    tril = (lax.broadcasted_iota(jnp.int32, (CHUNK, CHUNK), 1)
            <= lax.broadcasted_iota(jnp.int32, (CHUNK, CHUNK), 0))
    bst = bst_ref[...]
    per_group = []
    for gi in range(B_GROUPS):
        wg = jnp.where(tril, ws_ref[gi], 0.0).astype(bf16)
        cols = slice(gi * B_GROUP_DIM, (gi + 1) * B_GROUP_DIM)
        vg = jnp.concatenate([vln[ci * CHUNK:(ci + 1) * CHUNK, cols] for ci in range(n_ch)], axis=1)
        sg = jnp.dot(wg, vg, preferred_element_type=f32) + bst[:, gi:gi + 1]
        per_group.append(sg)
    s = jnp.concatenate(
        [jnp.concatenate([per_group[gi][:, ci * B_GROUP_DIM:(ci + 1) * B_GROUP_DIM]
                          for gi in range(B_GROUPS)], axis=1)
         for ci in range(n_ch)], axis=0)

    gb = proj(2 * B_WIDTH, 3 * B_WIDTH)
    yb = (u * s * (gb * jax.nn.sigmoid(gb))).astype(bf16)
    yd_a = jnp.dot(ya_ref[...], wbr_ref[0], preferred_element_type=f32)
    yd_b = jnp.dot(yb, wbr_ref[1], preferred_element_type=f32)
    ma = proj(3 * B_WIDTH, 3 * B_WIDTH + D_MODEL)
    mb = proj(3 * B_WIDTH + D_MODEL, 3 * B_WIDTH + 2 * D_MODEL)
    merged = jax.nn.sigmoid(ma) * yd_a + jax.nn.sigmoid(mb) * yd_b
    out_ref[...] = x + jnp.dot(merged.astype(bf16), wo_ref[...], preferred_element_type=f32)


def _mix(x2, ya2, norm_g, w_b, ln_g, ln_b, w_sp, b_sp_t, w_br, w_o):
    N = x2.shape[0]
    T = TOKEN_TILE
    n_b = w_b.shape[1]
    c2 = lambda i: (0, 0)
    c3 = lambda i: (0, 0, 0)
    return pl.pallas_call(
        _mix_kernel,
        grid=(N // T,),
        in_specs=[pl.BlockSpec((T, D_MODEL), lambda i: (i, 0)),
                  pl.BlockSpec((T, A_WIDTH), lambda i: (i, 0)),
                  pl.BlockSpec((1, D_MODEL), c2),
                  pl.BlockSpec((D_MODEL, n_b), c2),
                  pl.BlockSpec((1, B_WIDTH), c2),
                  pl.BlockSpec((1, B_WIDTH), c2),
                  pl.BlockSpec((B_GROUPS, CHUNK, CHUNK), c3),
                  pl.BlockSpec((CHUNK, B_GROUPS), c2),
                  pl.BlockSpec((2, A_WIDTH, D_MODEL), c3),
                  pl.BlockSpec((D_MODEL, D_MODEL), c2)],
        out_specs=pl.BlockSpec((T, D_MODEL), lambda i: (i, 0)),
        out_shape=jax.ShapeDtypeStruct((N, D_MODEL), f32),
        compiler_params=pltpu.CompilerParams(
            dimension_semantics=("arbitrary",),
            vmem_limit_bytes=VMEM_LIMIT_BYTES),
        name="mix",
    )(x2, ya2, norm_g, w_b, ln_g, ln_b, w_sp, b_sp_t, w_br, w_o)


def _attn_side_weights(w):
    pad = jnp.zeros((D_MODEL, LANES - A_HEAD_DIM - IDX_HEADS), w.dtype)
    return jnp.concatenate(
        [w[:, _Q0:_K0], w[:, _QI0:_KI0], w[:, _GA0:_QI0],
         w[:, _K0:_V0], w[:, _KI0:_WI0],
         w[:, _V0:_GA0], w[:, _WI0:_U0], pad], axis=1).astype(bf16)


def kernel(x, norm_g, w_in, q_norm_g, k_norm_g, rel_bias, sgu_ln_g, sgu_ln_b,
           w_spatial, b_spatial, w_branch, w_out):
    B, S, D = x.shape
    depth = w_in.shape[0]
    band = _bias_band(rel_bias)
    head_of = np.arange(A_WIDTH) // A_HEAD_DIM
    bd = jnp.asarray(head_of[:, None] == head_of[None, :], dtype=bf16)
    for l in range(depth):
        w_a = _attn_side_weights(w_in[l])
        w_b = w_in[l][:, _U0:_END].astype(bf16)
        g = norm_g[l][None, :]
        qg = jnp.tile(q_norm_g[l], A_HEADS)[None, :]
        kg = jnp.concatenate([k_norm_g[l], jnp.ones((LANES - A_HEAD_DIM,), f32)])[None, :]
        q, qi, kkt, vext, widx, gate = _proj_a(x, g, w_a, qg, kg, bd)
        ya = _attn(q, qi, kkt, vext, widx, gate, band)
        x2 = _mix(x.reshape(B * S, D), ya.reshape(B * S, A_WIDTH), g, w_b,
                  sgu_ln_g[l][None, :], sgu_ln_b[l][None, :], w_spatial[l],
                  b_spatial[l].T, w_branch[l].astype(bf16), w_out[l].astype(bf16))
        x = x2.reshape(B, S, D)
    return x
```

```python
import functools
import math

import numpy as np
import jax
import jax.numpy as jnp
from jax import lax
from jax.experimental import pallas as pl
from jax.experimental.pallas import tpu as pltpu

D_MODEL = 1024
A_HEADS = 8
A_HEAD_DIM = 64
A_WIDTH = A_HEADS * A_HEAD_DIM
IDX_HEADS = 8
IDX_DIM = 64
TOPK_MAX = 256
B_GROUPS = 4
B_GROUP_DIM = 128
B_WIDTH = B_GROUPS * B_GROUP_DIM
CHUNK = 128
REL_BUCKETS = 32
REL_MAX_DIST = 128
EPS = 1e-6

LANES = 128
VMEM_LIMIT_BYTES = 56 * 1024 * 1024

TOKEN_TILE = 512
Q_TILE = 256
IDX_STACK = 4
BISECT_ITERS = 20
BISECT_UNROLL = 4

_SPLITS = (A_WIDTH, A_HEAD_DIM, A_HEAD_DIM, A_WIDTH, IDX_HEADS * IDX_DIM, IDX_DIM,
           IDX_HEADS, B_WIDTH, B_WIDTH, B_WIDTH, D_MODEL, D_MODEL)
_OFFS = np.concatenate([[0], np.cumsum(_SPLITS)])
(_Q0, _K0, _V0, _GA0, _QI0, _KI0, _WI0, _U0, _VB0, _GB0, _MA0, _MB0, _END) = [int(o) for o in _OFFS]

f32 = jnp.float32
bf16 = jnp.bfloat16


def _t5_bucket_np(rel):
    max_exact = REL_BUCKETS // 2
    nf = np.maximum(rel, 1).astype(np.float32)
    large = max_exact + (np.log(nf / np.float32(max_exact))
                         / np.float32(math.log(REL_MAX_DIST / max_exact))
                         * np.float32(REL_BUCKETS - max_exact)).astype(np.int32)
    large = np.minimum(large, REL_BUCKETS - 1)
    return np.where(rel < max_exact, rel, large).astype(np.int32)


def _near_bucket_map():
    tq = np.arange(Q_TILE)[:, None]
    tk = np.arange(2 * Q_TILE)[None, :]
    rel = Q_TILE + tq - tk
    return _t5_bucket_np(np.maximum(rel, 0))


def _rms(x, g):
    ms = jnp.mean(x * x, axis=-1, keepdims=True)
    return x * lax.rsqrt(ms + EPS) * g


def _bias_band_kernel(rb_ref, bmap_ref, out_ref):
    h = pl.program_id(0)
    bmap = bmap_ref[...]
    acc = jnp.zeros(bmap.shape, f32)
    for b in range(REL_BUCKETS):
        acc = jnp.where(bmap == b, rb_ref[b, h], acc)
    out_ref[0] = acc - rb_ref[REL_BUCKETS - 1, h]


def _bias_band(rel_bias):
    bmap = jnp.asarray(_near_bucket_map())
    return pl.pallas_call(
        _bias_band_kernel,
        grid=(A_HEADS,),
        in_specs=[pl.BlockSpec(memory_space=pltpu.SMEM),
                  pl.BlockSpec((Q_TILE, 2 * Q_TILE), lambda h: (0, 0))],
        out_specs=pl.BlockSpec((1, Q_TILE, 2 * Q_TILE), lambda h: (h, 0, 0)),
        out_shape=jax.ShapeDtypeStruct((A_HEADS, Q_TILE, 2 * Q_TILE), f32),
        name="bias_band",
    )(rel_bias, bmap)


def _split2(s):
    hi = s.astype(bf16)
    lo = (s - hi.astype(f32)).astype(bf16)
    return hi, lo


def _proj_a_kernel(x_ref, g_ref, w_ref, qg_ref, kg_ref, bd_ref,
                   q_out, qi_out, kkt_out, vext_out, widx_out, gate_out):
    h = _rms(x_ref[0], g_ref[...]).astype(bf16)

    def proj(lo, hi):
        return jnp.dot(h, w_ref[:, lo:hi], preferred_element_type=f32)


    zkk = proj(3 * A_WIDTH, 3 * A_WIDTH + LANES)
    lane = lax.broadcasted_iota(jnp.int32, zkk.shape, 1)
    is_k = lane < A_HEAD_DIM
    ssk = jnp.sum(jnp.where(is_k, zkk * zkk, 0.0), axis=-1, keepdims=True)
    kn = zkk * lax.rsqrt(ssk * (1.0 / A_HEAD_DIM) + EPS) * kg_ref[...]
    kk = jnp.where(is_k, kn, zkk)
    kkt_out[0] = kk.T.astype(bf16)

    zvw = proj(3 * A_WIDTH + LANES, 3 * A_WIDTH + 2 * LANES)
    vext = jnp.where(lane < A_HEAD_DIM, zvw, jnp.where(lane == A_HEAD_DIM, 1.0, 0.0))
    vext_out[0] = vext.astype(bf16)
    widx_out[0] = zvw[:, A_HEAD_DIM:A_HEAD_DIM + IDX_HEADS] * (IDX_HEADS ** -0.5 * IDX_DIM ** -0.5)

    zq = proj(0, A_WIDTH)
    bd = bd_ref[...]
    ssq = sum(jnp.dot(t, bd, preferred_element_type=f32) for t in _split2(zq * zq))
    qn = zq * lax.rsqrt(ssq * (1.0 / A_HEAD_DIM) + EPS) * qg_ref[...]
    qn = (qn * (A_HEAD_DIM ** -0.5)).astype(bf16)
    zqi = proj(A_WIDTH, 2 * A_WIDTH).astype(bf16)
    for hd in range(A_HEADS):
        sl = slice(hd * A_HEAD_DIM, (hd + 1) * A_HEAD_DIM)
        q_out[0, hd] = qn[:, sl]
        qi_out[0, hd] = zqi[:, sl]

    gate_out[0] = proj(2 * A_WIDTH, 3 * A_WIDTH)


def _proj_a(x, norm_g, w_a, qg, kg, bd):
    B, S, _ = x.shape
    T = TOKEN_TILE
    n_a = w_a.shape[1]
    const = lambda b, i: (0, 0)
    return pl.pallas_call(
        _proj_a_kernel,
        grid=(B, S // T),
        in_specs=[pl.BlockSpec((1, T, D_MODEL), lambda b, i: (b, i, 0)),
                  pl.BlockSpec((1, D_MODEL), const),
                  pl.BlockSpec((D_MODEL, n_a), const),
                  pl.BlockSpec((1, A_WIDTH), const),
                  pl.BlockSpec((1, LANES), const),
                  pl.BlockSpec((A_WIDTH, A_WIDTH), const)],
        out_specs=[pl.BlockSpec((1, A_HEADS, T, A_HEAD_DIM), lambda b, i: (b, 0, i, 0)),
                   pl.BlockSpec((1, IDX_HEADS, T, IDX_DIM), lambda b, i: (b, 0, i, 0)),
                   pl.BlockSpec((1, LANES, T), lambda b, i: (b, 0, i)),
                   pl.BlockSpec((1, T, LANES), lambda b, i: (b, i, 0)),
                   pl.BlockSpec((1, T, IDX_HEADS), lambda b, i: (b, i, 0)),
                   pl.BlockSpec((1, T, A_WIDTH), lambda b, i: (b, i, 0))],
        out_shape=[jax.ShapeDtypeStruct((B, A_HEADS, S, A_HEAD_DIM), bf16),
                   jax.ShapeDtypeStruct((B, IDX_HEADS, S, IDX_DIM), bf16),
                   jax.ShapeDtypeStruct((B, LANES, S), bf16),
                   jax.ShapeDtypeStruct((B, S, LANES), bf16),
                   jax.ShapeDtypeStruct((B, S, IDX_HEADS), f32),
                   jax.ShapeDtypeStruct((B, S, A_WIDTH), f32)],
        compiler_params=pltpu.CompilerParams(
            dimension_semantics=("arbitrary", "arbitrary"),
            vmem_limit_bytes=VMEM_LIMIT_BYTES),
        name="proj_a",
    )(x, norm_g, w_a, qg, kg, bd)


def _count(pred):
    return jnp.sum(jnp.where(pred, 1.0, 0.0), axis=1, keepdims=True)


def _smallest_at_least(s, lo):
    a = jnp.min(jnp.where(s >= lo, s, jnp.inf), axis=1, keepdims=True)
    return a, _count(s > a)


def _tile_search(c, qi_ref, kkt_ref, widx_ref, sc_ref, st_ref, wb_ref):
    nk = (c + 1) * Q_TILE
    S = sc_ref.shape[1]
    k_top = float(TOPK_MAX)
    if nk <= TOPK_MAX:
        st_ref[...] = jnp.zeros(st_ref.shape, f32)
        return
    row = lax.broadcasted_iota(jnp.int32, (Q_TILE, nk), 0) + c * Q_TILE
    col = lax.broadcasted_iota(jnp.int32, (Q_TILE, nk), 1)
    causal = col <= row
    kit = kkt_ref[0, A_HEAD_DIM:2 * A_HEAD_DIM, 0:nk]
    widx = widx_ref[0]
    for hd in range(IDX_HEADS):
        wb_ref[hd] = jnp.broadcast_to(widx[:, hd:hd + 1], (Q_TILE, LANES))
    sc_ref[:, 0:nk] = jnp.zeros((Q_TILE, nk), f32)
    if nk < S:
        sc_ref[:, nk:S] = jnp.full((Q_TILE, S - nk), -jnp.inf, f32)

    def idx_heads(g, carry):
        h0 = g * IDX_STACK
        s = jnp.dot(qi_ref[0, pl.ds(h0, IDX_STACK)].reshape(IDX_STACK * Q_TILE, IDX_DIM), kit,
                    preferred_element_type=f32)
        acc = sc_ref[:, 0:nk]
        for t in range(IDX_STACK):
            w = jnp.concatenate([wb_ref[h0 + t]] * (nk // LANES), axis=1)
            acc = acc + w * jnp.maximum(s[t * Q_TILE:(t + 1) * Q_TILE], 0.0)
        sc_ref[:, 0:nk] = acc
        return carry

    lax.fori_loop(0, IDX_HEADS // IDX_STACK, idx_heads, 0)
    sc_ref[:, 0:nk] = jnp.where(causal, sc_ref[:, 0:nk], -jnp.inf)

    def scores():
        return sc_ref[:, 0:nk]

    sc = scores()
    lo0 = jnp.min(jnp.where(causal, sc, jnp.inf), axis=1, keepdims=True)
    hi0 = jnp.max(sc, axis=1, keepdims=True)
    clo0 = (lax.broadcasted_iota(jnp.int32, (Q_TILE, 1), 0) + (c * Q_TILE + 1)).astype(f32)

    def halve(_, st):
        lo, hi, clo = st
        mid = lo + (hi - lo) * 0.5
        cnt = _count(scores() >= mid)
        ge = cnt >= k_top
        return jnp.where(ge, mid, lo), jnp.where(ge, hi, mid), jnp.where(ge, cnt, clo)

    lo, _, clo = lax.fori_loop(0, BISECT_ITERS, halve, (lo0, hi0, clo0), unroll=BISECT_UNROLL)
    a, cgt = _smallest_at_least(scores(), lo)
    st_ref[0] = a
    st_ref[1] = cgt
    st_ref[2] = clo


def _finish_selection(sc_ref, st_ref, cut_ref, ltri_ref):
    S = sc_ref.shape[1]
    k_top = float(TOPK_MAX)

    def unfinished(st):
        return jnp.max(st[2]) >= k_top

    def refine(st):
        cge, a, cgt = st
        s = sc_ref[...]
        nxt = jnp.min(jnp.where(s > a, s, jnp.inf), axis=1, keepdims=True)
        todo = cgt >= k_top
        cge = jnp.where(todo, cgt, cge)
        a, cgt = _smallest_at_least(s, jnp.where(todo, nxt, a))
        return cge, a, cgt

    cge, thr, cgt = lax.while_loop(unfinished, refine, (st_ref[2], st_ref[0], st_ref[1]))
    st_ref[0] = thr

    cut_ref[...] = jnp.full(cut_ref.shape, S, jnp.int32)

    @pl.when(jnp.max(cge) > k_top)
    def _():
        need = k_top - cgt
        eqf = jnp.where(sc_ref[...] == thr, 1.0, 0.0)
        n_grp = S // LANES
        before = jnp.zeros((Q_TILE, 1), f32)
        jstar = jnp.zeros((Q_TILE, 1), f32)
        base = jnp.zeros((Q_TILE, 1), f32)
        for j in range(n_grp):
            reached = before < need
            jstar = jnp.where(reached, float(j), jstar)
            base = jnp.where(reached, before, base)
            before = before + jnp.sum(eqf[:, j * LANES:(j + 1) * LANES], axis=1, keepdims=True)
        grp = jnp.zeros((Q_TILE, LANES), f32)
        for j in range(n_grp):
            grp = jnp.where(jstar == float(j), eqf[:, j * LANES:(j + 1) * LANES], grp)
        upto = jnp.dot(grp.astype(bf16), ltri_ref[...],
                       preferred_element_type=f32)
        lstar = _count(upto < need - base)
        cut_ref[...] = (jstar * LANES + lstar).astype(jnp.int32)


def _tile_heads(c, q_ref, kkt_ref, vext_ref, gate_ref, band_ref, y_out,
                sc_ref, st_ref, cut_ref, yh_ref, lg_ref):
    nk = (c + 1) * Q_TILE
    kt = kkt_ref[0, 0:A_HEAD_DIM, 0:nk]
    vext = vext_ref[0, 0:nk, :]
    col = lax.broadcasted_iota(jnp.int32, (Q_TILE, nk), 1)
    if nk > TOPK_MAX:
        s = sc_ref[:, 0:nk]
        thr = st_ref[0]
        keep = (s > thr) | ((s == thr) & (col <= cut_ref[...]))
    else:
        keep = col <= lax.broadcasted_iota(jnp.int32, (Q_TILE, nk), 0) + c * Q_TILE
    sc_ref[:, 0:nk] = jnp.where(keep, 0.0, -jnp.inf)
    near_w = min(nk, 2 * Q_TILE)

    def masked_logits(hd):
        logit = jnp.dot(q_ref[0, hd], kt, preferred_element_type=f32)
        near = logit[:, nk - near_w:] + (band_ref[hd][:, 2 * Q_TILE - near_w:] + sc_ref[:, nk - near_w:nk])
        if nk > near_w:
            far = logit[:, :nk - near_w] + sc_ref[:, 0:nk - near_w]
            logit = jnp.concatenate([far, near], axis=1)
        else:
            logit = near
        return logit, jnp.max(logit, axis=1, keepdims=True)

    def weighted_values(hd, m):
        p = jnp.exp(lg_ref[:, 0:nk] - m).astype(bf16)
        yh_ref[hd] = jnp.dot(p, vext, preferred_element_type=f32)

    def head(hd, m_prev):
        p = jnp.exp(lg_ref[:, 0:nk] - m_prev).astype(bf16)
        logit, m = masked_logits(hd)
        lg_ref[:, 0:nk] = logit
        yh_ref[hd - 1] = jnp.dot(p, vext, preferred_element_type=f32)
        return m

    logit0, m0 = masked_logits(0)
    lg_ref[:, 0:nk] = logit0
    m_last = lax.fori_loop(1, A_HEADS, head, m0)
    weighted_values(A_HEADS - 1, m_last)
    ys = []
    for hd in range(A_HEADS):
        pv = yh_ref[hd]
        ys.append(pv[:, 0:A_HEAD_DIM] / pv[:, A_HEAD_DIM:A_HEAD_DIM + 1])
    y = jnp.concatenate(ys, axis=1)
    g = gate_ref[0]
    y_out[0] = (y * (g * jax.nn.sigmoid(g))).astype(bf16)


def _attn_kernel(q_ref, qi_ref, kkt_ref, vext_ref, widx_ref, gate_ref, band_ref, ltri_ref,
                 y_out, sc_ref, st_ref, cut_ref, yh_ref, wb_ref, lg_ref):
    i = pl.program_id(0)
    n_tiles = kkt_ref.shape[2] // Q_TILE
    for c in range(n_tiles):
        @pl.when(i == c)
        def _(c=c):
            _tile_search(c, qi_ref, kkt_ref, widx_ref, sc_ref, st_ref, wb_ref)

    @pl.when(i * Q_TILE >= TOPK_MAX)
    def _():
        _finish_selection(sc_ref, st_ref, cut_ref, ltri_ref)

    for c in range(n_tiles):
        @pl.when(i == c)
        def _(c=c):
            _tile_heads(c, q_ref, kkt_ref, vext_ref, gate_ref, band_ref, y_out,
                        sc_ref, st_ref, cut_ref, yh_ref, lg_ref)


def _attn(q, qi, kkt, vext, widx, gate, band):
    B, _, S, _ = q.shape
    ltri = jnp.asarray(np.arange(LANES)[:, None] <= np.arange(LANES)[None, :], dtype=bf16)
    return pl.pallas_call(
        _attn_kernel,
        grid=(S // Q_TILE, B),
        in_specs=[pl.BlockSpec((1, A_HEADS, Q_TILE, A_HEAD_DIM), lambda i, b: (b, 0, i, 0)),
                  pl.BlockSpec((1, IDX_HEADS, Q_TILE, IDX_DIM), lambda i, b: (b, 0, i, 0)),
                  pl.BlockSpec((1, LANES, S), lambda i, b: (b, 0, 0)),
                  pl.BlockSpec((1, S, LANES), lambda i, b: (b, 0, 0)),
                  pl.BlockSpec((1, Q_TILE, IDX_HEADS), lambda i, b: (b, i, 0)),
                  pl.BlockSpec((1, Q_TILE, A_WIDTH), lambda i, b: (b, i, 0)),
                  pl.BlockSpec((A_HEADS, Q_TILE, 2 * Q_TILE), lambda i, b: (0, 0, 0)),
                  pl.BlockSpec((LANES, LANES), lambda i, b: (0, 0))],
        out_specs=pl.BlockSpec((1, Q_TILE, A_WIDTH), lambda i, b: (b, i, 0)),
        out_shape=jax.ShapeDtypeStruct((B, S, A_WIDTH), bf16),
        scratch_shapes=[pltpu.VMEM((Q_TILE, S), f32),
                        pltpu.VMEM((3, Q_TILE, 1), f32),
                        pltpu.VMEM((Q_TILE, 1), jnp.int32),
                        pltpu.VMEM((A_HEADS, Q_TILE, LANES), f32),
                        pltpu.VMEM((IDX_HEADS, Q_TILE, LANES), f32),
                        pltpu.VMEM((Q_TILE, S), f32)],
        compiler_params=pltpu.CompilerParams(
            dimension_semantics=("arbitrary", "arbitrary"),
            vmem_limit_bytes=VMEM_LIMIT_BYTES),
        name="attn",
    )(q, qi, kkt, vext, widx, gate, band, ltri)


def _mix_kernel(x_ref, ya_ref, g_ref, w_ref, lng_ref, lnb_ref, ws_ref, bst_ref, wbr_ref, wo_ref,
                out_ref):
    x = x_ref[...]
    h = _rms(x, g_ref[...]).astype(bf16)
    T = x.shape[0]

    def proj(lo, hi):
        return jnp.dot(h, w_ref[:, lo:hi], preferred_element_type=f32)

    u = jax.nn.gelu(proj(0, B_WIDTH))
    vb = jax.nn.gelu(proj(B_WIDTH, 2 * B_WIDTH))
    mu = jnp.mean(vb, axis=-1, keepdims=True)
    var = jnp.mean(jnp.square(vb - mu), axis=-1, keepdims=True)
    vln = ((vb - mu) * lax.rsqrt(var + EPS) * lng_ref[...] + lnb_ref[...]).astype(bf16)

    n_ch = T // CHUNK
    tril = (lax.broadcasted_iota(jnp.int32, (CHUNK, CHUNK), 1)
            <= lax.broadcasted_iota(jnp.int32, (CHUNK, CHUNK), 0))
    bst = bst_ref[...]
    per_group = []
    for gi in range(B_GROUPS):
        wg = jnp.where(tril, ws_ref[gi], 0.0).astype(bf16)
        cols = slice(gi * B_GROUP_DIM, (gi + 1) * B_GROUP_DIM)
        vg = jnp.concatenate([vln[ci * CHUNK:(ci + 1) * CHUNK, cols] for ci in range(n_ch)], axis=1)
        sg = jnp.dot(wg, vg, preferred_element_type=f32) + bst[:, gi:gi + 1]
        per_group.append(sg)
    s = jnp.concatenate(
        [jnp.concatenate([per_group[gi][:, ci * B_GROUP_DIM:(ci + 1) * B_GROUP_DIM]
                          for gi in range(B_GROUPS)], axis=1)
         for ci in range(n_ch)], axis=0)

    gb = proj(2 * B_WIDTH, 3 * B_WIDTH)
    yb = (u * s * (gb * jax.nn.sigmoid(gb))).astype(bf16)
    yd_a = jnp.dot(ya_ref[...], wbr_ref[0], preferred_element_type=f32)
    yd_b = jnp.dot(yb, wbr_ref[1], preferred_element_type=f32)
    ma = proj(3 * B_WIDTH, 3 * B_WIDTH + D_MODEL)
    mb = proj(3 * B_WIDTH + D_MODEL, 3 * B_WIDTH + 2 * D_MODEL)
    merged = jax.nn.sigmoid(ma) * yd_a + jax.nn.sigmoid(mb) * yd_b
    out_ref[...] = x + jnp.dot(merged.astype(bf16), wo_ref[...], preferred_element_type=f32)


def _mix(x2, ya2, norm_g, w_b, ln_g, ln_b, w_sp, b_sp_t, w_br, w_o):
    N = x2.shape[0]
    T = TOKEN_TILE
    n_b = w_b.shape[1]
    c2 = lambda i: (0, 0)
    c3 = lambda i: (0, 0, 0)
    return pl.pallas_call(
        _mix_kernel,
        grid=(N // T,),
        in_specs=[pl.BlockSpec((T, D_MODEL), lambda i: (i, 0)),
                  pl.BlockSpec((T, A_WIDTH), lambda i: (i, 0)),
                  pl.BlockSpec((1, D_MODEL), c2),
                  pl.BlockSpec((D_MODEL, n_b), c2),
                  pl.BlockSpec((1, B_WIDTH), c2),
                  pl.BlockSpec((1, B_WIDTH), c2),
                  pl.BlockSpec((B_GROUPS, CHUNK, CHUNK), c3),
                  pl.BlockSpec((CHUNK, B_GROUPS), c2),
                  pl.BlockSpec((2, A_WIDTH, D_MODEL), c3),
                  pl.BlockSpec((D_MODEL, D_MODEL), c2)],
        out_specs=pl.BlockSpec((T, D_MODEL), lambda i: (i, 0)),
        out_shape=jax.ShapeDtypeStruct((N, D_MODEL), f32),
        compiler_params=pltpu.CompilerParams(
            dimension_semantics=("arbitrary",),
            vmem_limit_bytes=VMEM_LIMIT_BYTES),
        name="mix",
    )(x2, ya2, norm_g, w_b, ln_g, ln_b, w_sp, b_sp_t, w_br, w_o)


def _attn_side_weights(w):
    pad = jnp.zeros((D_MODEL, LANES - A_HEAD_DIM - IDX_HEADS), w.dtype)
    return jnp.concatenate(
        [w[:, _Q0:_K0], w[:, _QI0:_KI0], w[:, _GA0:_QI0],
         w[:, _K0:_V0], w[:, _KI0:_WI0],
         w[:, _V0:_GA0], w[:, _WI0:_U0], pad], axis=1).astype(bf16)


def kernel(x, norm_g, w_in, q_norm_g, k_norm_g, rel_bias, sgu_ln_g, sgu_ln_b,
           w_spatial, b_spatial, w_branch, w_out):
    B, S, D = x.shape
    depth = w_in.shape[0]
    band = _bias_band(rel_bias)
    head_of = np.arange(A_WIDTH) // A_HEAD_DIM
    bd = jnp.asarray(head_of[:, None] == head_of[None, :], dtype=bf16)
    for l in range(depth):
        w_a = _attn_side_weights(w_in[l])
        w_b = w_in[l][:, _U0:_END].astype(bf16)
        g = norm_g[l][None, :]
        qg = jnp.tile(q_norm_g[l], A_HEADS)[None, :]
        kg = jnp.concatenate([k_norm_g[l], jnp.ones((LANES - A_HEAD_DIM,), f32)])[None, :]
        q, qi, kkt, vext, widx, gate = _proj_a(x, g, w_a, qg, kg, bd)
        ya = _attn(q, qi, kkt, vext, widx, gate, band)
        x2 = _mix(x.reshape(B * S, D), ya.reshape(B * S, A_WIDTH), g, w_b,
                  sgu_ln_g[l][None, :], sgu_ln_b[l][None, :], w_spatial[l],
                  b_spatial[l].T, w_branch[l].astype(bf16), w_out[l].astype(bf16))
        x = x2.reshape(B, S, D)
    return x
```

```python
import functools
import math

import numpy as np
import jax
import jax.numpy as jnp
from jax import lax
from jax.experimental import pallas as pl
from jax.experimental.pallas import tpu as pltpu

D_MODEL = 1024
A_HEADS = 8
A_HEAD_DIM = 64
A_WIDTH = A_HEADS * A_HEAD_DIM
IDX_HEADS = 8
IDX_DIM = 64
TOPK_MAX = 256
B_GROUPS = 4
B_GROUP_DIM = 128
B_WIDTH = B_GROUPS * B_GROUP_DIM
CHUNK = 128
REL_BUCKETS = 32
REL_MAX_DIST = 128
EPS = 1e-6

LANES = 128
VMEM_LIMIT_BYTES = 56 * 1024 * 1024

TOKEN_TILE = 512
Q_TILE = 256
IDX_STACK = 4
HALVINGS_PER_HEAD = 3

_SPLITS = (A_WIDTH, A_HEAD_DIM, A_HEAD_DIM, A_WIDTH, IDX_HEADS * IDX_DIM, IDX_DIM,
           IDX_HEADS, B_WIDTH, B_WIDTH, B_WIDTH, D_MODEL, D_MODEL)
_OFFS = np.concatenate([[0], np.cumsum(_SPLITS)])
(_Q0, _K0, _V0, _GA0, _QI0, _KI0, _WI0, _U0, _VB0, _GB0, _MA0, _MB0, _END) = [int(o) for o in _OFFS]

f32 = jnp.float32
bf16 = jnp.bfloat16


def _t5_bucket_np(rel):
    max_exact = REL_BUCKETS // 2
    nf = np.maximum(rel, 1).astype(np.float32)
    large = max_exact + (np.log(nf / np.float32(max_exact))
                         / np.float32(math.log(REL_MAX_DIST / max_exact))
                         * np.float32(REL_BUCKETS - max_exact)).astype(np.int32)
    large = np.minimum(large, REL_BUCKETS - 1)
    return np.where(rel < max_exact, rel, large).astype(np.int32)


def _near_bucket_map():
    tq = np.arange(Q_TILE)[:, None]
    tk = np.arange(2 * Q_TILE)[None, :]
    rel = Q_TILE + tq - tk
    return _t5_bucket_np(np.maximum(rel, 0))


def _rms(x, g):
    ms = jnp.mean(x * x, axis=-1, keepdims=True)
    return x * lax.rsqrt(ms + EPS) * g


def _bias_band_kernel(rb_ref, bmap_ref, out_ref):
    h = pl.program_id(0)
    bmap = bmap_ref[...]
    acc = jnp.zeros(bmap.shape, f32)
    for b in range(REL_BUCKETS):
        acc = jnp.where(bmap == b, rb_ref[b, h], acc)
    out_ref[0] = acc - rb_ref[REL_BUCKETS - 1, h]


def _bias_band(rel_bias):
    bmap = jnp.asarray(_near_bucket_map())
    return pl.pallas_call(
        _bias_band_kernel,
        grid=(A_HEADS,),
        in_specs=[pl.BlockSpec(memory_space=pltpu.SMEM),
                  pl.BlockSpec((Q_TILE, 2 * Q_TILE), lambda h: (0, 0))],
        out_specs=pl.BlockSpec((1, Q_TILE, 2 * Q_TILE), lambda h: (h, 0, 0)),
        out_shape=jax.ShapeDtypeStruct((A_HEADS, Q_TILE, 2 * Q_TILE), f32),
        name="bias_band",
    )(rel_bias, bmap)


def _split2(s):
    hi = s.astype(bf16)
    lo = (s - hi.astype(f32)).astype(bf16)
    return hi, lo


def _proj_a_kernel(x_ref, g_ref, w_ref, qg_ref, kg_ref, bd_ref,
                   q_out, qi_out, kkt_out, vext_out, widx_out, gate_out):
    h = _rms(x_ref[0], g_ref[...]).astype(bf16)

    def proj(lo, hi):
        return jnp.dot(h, w_ref[:, lo:hi], preferred_element_type=f32)


    zkk = proj(3 * A_WIDTH, 3 * A_WIDTH + LANES)
    lane = lax.broadcasted_iota(jnp.int32, zkk.shape, 1)
    is_k = lane < A_HEAD_DIM
    ssk = jnp.sum(jnp.where(is_k, zkk * zkk, 0.0), axis=-1, keepdims=True)
    kn = zkk * lax.rsqrt(ssk * (1.0 / A_HEAD_DIM) + EPS) * kg_ref[...]
    kk = jnp.where(is_k, kn, zkk)
    kkt_out[0] = kk.T.astype(bf16)

    zvw = proj(3 * A_WIDTH + LANES, 3 * A_WIDTH + 2 * LANES)
    vext = jnp.where(lane < A_HEAD_DIM, zvw, jnp.where(lane == A_HEAD_DIM, 1.0, 0.0))
    vext_out[0] = vext.astype(bf16)
    widx_out[0] = zvw[:, A_HEAD_DIM:A_HEAD_DIM + IDX_HEADS] * (IDX_HEADS ** -0.5 * IDX_DIM ** -0.5)

    zq = proj(0, A_WIDTH)
    bd = bd_ref[...]
    ssq = sum(jnp.dot(t, bd, preferred_element_type=f32) for t in _split2(zq * zq))
    qn = zq * lax.rsqrt(ssq * (1.0 / A_HEAD_DIM) + EPS) * qg_ref[...]
    qn = (qn * (A_HEAD_DIM ** -0.5)).astype(bf16)
    zqi = proj(A_WIDTH, 2 * A_WIDTH).astype(bf16)
    for hd in range(A_HEADS):
        sl = slice(hd * A_HEAD_DIM, (hd + 1) * A_HEAD_DIM)
        q_out[0, hd] = qn[:, sl]
        qi_out[0, hd] = zqi[:, sl]

    gate_out[0] = proj(2 * A_WIDTH, 3 * A_WIDTH)


def _proj_a(x, norm_g, w_a, qg, kg, bd):
    B, S, _ = x.shape
    T = TOKEN_TILE
    n_a = w_a.shape[1]
    const = lambda b, i: (0, 0)
    return pl.pallas_call(
        _proj_a_kernel,
        grid=(B, S // T),
        in_specs=[pl.BlockSpec((1, T, D_MODEL), lambda b, i: (b, i, 0)),
                  pl.BlockSpec((1, D_MODEL), const),
                  pl.BlockSpec((D_MODEL, n_a), const),
                  pl.BlockSpec((1, A_WIDTH), const),
                  pl.BlockSpec((1, LANES), const),
                  pl.BlockSpec((A_WIDTH, A_WIDTH), const)],
        out_specs=[pl.BlockSpec((1, A_HEADS, T, A_HEAD_DIM), lambda b, i: (b, 0, i, 0)),
                   pl.BlockSpec((1, IDX_HEADS, T, IDX_DIM), lambda b, i: (b, 0, i, 0)),
                   pl.BlockSpec((1, LANES, T), lambda b, i: (b, 0, i)),
                   pl.BlockSpec((1, T, LANES), lambda b, i: (b, i, 0)),
                   pl.BlockSpec((1, T, IDX_HEADS), lambda b, i: (b, i, 0)),
                   pl.BlockSpec((1, T, A_WIDTH), lambda b, i: (b, i, 0))],
        out_shape=[jax.ShapeDtypeStruct((B, A_HEADS, S, A_HEAD_DIM), bf16),
                   jax.ShapeDtypeStruct((B, IDX_HEADS, S, IDX_DIM), bf16),
                   jax.ShapeDtypeStruct((B, LANES, S), bf16),
                   jax.ShapeDtypeStruct((B, S, LANES), bf16),
                   jax.ShapeDtypeStruct((B, S, IDX_HEADS), f32),
                   jax.ShapeDtypeStruct((B, S, A_WIDTH), f32)],
        compiler_params=pltpu.CompilerParams(
            dimension_semantics=("arbitrary", "arbitrary"),
            vmem_limit_bytes=VMEM_LIMIT_BYTES),
        name="proj_a",
    )(x, norm_g, w_a, qg, kg, bd)


def _count(pred):
    return jnp.sum(jnp.where(pred, 1.0, 0.0), axis=1, keepdims=True)


def _smallest_at_least(s, lo):
    a = jnp.min(jnp.where(s >= lo, s, jnp.inf), axis=1, keepdims=True)
    return a, _count(s > a)


def _tile_scores(c, qi_ref, kkt_ref, widx_ref, sc_ref, wb_ref):
    nk = (c + 1) * Q_TILE
    S = sc_ref.shape[1]
    k_top = float(TOPK_MAX)
    row = lax.broadcasted_iota(jnp.int32, (Q_TILE, nk), 0) + c * Q_TILE
    col = lax.broadcasted_iota(jnp.int32, (Q_TILE, nk), 1)
    causal = col <= row
    kit = kkt_ref[0, A_HEAD_DIM:2 * A_HEAD_DIM, 0:nk]
    widx = widx_ref[0]
    for hd in range(IDX_HEADS):
        wb_ref[hd] = jnp.broadcast_to(widx[:, hd:hd + 1], (Q_TILE, LANES))
    sc_ref[:, 0:nk] = jnp.zeros((Q_TILE, nk), f32)
    if nk < S:
        sc_ref[:, nk:S] = jnp.full((Q_TILE, S - nk), -jnp.inf, f32)

    def idx_heads(g, carry):
        h0 = g * IDX_STACK
        s = jnp.dot(qi_ref[0, pl.ds(h0, IDX_STACK)].reshape(IDX_STACK * Q_TILE, IDX_DIM), kit,
                    preferred_element_type=f32)
        acc = sc_ref[:, 0:nk]
        for t in range(IDX_STACK):
            w = jnp.concatenate([wb_ref[h0 + t]] * (nk // LANES), axis=1)
            acc = acc + w * jnp.maximum(s[t * Q_TILE:(t + 1) * Q_TILE], 0.0)
        sc_ref[:, 0:nk] = acc
        return carry

    lax.fori_loop(0, IDX_HEADS // IDX_STACK, idx_heads, 0)
    sc = jnp.where(causal, sc_ref[:, 0:nk], -jnp.inf)
    sc_ref[:, 0:nk] = sc
    lo0 = jnp.min(jnp.where(causal, sc, jnp.inf), axis=1, keepdims=True)
    hi0 = jnp.max(sc, axis=1, keepdims=True)
    clo0 = (lax.broadcasted_iota(jnp.int32, (Q_TILE, 1), 0) + (c * Q_TILE + 1)).astype(f32)

    def halve(st):
        lo, hi, clo = st
        mid = lo + (hi - lo) * 0.5
        cnt = _count(sc_ref[:, 0:nk] >= mid)
        ge = cnt >= k_top
        return jnp.where(ge, mid, lo), jnp.where(ge, hi, mid), jnp.where(ge, cnt, clo)

    return (lo0, hi0, clo0), halve


def _finish_selection(sc_ref, st_ref, cut_ref, ltri_ref):
    S = sc_ref.shape[1]
    k_top = float(TOPK_MAX)

    def unfinished(st):
        return jnp.max(st[2]) >= k_top

    def refine(st):
        cge, a, cgt = st
        s = sc_ref[...]
        nxt = jnp.min(jnp.where(s > a, s, jnp.inf), axis=1, keepdims=True)
        todo = cgt >= k_top
        cge = jnp.where(todo, cgt, cge)
        a, cgt = _smallest_at_least(s, jnp.where(todo, nxt, a))
        return cge, a, cgt

    cge, thr, cgt = lax.while_loop(unfinished, refine, (st_ref[2], st_ref[0], st_ref[1]))
    st_ref[0] = thr

    cut_ref[...] = jnp.full(cut_ref.shape, S, jnp.int32)

    @pl.when(jnp.max(cge) > k_top)
    def _():
        need = k_top - cgt
        eqf = jnp.where(sc_ref[...] == thr, 1.0, 0.0)
        n_grp = S // LANES
        before = jnp.zeros((Q_TILE, 1), f32)
        jstar = jnp.zeros((Q_TILE, 1), f32)
        base = jnp.zeros((Q_TILE, 1), f32)
        for j in range(n_grp):
            reached = before < need
            jstar = jnp.where(reached, float(j), jstar)
            base = jnp.where(reached, before, base)
            before = before + jnp.sum(eqf[:, j * LANES:(j + 1) * LANES], axis=1, keepdims=True)
        grp = jnp.zeros((Q_TILE, LANES), f32)
        for j in range(n_grp):
            grp = jnp.where(jstar == float(j), eqf[:, j * LANES:(j + 1) * LANES], grp)
        upto = jnp.dot(grp.astype(bf16), ltri_ref[...],
                       preferred_element_type=f32)
        lstar = _count(upto < need - base)
        cut_ref[...] = (jstar * LANES + lstar).astype(jnp.int32)


def _tile_heads(c, q_ref, kkt_ref, vext_ref, gate_ref, band_ref, y_out, neg_ref, yh_ref, lg_ref,
                side_work=None, side_state=0):
    nk = (c + 1) * Q_TILE
    kt = kkt_ref[0, 0:A_HEAD_DIM, 0:nk]
    vext = vext_ref[0, 0:nk, :]
    near_w = min(nk, 2 * Q_TILE)

    def masked_logits(hd):
        logit = jnp.dot(q_ref[0, hd], kt, preferred_element_type=f32)
        near = logit[:, nk - near_w:] + (band_ref[hd][:, 2 * Q_TILE - near_w:] + neg_ref[:, nk - near_w:nk])
        if nk > near_w:
            far = logit[:, :nk - near_w] + neg_ref[:, 0:nk - near_w]
            logit = jnp.concatenate([far, near], axis=1)
        else:
            logit = near
        return logit, jnp.max(logit, axis=1, keepdims=True)

    def weighted_values(hd, m):
        p = jnp.exp(lg_ref[:, 0:nk] - m).astype(bf16)
        yh_ref[hd] = jnp.dot(p, vext, preferred_element_type=f32)

    def head(hd, carry):
        m_prev, side = carry
        p = jnp.exp(lg_ref[:, 0:nk] - m_prev).astype(bf16)
        logit, m = masked_logits(hd)
        lg_ref[:, 0:nk] = logit
        yh_ref[hd - 1] = jnp.dot(p, vext, preferred_element_type=f32)
        if side_work is not None:
            side = side_work(side)
        return m, side

    logit0, m0 = masked_logits(0)
    lg_ref[:, 0:nk] = logit0
    m_last, side_state = lax.fori_loop(1, A_HEADS, head, (m0, side_state))
    weighted_values(A_HEADS - 1, m_last)
    ys = []
    for hd in range(A_HEADS):
        pv = yh_ref[hd]
        ys.append(pv[:, 0:A_HEAD_DIM] / pv[:, A_HEAD_DIM:A_HEAD_DIM + 1])
    y = jnp.concatenate(ys, axis=1)
    g = gate_ref[0]
    y_out[0] = (y * (g * jax.nn.sigmoid(g))).astype(bf16)
    return side_state


def _tile_step(c, n_tiles, q_ref, qin_ref, kkt_ref, vext_ref, widxn_ref, gate_ref, band_ref, y_out,
               sc_ref, neg_ref, st_ref, yh_ref, wb_ref, lg_ref):
    if c == 0:
        row = lax.broadcasted_iota(jnp.int32, (Q_TILE, Q_TILE), 0)
        col = lax.broadcasted_iota(jnp.int32, (Q_TILE, Q_TILE), 1)
        neg_ref[:, 0:Q_TILE] = jnp.where(col <= row, 0.0, -jnp.inf)
    heads = functools.partial(_tile_heads, c, q_ref, kkt_ref, vext_ref, gate_ref, band_ref, y_out,
                              neg_ref, yh_ref, lg_ref)
    if c + 1 == n_tiles:
        heads()
        return
    start, halve = _tile_scores(c + 1, qin_ref, kkt_ref, widxn_ref, sc_ref, wb_ref)

    def halvings(st):
        for _ in range(HALVINGS_PER_HEAD):
            st = halve(st)
        return st

    lo, _, clo = heads(side_work=halvings, side_state=start)
    a, cgt = _smallest_at_least(sc_ref[:, 0:(c + 2) * Q_TILE], lo)
    st_ref[0] = a
    st_ref[1] = cgt
    st_ref[2] = clo


def _tile_mask(c, sc_ref, neg_ref, st_ref, cut_ref):
    nk = (c + 1) * Q_TILE
    col = lax.broadcasted_iota(jnp.int32, (Q_TILE, nk), 1)
    s = sc_ref[:, 0:nk]
    thr = st_ref[0]
    keep = (s > thr) | ((s == thr) & (col <= cut_ref[...]))
    neg_ref[:, 0:nk] = jnp.where(keep, 0.0, -jnp.inf)


def _attn_kernel(q_ref, qin_ref, kkt_ref, vext_ref, widxn_ref, gate_ref, band_ref, ltri_ref,
                 y_out, sc_ref, neg_ref, st_ref, cut_ref, yh_ref, wb_ref, lg_ref):
    i = pl.program_id(1)
    n_tiles = kkt_ref.shape[2] // Q_TILE
    for c in range(n_tiles):
        @pl.when(i == c)
        def _(c=c):
            _tile_step(c, n_tiles, q_ref, qin_ref, kkt_ref, vext_ref, widxn_ref, gate_ref, band_ref,
                       y_out, sc_ref, neg_ref, st_ref, yh_ref, wb_ref, lg_ref)

    @pl.when(i + 1 < n_tiles)
    def _():
        _finish_selection(sc_ref, st_ref, cut_ref, ltri_ref)

    for c in range(1, n_tiles):
        @pl.when(i + 1 == c)
        def _(c=c):
            _tile_mask(c, sc_ref, neg_ref, st_ref, cut_ref)


def _attn(q, qi, kkt, vext, widx, gate, band):
    B, _, S, _ = q.shape
    ltri = jnp.asarray(np.arange(LANES)[:, None] <= np.arange(LANES)[None, :], dtype=bf16)
    n_tiles = S // Q_TILE
    nxt = lambda i: jnp.minimum(i + 1, n_tiles - 1)
    return pl.pallas_call(
        _attn_kernel,
        grid=(B, n_tiles),
        in_specs=[pl.BlockSpec((1, A_HEADS, Q_TILE, A_HEAD_DIM), lambda b, i: (b, 0, i, 0)),
                  pl.BlockSpec((1, IDX_HEADS, Q_TILE, IDX_DIM), lambda b, i: (b, 0, nxt(i), 0)),
                  pl.BlockSpec((1, LANES, S), lambda b, i: (b, 0, 0)),
                  pl.BlockSpec((1, S, LANES), lambda b, i: (b, 0, 0)),
                  pl.BlockSpec((1, Q_TILE, IDX_HEADS), lambda b, i: (b, nxt(i), 0)),
                  pl.BlockSpec((1, Q_TILE, A_WIDTH), lambda b, i: (b, i, 0)),
                  pl.BlockSpec((A_HEADS, Q_TILE, 2 * Q_TILE), lambda b, i: (0, 0, 0)),
                  pl.BlockSpec((LANES, LANES), lambda b, i: (0, 0))],
        out_specs=pl.BlockSpec((1, Q_TILE, A_WIDTH), lambda b, i: (b, i, 0)),
        out_shape=jax.ShapeDtypeStruct((B, S, A_WIDTH), bf16),
        scratch_shapes=[pltpu.VMEM((Q_TILE, S), f32),
                        pltpu.VMEM((Q_TILE, S), f32),
                        pltpu.VMEM((3, Q_TILE, 1), f32),
                        pltpu.VMEM((Q_TILE, 1), jnp.int32),
                        pltpu.VMEM((A_HEADS, Q_TILE, LANES), f32),
                        pltpu.VMEM((IDX_HEADS, Q_TILE, LANES), f32),
                        pltpu.VMEM((Q_TILE, S), f32)],
        compiler_params=pltpu.CompilerParams(
            dimension_semantics=("arbitrary", "arbitrary"),
            vmem_limit_bytes=VMEM_LIMIT_BYTES),
        name="attn",
    )(q, qi, kkt, vext, widx, gate, band, ltri)


def _mix_kernel(x_ref, ya_ref, g_ref, w_ref, lng_ref, lnb_ref, ws_ref, bst_ref, wbr_ref, wo_ref,
                out_ref):
    x = x_ref[...]
    h = _rms(x, g_ref[...]).astype(bf16)
    T = x.shape[0]

    def proj(lo, hi):
        return jnp.dot(h, w_ref[:, lo:hi], preferred_element_type=f32)

    u = jax.nn.gelu(proj(0, B_WIDTH))
    vb = jax.nn.gelu(proj(B_WIDTH, 2 * B_WIDTH))
    mu = jnp.mean(vb, axis=-1, keepdims=True)
    var = jnp.mean(jnp.square(vb - mu), axis=-1, keepdims=True)
    vln = ((vb - mu) * lax.rsqrt(var + EPS) * lng_ref[...] + lnb_ref[...]).astype(bf16)

    n_ch = T // CHUNK
    tril = (lax.broadcasted_iota(jnp.int32, (CHUNK, CHUNK), 1)
            <= lax.broadcasted_iota(jnp.int32, (CHUNK, CHUNK), 0))
    bst = bst_ref[...]
    per_group = []
    for gi in range(B_GROUPS):
        wg = jnp.where(tril, ws_ref[gi], 0.0).astype(bf16)
        cols = slice(gi * B_GROUP_DIM, (gi + 1) * B_GROUP_DIM)
        vg = jnp.concatenate([vln[ci * CHUNK:(ci + 1) * CHUNK, cols] for ci in range(n_ch)], axis=1)
        sg = jnp.dot(wg, vg, preferred_element_type=f32) + bst[:, gi:gi + 1]
        per_group.append(sg)
    s = jnp.concatenate(
        [jnp.concatenate([per_group[gi][:, ci * B_GROUP_DIM:(ci + 1) * B_GROUP_DIM]
                          for gi in range(B_GROUPS)], axis=1)
         for ci in range(n_ch)], axis=0)

    gb = proj(2 * B_WIDTH, 3 * B_WIDTH)
    yb = (u * s * (gb * jax.nn.sigmoid(gb))).astype(bf16)
    yd_a = jnp.dot(ya_ref[...], wbr_ref[0], preferred_element_type=f32)
    yd_b = jnp.dot(yb, wbr_ref[1], preferred_element_type=f32)
    ma = proj(3 * B_WIDTH, 3 * B_WIDTH + D_MODEL)
    mb = proj(3 * B_WIDTH + D_MODEL, 3 * B_WIDTH + 2 * D_MODEL)
    merged = jax.nn.sigmoid(ma) * yd_a + jax.nn.sigmoid(mb) * yd_b
    out_ref[...] = x + jnp.dot(merged.astype(bf16), wo_ref[...], preferred_element_type=f32)


def _mix(x2, ya2, norm_g, w_b, ln_g, ln_b, w_sp, b_sp_t, w_br, w_o):
    N = x2.shape[0]
    T = TOKEN_TILE
    n_b = w_b.shape[1]
    c2 = lambda i: (0, 0)
    c3 = lambda i: (0, 0, 0)
    return pl.pallas_call(
        _mix_kernel,
        grid=(N // T,),
        in_specs=[pl.BlockSpec((T, D_MODEL), lambda i: (i, 0)),
                  pl.BlockSpec((T, A_WIDTH), lambda i: (i, 0)),
                  pl.BlockSpec((1, D_MODEL), c2),
                  pl.BlockSpec((D_MODEL, n_b), c2),
                  pl.BlockSpec((1, B_WIDTH), c2),
                  pl.BlockSpec((1, B_WIDTH), c2),
                  pl.BlockSpec((B_GROUPS, CHUNK, CHUNK), c3),
                  pl.BlockSpec((CHUNK, B_GROUPS), c2),
                  pl.BlockSpec((2, A_WIDTH, D_MODEL), c3),
                  pl.BlockSpec((D_MODEL, D_MODEL), c2)],
        out_specs=pl.BlockSpec((T, D_MODEL), lambda i: (i, 0)),
        out_shape=jax.ShapeDtypeStruct((N, D_MODEL), f32),
        compiler_params=pltpu.CompilerParams(
            dimension_semantics=("arbitrary",),
            vmem_limit_bytes=VMEM_LIMIT_BYTES),
        name="mix",
    )(x2, ya2, norm_g, w_b, ln_g, ln_b, w_sp, b_sp_t, w_br, w_o)


def _attn_side_weights(w):
    pad = jnp.zeros((D_MODEL, LANES - A_HEAD_DIM - IDX_HEADS), w.dtype)
    return jnp.concatenate(
        [w[:, _Q0:_K0], w[:, _QI0:_KI0], w[:, _GA0:_QI0],
         w[:, _K0:_V0], w[:, _KI0:_WI0],
         w[:, _V0:_GA0], w[:, _WI0:_U0], pad], axis=1).astype(bf16)


def kernel(x, norm_g, w_in, q_norm_g, k_norm_g, rel_bias, sgu_ln_g, sgu_ln_b,
           w_spatial, b_spatial, w_branch, w_out):
    B, S, D = x.shape
    depth = w_in.shape[0]
    band = _bias_band(rel_bias)
    head_of = np.arange(A_WIDTH) // A_HEAD_DIM
    bd = jnp.asarray(head_of[:, None] == head_of[None, :], dtype=bf16)
    for l in range(depth):
        w_a = _attn_side_weights(w_in[l])
        w_b = w_in[l][:, _U0:_END].astype(bf16)
        g = norm_g[l][None, :]
        qg = jnp.tile(q_norm_g[l], A_HEADS)[None, :]
        kg = jnp.concatenate([k_norm_g[l], jnp.ones((LANES - A_HEAD_DIM,), f32)])[None, :]
        q, qi, kkt, vext, widx, gate = _proj_a(x, g, w_a, qg, kg, bd)
        ya = _attn(q, qi, kkt, vext, widx, gate, band)
        x2 = _mix(x.reshape(B * S, D), ya.reshape(B * S, A_WIDTH), g, w_b,
                  sgu_ln_g[l][None, :], sgu_ln_b[l][None, :], w_spatial[l],
                  b_spatial[l].T, w_branch[l].astype(bf16), w_out[l].astype(bf16))
        x = x2.reshape(B, S, D)
    return x
```

```python
import functools
import math

import numpy as np
import jax
import jax.numpy as jnp
from jax import lax
from jax.experimental import pallas as pl
from jax.experimental.pallas import tpu as pltpu

D_MODEL = 1024
A_HEADS = 8
A_HEAD_DIM = 64
A_WIDTH = A_HEADS * A_HEAD_DIM
IDX_HEADS = 8
IDX_DIM = 64
TOPK_MAX = 256
B_GROUPS = 4
B_GROUP_DIM = 128
B_WIDTH = B_GROUPS * B_GROUP_DIM
CHUNK = 128
REL_BUCKETS = 32
REL_MAX_DIST = 128
EPS = 1e-6

LANES = 128
VMEM_LIMIT_BYTES = 56 * 1024 * 1024

TOKEN_TILE = 512
Q_TILE = 256
IDX_STACK = 8
HALVINGS_PER_HEAD = 3

_SPLITS = (A_WIDTH, A_HEAD_DIM, A_HEAD_DIM, A_WIDTH, IDX_HEADS * IDX_DIM, IDX_DIM,
           IDX_HEADS, B_WIDTH, B_WIDTH, B_WIDTH, D_MODEL, D_MODEL)
_OFFS = np.concatenate([[0], np.cumsum(_SPLITS)])
(_Q0, _K0, _V0, _GA0, _QI0, _KI0, _WI0, _U0, _VB0, _GB0, _MA0, _MB0, _END) = [int(o) for o in _OFFS]

f32 = jnp.float32
bf16 = jnp.bfloat16


def _t5_bucket_np(rel):
    max_exact = REL_BUCKETS // 2
    nf = np.maximum(rel, 1).astype(np.float32)
    large = max_exact + (np.log(nf / np.float32(max_exact))
                         / np.float32(math.log(REL_MAX_DIST / max_exact))
                         * np.float32(REL_BUCKETS - max_exact)).astype(np.int32)
    large = np.minimum(large, REL_BUCKETS - 1)
    return np.where(rel < max_exact, rel, large).astype(np.int32)


def _near_bucket_map():
    tq = np.arange(Q_TILE)[:, None]
    tk = np.arange(2 * Q_TILE)[None, :]
    rel = Q_TILE + tq - tk
    return _t5_bucket_np(np.maximum(rel, 0))


def _rms(x, g):
    ms = jnp.mean(x * x, axis=-1, keepdims=True)
    return x * lax.rsqrt(ms + EPS) * g


def _bias_band_kernel(rb_ref, bmap_ref, out_ref):
    h = pl.program_id(0)
    bmap = bmap_ref[...]
    acc = jnp.zeros(bmap.shape, f32)
    for b in range(REL_BUCKETS):
        acc = jnp.where(bmap == b, rb_ref[b, h], acc)
    out_ref[0] = acc - rb_ref[REL_BUCKETS - 1, h]


def _bias_band(rel_bias):
    bmap = jnp.asarray(_near_bucket_map())
    return pl.pallas_call(
        _bias_band_kernel,
        grid=(A_HEADS,),
        in_specs=[pl.BlockSpec(memory_space=pltpu.SMEM),
                  pl.BlockSpec((Q_TILE, 2 * Q_TILE), lambda h: (0, 0))],
        out_specs=pl.BlockSpec((1, Q_TILE, 2 * Q_TILE), lambda h: (h, 0, 0)),
        out_shape=jax.ShapeDtypeStruct((A_HEADS, Q_TILE, 2 * Q_TILE), f32),
        name="bias_band",
    )(rel_bias, bmap)


def _split2(s):
    hi = s.astype(bf16)
    lo = (s - hi.astype(f32)).astype(bf16)
    return hi, lo


def _proj_a_kernel(x_ref, g_ref, w_ref, qg_ref, kg_ref, bd_ref,
                   q_out, qi_out, kkt_out, vext_out, widx_out, gate_out):
    h = _rms(x_ref[0], g_ref[...]).astype(bf16)

    def proj(lo, hi):
        return jnp.dot(h, w_ref[:, lo:hi], preferred_element_type=f32)


    zkk = proj(3 * A_WIDTH, 3 * A_WIDTH + LANES)
    lane = lax.broadcasted_iota(jnp.int32, zkk.shape, 1)
    is_k = lane < A_HEAD_DIM
    ssk = jnp.sum(jnp.where(is_k, zkk * zkk, 0.0), axis=-1, keepdims=True)
    kn = zkk * lax.rsqrt(ssk * (1.0 / A_HEAD_DIM) + EPS) * kg_ref[...]
    kk = jnp.where(is_k, kn, zkk)
    kkt_out[0] = kk.T.astype(bf16)

    zvw = proj(3 * A_WIDTH + LANES, 3 * A_WIDTH + 2 * LANES)
    vext = jnp.where(lane < A_HEAD_DIM, zvw, jnp.where(lane == A_HEAD_DIM, 1.0, 0.0))
    vext_out[0] = vext.astype(bf16)
    widx_out[0] = zvw[:, A_HEAD_DIM:A_HEAD_DIM + IDX_HEADS] * (IDX_HEADS ** -0.5 * IDX_DIM ** -0.5)

    zq = proj(0, A_WIDTH)
    bd = bd_ref[...]
    ssq = sum(jnp.dot(t, bd, preferred_element_type=f32) for t in _split2(zq * zq))
    qn = zq * lax.rsqrt(ssq * (1.0 / A_HEAD_DIM) + EPS) * qg_ref[...]
    qn = (qn * (A_HEAD_DIM ** -0.5)).astype(bf16)
    zqi = proj(A_WIDTH, 2 * A_WIDTH).astype(bf16)
    for hd in range(A_HEADS):
        sl = slice(hd * A_HEAD_DIM, (hd + 1) * A_HEAD_DIM)
        q_out[0, hd] = qn[:, sl]
        qi_out[0, hd] = zqi[:, sl]

    gate_out[0] = proj(2 * A_WIDTH, 3 * A_WIDTH)


def _proj_a(x, norm_g, w_a, qg, kg, bd):
    B, S, _ = x.shape
    T = TOKEN_TILE
    n_a = w_a.shape[1]
    const = lambda b, i: (0, 0)
    return pl.pallas_call(
        _proj_a_kernel,
        grid=(B, S // T),
        in_specs=[pl.BlockSpec((1, T, D_MODEL), lambda b, i: (b, i, 0)),
                  pl.BlockSpec((1, D_MODEL), const),
                  pl.BlockSpec((D_MODEL, n_a), const),
                  pl.BlockSpec((1, A_WIDTH), const),
                  pl.BlockSpec((1, LANES), const),
                  pl.BlockSpec((A_WIDTH, A_WIDTH), const)],
        out_specs=[pl.BlockSpec((1, A_HEADS, T, A_HEAD_DIM), lambda b, i: (b, 0, i, 0)),
                   pl.BlockSpec((1, IDX_HEADS, T, IDX_DIM), lambda b, i: (b, 0, i, 0)),
                   pl.BlockSpec((1, LANES, T), lambda b, i: (b, 0, i)),
                   pl.BlockSpec((1, T, LANES), lambda b, i: (b, i, 0)),
                   pl.BlockSpec((1, T, IDX_HEADS), lambda b, i: (b, i, 0)),
                   pl.BlockSpec((1, T, A_WIDTH), lambda b, i: (b, i, 0))],
        out_shape=[jax.ShapeDtypeStruct((B, A_HEADS, S, A_HEAD_DIM), bf16),
                   jax.ShapeDtypeStruct((B, IDX_HEADS, S, IDX_DIM), bf16),
                   jax.ShapeDtypeStruct((B, LANES, S), bf16),
                   jax.ShapeDtypeStruct((B, S, LANES), bf16),
                   jax.ShapeDtypeStruct((B, S, IDX_HEADS), f32),
                   jax.ShapeDtypeStruct((B, S, A_WIDTH), f32)],
        compiler_params=pltpu.CompilerParams(
            dimension_semantics=("arbitrary", "arbitrary"),
            vmem_limit_bytes=VMEM_LIMIT_BYTES),
        name="proj_a",
    )(x, norm_g, w_a, qg, kg, bd)


def _count(pred):
    return jnp.sum(jnp.where(pred, 1.0, 0.0), axis=1, keepdims=True)


def _smallest_at_least(s, lo):
    a = jnp.min(jnp.where(s >= lo, s, jnp.inf), axis=1, keepdims=True)
    return a, _count(s > a)


def _tile_scores(c, qi_ref, kkt_ref, widx_ref, sc_ref, wb_ref):
    nk = (c + 1) * Q_TILE
    S = sc_ref.shape[1]
    k_top = float(TOPK_MAX)
    row = lax.broadcasted_iota(jnp.int32, (Q_TILE, nk), 0) + c * Q_TILE
    col = lax.broadcasted_iota(jnp.int32, (Q_TILE, nk), 1)
    causal = col <= row
    kit = kkt_ref[0, A_HEAD_DIM:2 * A_HEAD_DIM, 0:nk]
    widx = widx_ref[0]
    for hd in range(IDX_HEADS):
        wb_ref[hd] = jnp.broadcast_to(widx[:, hd:hd + 1], (Q_TILE, LANES))
    sc_ref[:, 0:nk] = jnp.zeros((Q_TILE, nk), f32)
    if nk < S:
        sc_ref[:, nk:S] = jnp.full((Q_TILE, S - nk), -jnp.inf, f32)

    def idx_heads(g, carry):
        h0 = g * IDX_STACK
        s = jnp.dot(qi_ref[0, pl.ds(h0, IDX_STACK)].reshape(IDX_STACK * Q_TILE, IDX_DIM), kit,
                    preferred_element_type=f32)
        acc = sc_ref[:, 0:nk]
        for t in range(IDX_STACK):
            w = jnp.concatenate([wb_ref[h0 + t]] * (nk // LANES), axis=1)
            acc = acc + w * jnp.maximum(s[t * Q_TILE:(t + 1) * Q_TILE], 0.0)
        sc_ref[:, 0:nk] = acc
        return carry

    lax.fori_loop(0, IDX_HEADS // IDX_STACK, idx_heads, 0)
    sc = jnp.where(causal, sc_ref[:, 0:nk], -jnp.inf)
    sc_ref[:, 0:nk] = sc
    lo0 = jnp.min(jnp.where(causal, sc, jnp.inf), axis=1, keepdims=True)
    hi0 = jnp.max(sc, axis=1, keepdims=True)
    clo0 = (lax.broadcasted_iota(jnp.int32, (Q_TILE, 1), 0) + (c * Q_TILE + 1)).astype(f32)

    def halve(st):
        lo, hi, clo = st
        mid = lo + (hi - lo) * 0.5
        cnt = _count(sc_ref[:, 0:nk] >= mid)
        ge = cnt >= k_top
        return jnp.where(ge, mid, lo), jnp.where(ge, hi, mid), jnp.where(ge, cnt, clo)

    return (lo0, hi0, clo0), halve


def _finish_selection(sc_ref, st_ref, cut_ref, ltri_ref):
    S = sc_ref.shape[1]
    k_top = float(TOPK_MAX)

    def unfinished(st):
        return jnp.max(st[2]) >= k_top

    def refine(st):
        cge, a, cgt = st
        s = sc_ref[...]
        nxt = jnp.min(jnp.where(s > a, s, jnp.inf), axis=1, keepdims=True)
        todo = cgt >= k_top
        cge = jnp.where(todo, cgt, cge)
        a, cgt = _smallest_at_least(s, jnp.where(todo, nxt, a))
        return cge, a, cgt

    cge, thr, cgt = lax.while_loop(unfinished, refine, (st_ref[2], st_ref[0], st_ref[1]))
    st_ref[0] = thr

    cut_ref[...] = jnp.full(cut_ref.shape, S, jnp.int32)

    @pl.when(jnp.max(cge) > k_top)
    def _():
        need = k_top - cgt
        eqf = jnp.where(sc_ref[...] == thr, 1.0, 0.0)
        n_grp = S // LANES
        before = jnp.zeros((Q_TILE, 1), f32)
        jstar = jnp.zeros((Q_TILE, 1), f32)
        base = jnp.zeros((Q_TILE, 1), f32)
        for j in range(n_grp):
            reached = before < need
            jstar = jnp.where(reached, float(j), jstar)
            base = jnp.where(reached, before, base)
            before = before + jnp.sum(eqf[:, j * LANES:(j + 1) * LANES], axis=1, keepdims=True)
        grp = jnp.zeros((Q_TILE, LANES), f32)
        for j in range(n_grp):
            grp = jnp.where(jstar == float(j), eqf[:, j * LANES:(j + 1) * LANES], grp)
        upto = jnp.dot(grp.astype(bf16), ltri_ref[...],
                       preferred_element_type=f32)
        lstar = _count(upto < need - base)
        cut_ref[...] = (jstar * LANES + lstar).astype(jnp.int32)


def _tile_heads(c, q_ref, kkt_ref, vext_ref, gate_ref, band_ref, y_out, neg_ref, yh_ref, lg_ref,
                side_work=None, side_state=0):
    nk = (c + 1) * Q_TILE
    kt = kkt_ref[0, 0:A_HEAD_DIM, 0:nk]
    vext = vext_ref[0, 0:nk, :]
    near_w = min(nk, 2 * Q_TILE)

    def masked_logits(hd):
        logit = jnp.dot(q_ref[0, hd], kt, preferred_element_type=f32)
        near = logit[:, nk - near_w:] + (band_ref[hd][:, 2 * Q_TILE - near_w:] + neg_ref[:, nk - near_w:nk])
        if nk > near_w:
            far = logit[:, :nk - near_w] + neg_ref[:, 0:nk - near_w]
            logit = jnp.concatenate([far, near], axis=1)
        else:
            logit = near
        return logit, jnp.max(logit, axis=1, keepdims=True)

    def weighted_values(hd, m):
        p = jnp.exp(lg_ref[:, 0:nk] - m).astype(bf16)
        yh_ref[hd] = jnp.dot(p, vext, preferred_element_type=f32)

    def head(hd, carry):
        m_prev, side = carry
        p = jnp.exp(lg_ref[:, 0:nk] - m_prev).astype(bf16)
        logit, m = masked_logits(hd)
        lg_ref[:, 0:nk] = logit
        yh_ref[hd - 1] = jnp.dot(p, vext, preferred_element_type=f32)
        if side_work is not None:
            side = side_work(side)
        return m, side

    logit0, m0 = masked_logits(0)
    lg_ref[:, 0:nk] = logit0
    m_last, side_state = lax.fori_loop(1, A_HEADS, head, (m0, side_state))
    weighted_values(A_HEADS - 1, m_last)
    ys = []
    for hd in range(A_HEADS):
        pv = yh_ref[hd]
        ys.append(pv[:, 0:A_HEAD_DIM] / pv[:, A_HEAD_DIM:A_HEAD_DIM + 1])
    y = jnp.concatenate(ys, axis=1)
    g = gate_ref[0]
    y_out[0] = (y * (g * jax.nn.sigmoid(g))).astype(bf16)
    return side_state


def _tile_step(c, n_tiles, q_ref, qin_ref, kkt_ref, vext_ref, widxn_ref, gate_ref, band_ref, y_out,
               sc_ref, neg_ref, st_ref, yh_ref, wb_ref, lg_ref):
    if c == 0:
        row = lax.broadcasted_iota(jnp.int32, (Q_TILE, Q_TILE), 0)
        col = lax.broadcasted_iota(jnp.int32, (Q_TILE, Q_TILE), 1)
        neg_ref[:, 0:Q_TILE] = jnp.where(col <= row, 0.0, -jnp.inf)
    heads = functools.partial(_tile_heads, c, q_ref, kkt_ref, vext_ref, gate_ref, band_ref, y_out,
                              neg_ref, yh_ref, lg_ref)
    if c + 1 == n_tiles:
        heads()
        return
    start, halve = _tile_scores(c + 1, qin_ref, kkt_ref, widxn_ref, sc_ref, wb_ref)

    def halvings(st):
        for _ in range(HALVINGS_PER_HEAD):
            st = halve(st)
        return st

    lo, _, clo = heads(side_work=halvings, side_state=start)
    a, cgt = _smallest_at_least(sc_ref[:, 0:(c + 2) * Q_TILE], lo)
    st_ref[0] = a
    st_ref[1] = cgt
    st_ref[2] = clo


def _tile_mask(c, sc_ref, neg_ref, st_ref, cut_ref):
    nk = (c + 1) * Q_TILE
    col = lax.broadcasted_iota(jnp.int32, (Q_TILE, nk), 1)
    s = sc_ref[:, 0:nk]
    thr = st_ref[0]
    keep = (s > thr) | ((s == thr) & (col <= cut_ref[...]))
    neg_ref[:, 0:nk] = jnp.where(keep, 0.0, -jnp.inf)


def _attn_kernel(q_ref, qin_ref, kkt_ref, vext_ref, widxn_ref, gate_ref, band_ref, ltri_ref,
                 y_out, sc_ref, neg_ref, st_ref, cut_ref, yh_ref, wb_ref, lg_ref):
    i = pl.program_id(1)
    n_tiles = kkt_ref.shape[2] // Q_TILE
    for c in range(n_tiles):
        @pl.when(i == c)
        def _(c=c):
            _tile_step(c, n_tiles, q_ref, qin_ref, kkt_ref, vext_ref, widxn_ref, gate_ref, band_ref,
                       y_out, sc_ref, neg_ref, st_ref, yh_ref, wb_ref, lg_ref)

    @pl.when(i + 1 < n_tiles)
    def _():
        _finish_selection(sc_ref, st_ref, cut_ref, ltri_ref)

    for c in range(1, n_tiles):
        @pl.when(i + 1 == c)
        def _(c=c):
            _tile_mask(c, sc_ref, neg_ref, st_ref, cut_ref)


def _attn(q, qi, kkt, vext, widx, gate, band):
    B, _, S, _ = q.shape
    ltri = jnp.asarray(np.arange(LANES)[:, None] <= np.arange(LANES)[None, :], dtype=bf16)
    n_tiles = S // Q_TILE
    nxt = lambda i: jnp.minimum(i + 1, n_tiles - 1)
    return pl.pallas_call(
        _attn_kernel,
        grid=(B, n_tiles),
        in_specs=[pl.BlockSpec((1, A_HEADS, Q_TILE, A_HEAD_DIM), lambda b, i: (b, 0, i, 0)),
                  pl.BlockSpec((1, IDX_HEADS, Q_TILE, IDX_DIM), lambda b, i: (b, 0, nxt(i), 0)),
                  pl.BlockSpec((1, LANES, S), lambda b, i: (b, 0, 0)),
                  pl.BlockSpec((1, S, LANES), lambda b, i: (b, 0, 0)),
                  pl.BlockSpec((1, Q_TILE, IDX_HEADS), lambda b, i: (b, nxt(i), 0)),
                  pl.BlockSpec((1, Q_TILE, A_WIDTH), lambda b, i: (b, i, 0)),
                  pl.BlockSpec((A_HEADS, Q_TILE, 2 * Q_TILE), lambda b, i: (0, 0, 0)),
                  pl.BlockSpec((LANES, LANES), lambda b, i: (0, 0))],
        out_specs=pl.BlockSpec((1, Q_TILE, A_WIDTH), lambda b, i: (b, i, 0)),
        out_shape=jax.ShapeDtypeStruct((B, S, A_WIDTH), bf16),
        scratch_shapes=[pltpu.VMEM((Q_TILE, S), f32),
                        pltpu.VMEM((Q_TILE, S), f32),
                        pltpu.VMEM((3, Q_TILE, 1), f32),
                        pltpu.VMEM((Q_TILE, 1), jnp.int32),
                        pltpu.VMEM((A_HEADS, Q_TILE, LANES), f32),
                        pltpu.VMEM((IDX_HEADS, Q_TILE, LANES), f32),
                        pltpu.VMEM((Q_TILE, S), f32)],
        compiler_params=pltpu.CompilerParams(
            dimension_semantics=("arbitrary", "arbitrary"),
            vmem_limit_bytes=VMEM_LIMIT_BYTES),
        name="attn",
    )(q, qi, kkt, vext, widx, gate, band, ltri)


def _mix_kernel(x_ref, ya_ref, g_ref, w_ref, lng_ref, lnb_ref, ws_ref, bst_ref, wbr_ref, wo_ref,
                out_ref):
    x = x_ref[...]
    h = _rms(x, g_ref[...]).astype(bf16)
    T = x.shape[0]

    def proj(lo, hi):
        return jnp.dot(h, w_ref[:, lo:hi], preferred_element_type=f32)

    u = jax.nn.gelu(proj(0, B_WIDTH))
    vb = jax.nn.gelu(proj(B_WIDTH, 2 * B_WIDTH))
    mu = jnp.mean(vb, axis=-1, keepdims=True)
    var = jnp.mean(jnp.square(vb - mu), axis=-1, keepdims=True)
    vln = ((vb - mu) * lax.rsqrt(var + EPS) * lng_ref[...] + lnb_ref[...]).astype(bf16)

    n_ch = T // CHUNK
    tril = (lax.broadcasted_iota(jnp.int32, (CHUNK, CHUNK), 1)
            <= lax.broadcasted_iota(jnp.int32, (CHUNK, CHUNK), 0))
    bst = bst_ref[...]
    per_group = []
    for gi in range(B_GROUPS):
        wg = jnp.where(tril, ws_ref[gi], 0.0).astype(bf16)
        cols = slice(gi * B_GROUP_DIM, (gi + 1) * B_GROUP_DIM)
        vg = jnp.concatenate([vln[ci * CHUNK:(ci + 1) * CHUNK, cols] for ci in range(n_ch)], axis=1)
        sg = jnp.dot(wg, vg, preferred_element_type=f32) + bst[:, gi:gi + 1]
        per_group.append(sg)
    s = jnp.concatenate(
        [jnp.concatenate([per_group[gi][:, ci * B_GROUP_DIM:(ci + 1) * B_GROUP_DIM]
                          for gi in range(B_GROUPS)], axis=1)
         for ci in range(n_ch)], axis=0)

    gb = proj(2 * B_WIDTH, 3 * B_WIDTH)
    yb = (u * s * (gb * jax.nn.sigmoid(gb))).astype(bf16)
    yd_a = jnp.dot(ya_ref[...], wbr_ref[0], preferred_element_type=f32)
    yd_b = jnp.dot(yb, wbr_ref[1], preferred_element_type=f32)
    ma = proj(3 * B_WIDTH, 3 * B_WIDTH + D_MODEL)
    mb = proj(3 * B_WIDTH + D_MODEL, 3 * B_WIDTH + 2 * D_MODEL)
    merged = jax.nn.sigmoid(ma) * yd_a + jax.nn.sigmoid(mb) * yd_b
    out_ref[...] = x + jnp.dot(merged.astype(bf16), wo_ref[...], preferred_element_type=f32)


def _mix(x2, ya2, norm_g, w_b, ln_g, ln_b, w_sp, b_sp_t, w_br, w_o):
    N = x2.shape[0]
    T = TOKEN_TILE
    n_b = w_b.shape[1]
    c2 = lambda i: (0, 0)
    c3 = lambda i: (0, 0, 0)
    return pl.pallas_call(
        _mix_kernel,
        grid=(N // T,),
        in_specs=[pl.BlockSpec((T, D_MODEL), lambda i: (i, 0)),
                  pl.BlockSpec((T, A_WIDTH), lambda i: (i, 0)),
                  pl.BlockSpec((1, D_MODEL), c2),
                  pl.BlockSpec((D_MODEL, n_b), c2),
                  pl.BlockSpec((1, B_WIDTH), c2),
                  pl.BlockSpec((1, B_WIDTH), c2),
                  pl.BlockSpec((B_GROUPS, CHUNK, CHUNK), c3),
                  pl.BlockSpec((CHUNK, B_GROUPS), c2),
                  pl.BlockSpec((2, A_WIDTH, D_MODEL), c3),
                  pl.BlockSpec((D_MODEL, D_MODEL), c2)],
        out_specs=pl.BlockSpec((T, D_MODEL), lambda i: (i, 0)),
        out_shape=jax.ShapeDtypeStruct((N, D_MODEL), f32),
        compiler_params=pltpu.CompilerParams(
            dimension_semantics=("arbitrary",),
            vmem_limit_bytes=VMEM_LIMIT_BYTES),
        name="mix",
    )(x2, ya2, norm_g, w_b, ln_g, ln_b, w_sp, b_sp_t, w_br, w_o)


def _attn_side_weights(w):
    pad = jnp.zeros((D_MODEL, LANES - A_HEAD_DIM - IDX_HEADS), w.dtype)
    return jnp.concatenate(
        [w[:, _Q0:_K0], w[:, _QI0:_KI0], w[:, _GA0:_QI0],
         w[:, _K0:_V0], w[:, _KI0:_WI0],
         w[:, _V0:_GA0], w[:, _WI0:_U0], pad], axis=1).astype(bf16)


def kernel(x, norm_g, w_in, q_norm_g, k_norm_g, rel_bias, sgu_ln_g, sgu_ln_b,
           w_spatial, b_spatial, w_branch, w_out):
    B, S, D = x.shape
    depth = w_in.shape[0]
    band = _bias_band(rel_bias)
    head_of = np.arange(A_WIDTH) // A_HEAD_DIM
    bd = jnp.asarray(head_of[:, None] == head_of[None, :], dtype=bf16)
    for l in range(depth):
        w_a = _attn_side_weights(w_in[l])
        w_b = w_in[l][:, _U0:_END].astype(bf16)
        g = norm_g[l][None, :]
        qg = jnp.tile(q_norm_g[l], A_HEADS)[None, :]
        kg = jnp.concatenate([k_norm_g[l], jnp.ones((LANES - A_HEAD_DIM,), f32)])[None, :]
        q, qi, kkt, vext, widx, gate = _proj_a(x, g, w_a, qg, kg, bd)
        ya = _attn(q, qi, kkt, vext, widx, gate, band)
        x2 = _mix(x.reshape(B * S, D), ya.reshape(B * S, A_WIDTH), g, w_b,
                  sgu_ln_g[l][None, :], sgu_ln_b[l][None, :], w_spatial[l],
                  b_spatial[l].T, w_branch[l].astype(bf16), w_out[l].astype(bf16))
        x = x2.reshape(B, S, D)
    return x
```

```python
import functools
import math

import numpy as np
import jax
import jax.numpy as jnp
from jax import lax
from jax.experimental import pallas as pl
from jax.experimental.pallas import tpu as pltpu

D_MODEL = 1024
A_HEADS = 8
A_HEAD_DIM = 64
A_WIDTH = A_HEADS * A_HEAD_DIM
IDX_HEADS = 8
IDX_DIM = 64
TOPK_MAX = 256
B_GROUPS = 4
B_GROUP_DIM = 128
B_WIDTH = B_GROUPS * B_GROUP_DIM
CHUNK = 128
REL_BUCKETS = 32
REL_MAX_DIST = 128
EPS = 1e-6

LANES = 128
VMEM_LIMIT_BYTES = 56 * 1024 * 1024

TOKEN_TILE = 512
Q_TILE = 256
IDX_STACK = 8
HALVINGS_PER_HEAD = 3

_SPLITS = (A_WIDTH, A_HEAD_DIM, A_HEAD_DIM, A_WIDTH, IDX_HEADS * IDX_DIM, IDX_DIM,
           IDX_HEADS, B_WIDTH, B_WIDTH, B_WIDTH, D_MODEL, D_MODEL)
_OFFS = np.concatenate([[0], np.cumsum(_SPLITS)])
(_Q0, _K0, _V0, _GA0, _QI0, _KI0, _WI0, _U0, _VB0, _GB0, _MA0, _MB0, _END) = [int(o) for o in _OFFS]

f32 = jnp.float32
bf16 = jnp.bfloat16


def _t5_bucket_np(rel):
    max_exact = REL_BUCKETS // 2
    nf = np.maximum(rel, 1).astype(np.float32)
    large = max_exact + (np.log(nf / np.float32(max_exact))
                         / np.float32(math.log(REL_MAX_DIST / max_exact))
                         * np.float32(REL_BUCKETS - max_exact)).astype(np.int32)
    large = np.minimum(large, REL_BUCKETS - 1)
    return np.where(rel < max_exact, rel, large).astype(np.int32)


def _near_bucket_map():
    tq = np.arange(Q_TILE)[:, None]
    tk = np.arange(2 * Q_TILE)[None, :]
    rel = Q_TILE + tq - tk
    return _t5_bucket_np(np.maximum(rel, 0))


def _rms(x, g):
    ms = jnp.mean(x * x, axis=-1, keepdims=True)
    return x * lax.rsqrt(ms + EPS) * g


def _bias_band_kernel(rb_ref, bmap_ref, out_ref):
    h = pl.program_id(0)
    bmap = bmap_ref[...]
    acc = jnp.zeros(bmap.shape, f32)
    for b in range(REL_BUCKETS):
        acc = jnp.where(bmap == b, rb_ref[b, h], acc)
    out_ref[0] = acc - rb_ref[REL_BUCKETS - 1, h]


def _bias_band(rel_bias):
    bmap = jnp.asarray(_near_bucket_map())
    return pl.pallas_call(
        _bias_band_kernel,
        grid=(A_HEADS,),
        in_specs=[pl.BlockSpec(memory_space=pltpu.SMEM),
                  pl.BlockSpec((Q_TILE, 2 * Q_TILE), lambda h: (0, 0))],
        out_specs=pl.BlockSpec((1, Q_TILE, 2 * Q_TILE), lambda h: (h, 0, 0)),
        out_shape=jax.ShapeDtypeStruct((A_HEADS, Q_TILE, 2 * Q_TILE), f32),
        name="bias_band",
    )(rel_bias, bmap)


def _split2(s):
    hi = s.astype(bf16)
    lo = (s - hi.astype(f32)).astype(bf16)
    return hi, lo


def _proj_a_kernel(x_ref, g_ref, w_ref, qg_ref, kg_ref, bd_ref,
                   q_out, qi_out, kkt_out, vext_out, widx_out, gate_out):
    h = _rms(x_ref[0], g_ref[...]).astype(bf16)

    def proj(lo, hi):
        return jnp.dot(h, w_ref[:, lo:hi], preferred_element_type=f32)


    zkk = proj(3 * A_WIDTH, 3 * A_WIDTH + LANES)
    lane = lax.broadcasted_iota(jnp.int32, zkk.shape, 1)
    is_k = lane < A_HEAD_DIM
    ssk = jnp.sum(jnp.where(is_k, zkk * zkk, 0.0), axis=-1, keepdims=True)
    kn = zkk * lax.rsqrt(ssk * (1.0 / A_HEAD_DIM) + EPS) * kg_ref[...]
    kk = jnp.where(is_k, kn, zkk)
    kkt_out[0] = kk.T.astype(bf16)

    zvw = proj(3 * A_WIDTH + LANES, 3 * A_WIDTH + 2 * LANES)
    vext = jnp.where(lane < A_HEAD_DIM, zvw, jnp.where(lane == A_HEAD_DIM, 1.0, 0.0))
    vext_out[0] = vext.astype(bf16)
    widx_out[0] = zvw[:, A_HEAD_DIM:A_HEAD_DIM + IDX_HEADS] * (IDX_HEADS ** -0.5 * IDX_DIM ** -0.5)

    zq = proj(0, A_WIDTH)
    bd = bd_ref[...]
    ssq = sum(jnp.dot(t, bd, preferred_element_type=f32) for t in _split2(zq * zq))
    qn = zq * lax.rsqrt(ssq * (1.0 / A_HEAD_DIM) + EPS) * qg_ref[...]
    qn = (qn * (A_HEAD_DIM ** -0.5)).astype(bf16)
    zqi = proj(A_WIDTH, 2 * A_WIDTH).astype(bf16)
    for hd in range(A_HEADS):
        sl = slice(hd * A_HEAD_DIM, (hd + 1) * A_HEAD_DIM)
        q_out[0, hd] = qn[:, sl]
        qi_out[0, hd] = zqi[:, sl]

    gate_out[0] = proj(2 * A_WIDTH, 3 * A_WIDTH)


def _proj_a(x, norm_g, w_a, qg, kg, bd):
    B, S, _ = x.shape
    T = TOKEN_TILE
    n_a = w_a.shape[1]
    const = lambda b, i: (0, 0)
    return pl.pallas_call(
        _proj_a_kernel,
        grid=(B, S // T),
        in_specs=[pl.BlockSpec((1, T, D_MODEL), lambda b, i: (b, i, 0)),
                  pl.BlockSpec((1, D_MODEL), const),
                  pl.BlockSpec((D_MODEL, n_a), const),
                  pl.BlockSpec((1, A_WIDTH), const),
                  pl.BlockSpec((1, LANES), const),
                  pl.BlockSpec((A_WIDTH, A_WIDTH), const)],
        out_specs=[pl.BlockSpec((1, A_HEADS, T, A_HEAD_DIM), lambda b, i: (b, 0, i, 0)),
                   pl.BlockSpec((1, IDX_HEADS, T, IDX_DIM), lambda b, i: (b, 0, i, 0)),
                   pl.BlockSpec((1, LANES, T), lambda b, i: (b, 0, i)),
                   pl.BlockSpec((1, T, LANES), lambda b, i: (b, i, 0)),
                   pl.BlockSpec((1, T, IDX_HEADS), lambda b, i: (b, i, 0)),
                   pl.BlockSpec((1, T, A_WIDTH), lambda b, i: (b, i, 0))],
        out_shape=[jax.ShapeDtypeStruct((B, A_HEADS, S, A_HEAD_DIM), bf16),
                   jax.ShapeDtypeStruct((B, IDX_HEADS, S, IDX_DIM), bf16),
                   jax.ShapeDtypeStruct((B, LANES, S), bf16),
                   jax.ShapeDtypeStruct((B, S, LANES), bf16),
                   jax.ShapeDtypeStruct((B, S, IDX_HEADS), f32),
                   jax.ShapeDtypeStruct((B, S, A_WIDTH), f32)],
        compiler_params=pltpu.CompilerParams(
            dimension_semantics=("arbitrary", "arbitrary"),
            vmem_limit_bytes=VMEM_LIMIT_BYTES),
        name="proj_a",
    )(x, norm_g, w_a, qg, kg, bd)


def _count(pred):
    return jnp.sum(jnp.where(pred, 1.0, 0.0), axis=1, keepdims=True)


def _smallest_at_least(s, lo):
    a = jnp.min(jnp.where(s >= lo, s, jnp.inf), axis=1, keepdims=True)
    return a, _count(s > a)


def _tile_scores(c, qi_ref, kkt_ref, widx_ref, sc_ref, wb_ref):
    nk = (c + 1) * Q_TILE
    S = sc_ref.shape[1]
    k_top = float(TOPK_MAX)
    row = lax.broadcasted_iota(jnp.int32, (Q_TILE, nk), 0) + c * Q_TILE
    col = lax.broadcasted_iota(jnp.int32, (Q_TILE, nk), 1)
    causal = col <= row
    kit = kkt_ref[0, A_HEAD_DIM:2 * A_HEAD_DIM, 0:nk]
    widx = widx_ref[0]
    for hd in range(IDX_HEADS):
        wb_ref[hd] = jnp.broadcast_to(widx[:, hd:hd + 1], (Q_TILE, LANES))
    sc_ref[:, 0:nk] = jnp.zeros((Q_TILE, nk), f32)
    if nk < S:
        sc_ref[:, nk:S] = jnp.full((Q_TILE, S - nk), -jnp.inf, f32)

    def idx_heads(g, carry):
        h0 = g * IDX_STACK
        s = jnp.dot(qi_ref[0, pl.ds(h0, IDX_STACK)].reshape(IDX_STACK * Q_TILE, IDX_DIM), kit,
                    preferred_element_type=f32)
        acc = sc_ref[:, 0:nk]
        for t in range(IDX_STACK):
            w = jnp.concatenate([wb_ref[h0 + t]] * (nk // LANES), axis=1)
            acc = acc + w * jnp.maximum(s[t * Q_TILE:(t + 1) * Q_TILE], 0.0)
        sc_ref[:, 0:nk] = acc
        return carry

    lax.fori_loop(0, IDX_HEADS // IDX_STACK, idx_heads, 0)
    sc = jnp.where(causal, sc_ref[:, 0:nk], -jnp.inf)
    sc_ref[:, 0:nk] = sc
    lo0 = jnp.min(jnp.where(causal, sc, jnp.inf), axis=1, keepdims=True)
    hi0 = jnp.max(sc, axis=1, keepdims=True)
    clo0 = (lax.broadcasted_iota(jnp.int32, (Q_TILE, 1), 0) + (c * Q_TILE + 1)).astype(f32)

    def halve(st):
        lo, hi, clo = st
        mid = lo + (hi - lo) * 0.5
        cnt = _count(sc_ref[:, 0:nk] >= mid)
        ge = cnt >= k_top
        return jnp.where(ge, mid, lo), jnp.where(ge, hi, mid), jnp.where(ge, cnt, clo)

    return (lo0, hi0, clo0), halve


def _finish_selection(sc_ref, st_ref, cut_ref, upre_ref, ltri_ref):
    S = sc_ref.shape[1]
    k_top = float(TOPK_MAX)

    def unfinished(st):
        return jnp.max(st[2]) >= k_top

    def refine(st):
        cge, a, cgt = st
        s = sc_ref[...]
        nxt = jnp.min(jnp.where(s > a, s, jnp.inf), axis=1, keepdims=True)
        todo = cgt >= k_top
        cge = jnp.where(todo, cgt, cge)
        a, cgt = _smallest_at_least(s, jnp.where(todo, nxt, a))
        return cge, a, cgt

    cge, thr, cgt = lax.while_loop(unfinished, refine, (st_ref[2], st_ref[0], st_ref[1]))
    st_ref[0] = thr

    cut_ref[...] = jnp.full(cut_ref.shape, S, jnp.int32)

    @pl.when(jnp.max(cge) > k_top)
    def _():
        need = k_top - cgt
        eqf = jnp.where(sc_ref[...] == thr, 1.0, 0.0)
        n_grp = S // LANES
        before = jnp.dot(eqf.astype(bf16), upre_ref[...],
                         preferred_element_type=f32)
        lane = lax.broadcasted_iota(jnp.int32, (Q_TILE, LANES), 1)
        jstar = _count((lane < n_grp) & (before < need)) - 1.0
        base = jnp.sum(jnp.where(lane.astype(f32) == jstar, before, 0.0), axis=1, keepdims=True)
        grp = jnp.zeros((Q_TILE, LANES), f32)
        for j in range(n_grp):
            grp = jnp.where(jstar == float(j), eqf[:, j * LANES:(j + 1) * LANES], grp)
        upto = jnp.dot(grp.astype(bf16), ltri_ref[...],
                       preferred_element_type=f32)
        lstar = _count(upto < need - base)
        cut_ref[...] = (jstar * LANES + lstar).astype(jnp.int32)


def _tile_heads(c, q_ref, kkt_ref, vext_ref, gate_ref, band_ref, y_out, neg_ref, yh_ref, lg_ref,
                side_work=None, side_state=0):
    nk = (c + 1) * Q_TILE
    kt = kkt_ref[0, 0:A_HEAD_DIM, 0:nk]
    vext = vext_ref[0, 0:nk, :]
    near_w = min(nk, 2 * Q_TILE)

    def masked_logits(hd):
        logit = jnp.dot(q_ref[0, hd], kt, preferred_element_type=f32)
        near = logit[:, nk - near_w:] + (band_ref[hd][:, 2 * Q_TILE - near_w:] + neg_ref[:, nk - near_w:nk])
        if nk > near_w:
            far = logit[:, :nk - near_w] + neg_ref[:, 0:nk - near_w]
            logit = jnp.concatenate([far, near], axis=1)
        else:
            logit = near
        return logit, jnp.max(logit, axis=1, keepdims=True)

    def weighted_values(hd, m):
        p = jnp.exp(lg_ref[:, 0:nk] - m).astype(bf16)
        yh_ref[hd] = jnp.dot(p, vext, preferred_element_type=f32)

    def head(hd, carry):
        m_prev, side = carry
        p = jnp.exp(lg_ref[:, 0:nk] - m_prev).astype(bf16)
        logit, m = masked_logits(hd)
        lg_ref[:, 0:nk] = logit
        yh_ref[hd - 1] = jnp.dot(p, vext, preferred_element_type=f32)
        if side_work is not None:
            side = side_work(side)
        return m, side

    logit0, m0 = masked_logits(0)
    lg_ref[:, 0:nk] = logit0
    m_last, side_state = lax.fori_loop(1, A_HEADS, head, (m0, side_state))
    weighted_values(A_HEADS - 1, m_last)
    ys = []
    for hd in range(A_HEADS):
        pv = yh_ref[hd]
        ys.append(pv[:, 0:A_HEAD_DIM] / pv[:, A_HEAD_DIM:A_HEAD_DIM + 1])
    y = jnp.concatenate(ys, axis=1)
    g = gate_ref[0]
    y_out[0] = (y * (g * jax.nn.sigmoid(g))).astype(bf16)
    return side_state


def _tile_step(c, n_tiles, q_ref, qin_ref, kkt_ref, vext_ref, widxn_ref, gate_ref, band_ref, y_out,
               sc_ref, neg_ref, st_ref, yh_ref, wb_ref, lg_ref):
    if c == 0:
        row = lax.broadcasted_iota(jnp.int32, (Q_TILE, Q_TILE), 0)
        col = lax.broadcasted_iota(jnp.int32, (Q_TILE, Q_TILE), 1)
        neg_ref[:, 0:Q_TILE] = jnp.where(col <= row, 0.0, -jnp.inf)
    heads = functools.partial(_tile_heads, c, q_ref, kkt_ref, vext_ref, gate_ref, band_ref, y_out,
                              neg_ref, yh_ref, lg_ref)
    if c + 1 == n_tiles:
        heads()
        return
    start, halve = _tile_scores(c + 1, qin_ref, kkt_ref, widxn_ref, sc_ref, wb_ref)

    def halvings(st):
        for _ in range(HALVINGS_PER_HEAD):
            st = halve(st)
        return st

    lo, _, clo = heads(side_work=halvings, side_state=start)
    a, cgt = _smallest_at_least(sc_ref[:, 0:(c + 2) * Q_TILE], lo)
    st_ref[0] = a
    st_ref[1] = cgt
    st_ref[2] = clo


def _tile_mask(c, sc_ref, neg_ref, st_ref, cut_ref):
    nk = (c + 1) * Q_TILE
    col = lax.broadcasted_iota(jnp.int32, (Q_TILE, nk), 1)
    s = sc_ref[:, 0:nk]
    thr = st_ref[0]
    keep = (s > thr) | ((s == thr) & (col <= cut_ref[...]))
    neg_ref[:, 0:nk] = jnp.where(keep, 0.0, -jnp.inf)


def _attn_kernel(q_ref, qin_ref, kkt_ref, vext_ref, widxn_ref, gate_ref, band_ref, upre_ref, ltri_ref,
                 y_out, sc_ref, neg_ref, st_ref, cut_ref, yh_ref, wb_ref, lg_ref):
    i = pl.program_id(1)
    n_tiles = kkt_ref.shape[2] // Q_TILE
    for c in range(n_tiles):
        @pl.when(i == c)
        def _(c=c):
            _tile_step(c, n_tiles, q_ref, qin_ref, kkt_ref, vext_ref, widxn_ref, gate_ref, band_ref,
                       y_out, sc_ref, neg_ref, st_ref, yh_ref, wb_ref, lg_ref)

    @pl.when(i + 1 < n_tiles)
    def _():
        _finish_selection(sc_ref, st_ref, cut_ref, upre_ref, ltri_ref)

    for c in range(1, n_tiles):
        @pl.when(i + 1 == c)
        def _(c=c):
            _tile_mask(c, sc_ref, neg_ref, st_ref, cut_ref)


def _attn(q, qi, kkt, vext, widx, gate, band):
    B, _, S, _ = q.shape
    key_group = np.arange(S) // LANES
    upre = jnp.asarray(key_group[:, None] < np.arange(LANES)[None, :], dtype=bf16)
    ltri = jnp.asarray(np.arange(LANES)[:, None] <= np.arange(LANES)[None, :], dtype=bf16)
    n_tiles = S // Q_TILE
    nxt = lambda i: jnp.minimum(i + 1, n_tiles - 1)
    return pl.pallas_call(
        _attn_kernel,
        grid=(B, n_tiles),
        in_specs=[pl.BlockSpec((1, A_HEADS, Q_TILE, A_HEAD_DIM), lambda b, i: (b, 0, i, 0)),
                  pl.BlockSpec((1, IDX_HEADS, Q_TILE, IDX_DIM), lambda b, i: (b, 0, nxt(i), 0)),
                  pl.BlockSpec((1, LANES, S), lambda b, i: (b, 0, 0)),
                  pl.BlockSpec((1, S, LANES), lambda b, i: (b, 0, 0)),
                  pl.BlockSpec((1, Q_TILE, IDX_HEADS), lambda b, i: (b, nxt(i), 0)),
                  pl.BlockSpec((1, Q_TILE, A_WIDTH), lambda b, i: (b, i, 0)),
                  pl.BlockSpec((A_HEADS, Q_TILE, 2 * Q_TILE), lambda b, i: (0, 0, 0)),
                  pl.BlockSpec((S, LANES), lambda b, i: (0, 0)),
                  pl.BlockSpec((LANES, LANES), lambda b, i: (0, 0))],
        out_specs=pl.BlockSpec((1, Q_TILE, A_WIDTH), lambda b, i: (b, i, 0)),
        out_shape=jax.ShapeDtypeStruct((B, S, A_WIDTH), bf16),
        scratch_shapes=[pltpu.VMEM((Q_TILE, S), f32),
                        pltpu.VMEM((Q_TILE, S), f32),
                        pltpu.VMEM((3, Q_TILE, 1), f32),
                        pltpu.VMEM((Q_TILE, 1), jnp.int32),
                        pltpu.VMEM((A_HEADS, Q_TILE, LANES), f32),
                        pltpu.VMEM((IDX_HEADS, Q_TILE, LANES), f32),
                        pltpu.VMEM((Q_TILE, S), f32)],
        compiler_params=pltpu.CompilerParams(
            dimension_semantics=("arbitrary", "arbitrary"),
            vmem_limit_bytes=VMEM_LIMIT_BYTES),
        name="attn",
    )(q, qi, kkt, vext, widx, gate, band, upre, ltri)


def _mix_kernel(x_ref, ya_ref, g_ref, w_ref, lng_ref, lnb_ref, ws_ref, bst_ref, wbr_ref, wo_ref,
                out_ref):
    x = x_ref[...]
    h = _rms(x, g_ref[...]).astype(bf16)
    T = x.shape[0]

    def proj(lo, hi):
        return jnp.dot(h, w_ref[:, lo:hi], preferred_element_type=f32)

    u = jax.nn.gelu(proj(0, B_WIDTH))
    vb = jax.nn.gelu(proj(B_WIDTH, 2 * B_WIDTH))
    mu = jnp.mean(vb, axis=-1, keepdims=True)
    var = jnp.mean(jnp.square(vb - mu), axis=-1, keepdims=True)
    vln = ((vb - mu) * lax.rsqrt(var + EPS) * lng_ref[...] + lnb_ref[...]).astype(bf16)

    n_ch = T // CHUNK
    tril = (lax.broadcasted_iota(jnp.int32, (CHUNK, CHUNK), 1)
            <= lax.broadcasted_iota(jnp.int32, (CHUNK, CHUNK), 0))
    bst = bst_ref[...]
    per_group = []
    for gi in range(B_GROUPS):
        wg = jnp.where(tril, ws_ref[gi], 0.0).astype(bf16)
        cols = slice(gi * B_GROUP_DIM, (gi + 1) * B_GROUP_DIM)
        vg = jnp.concatenate([vln[ci * CHUNK:(ci + 1) * CHUNK, cols] for ci in range(n_ch)], axis=1)
        sg = jnp.dot(wg, vg, preferred_element_type=f32) + bst[:, gi:gi + 1]
        per_group.append(sg)
    s = jnp.concatenate(
        [jnp.concatenate([per_group[gi][:, ci * B_GROUP_DIM:(ci + 1) * B_GROUP_DIM]
                          for gi in range(B_GROUPS)], axis=1)
         for ci in range(n_ch)], axis=0)

    gb = proj(2 * B_WIDTH, 3 * B_WIDTH)
    yb = (u * s * (gb * jax.nn.sigmoid(gb))).astype(bf16)
    yd_a = jnp.dot(ya_ref[...], wbr_ref[0], preferred_element_type=f32)
    yd_b = jnp.dot(yb, wbr_ref[1], preferred_element_type=f32)
    ma = proj(3 * B_WIDTH, 3 * B_WIDTH + D_MODEL)
    mb = proj(3 * B_WIDTH + D_MODEL, 3 * B_WIDTH + 2 * D_MODEL)
    merged = jax.nn.sigmoid(ma) * yd_a + jax.nn.sigmoid(mb) * yd_b
    out_ref[...] = x + jnp.dot(merged.astype(bf16), wo_ref[...], preferred_element_type=f32)


def _mix(x2, ya2, norm_g, w_b, ln_g, ln_b, w_sp, b_sp_t, w_br, w_o):
    N = x2.shape[0]
    T = TOKEN_TILE
    n_b = w_b.shape[1]
    c2 = lambda i: (0, 0)
    c3 = lambda i: (0, 0, 0)
    return pl.pallas_call(
        _mix_kernel,
        grid=(N // T,),
        in_specs=[pl.BlockSpec((T, D_MODEL), lambda i: (i, 0)),
                  pl.BlockSpec((T, A_WIDTH), lambda i: (i, 0)),
                  pl.BlockSpec((1, D_MODEL), c2),
                  pl.BlockSpec((D_MODEL, n_b), c2),
                  pl.BlockSpec((1, B_WIDTH), c2),
                  pl.BlockSpec((1, B_WIDTH), c2),
                  pl.BlockSpec((B_GROUPS, CHUNK, CHUNK), c3),
                  pl.BlockSpec((CHUNK, B_GROUPS), c2),
                  pl.BlockSpec((2, A_WIDTH, D_MODEL), c3),
                  pl.BlockSpec((D_MODEL, D_MODEL), c2)],
        out_specs=pl.BlockSpec((T, D_MODEL), lambda i: (i, 0)),
        out_shape=jax.ShapeDtypeStruct((N, D_MODEL), f32),
        compiler_params=pltpu.CompilerParams(
            dimension_semantics=("arbitrary",),
            vmem_limit_bytes=VMEM_LIMIT_BYTES),
        name="mix",
    )(x2, ya2, norm_g, w_b, ln_g, ln_b, w_sp, b_sp_t, w_br, w_o)


def _attn_side_weights(w):
    pad = jnp.zeros((D_MODEL, LANES - A_HEAD_DIM - IDX_HEADS), w.dtype)
    return jnp.concatenate(
        [w[:, _Q0:_K0], w[:, _QI0:_KI0], w[:, _GA0:_QI0],
         w[:, _K0:_V0], w[:, _KI0:_WI0],
         w[:, _V0:_GA0], w[:, _WI0:_U0], pad], axis=1).astype(bf16)


def kernel(x, norm_g, w_in, q_norm_g, k_norm_g, rel_bias, sgu_ln_g, sgu_ln_b,
           w_spatial, b_spatial, w_branch, w_out):
    B, S, D = x.shape
    depth = w_in.shape[0]
    band = _bias_band(rel_bias)
    head_of = np.arange(A_WIDTH) // A_HEAD_DIM
    bd = jnp.asarray(head_of[:, None] == head_of[None, :], dtype=bf16)
    for l in range(depth):
        w_a = _attn_side_weights(w_in[l])
        w_b = w_in[l][:, _U0:_END].astype(bf16)
        g = norm_g[l][None, :]
        qg = jnp.tile(q_norm_g[l], A_HEADS)[None, :]
        kg = jnp.concatenate([k_norm_g[l], jnp.ones((LANES - A_HEAD_DIM,), f32)])[None, :]
        q, qi, kkt, vext, widx, gate = _proj_a(x, g, w_a, qg, kg, bd)
        ya = _attn(q, qi, kkt, vext, widx, gate, band)
        x2 = _mix(x.reshape(B * S, D), ya.reshape(B * S, A_WIDTH), g, w_b,
                  sgu_ln_g[l][None, :], sgu_ln_b[l][None, :], w_spatial[l],
                  b_spatial[l].T, w_branch[l].astype(bf16), w_out[l].astype(bf16))
        x = x2.reshape(B, S, D)
    return x
```

```python
import functools
import math

import numpy as np
import jax
import jax.numpy as jnp
from jax import lax
from jax.experimental import pallas as pl
from jax.experimental.pallas import tpu as pltpu

D_MODEL = 1024
A_HEADS = 8
A_HEAD_DIM = 64
A_WIDTH = A_HEADS * A_HEAD_DIM
IDX_HEADS = 8
IDX_DIM = 64
TOPK_MAX = 256
B_GROUPS = 4
B_GROUP_DIM = 128
B_WIDTH = B_GROUPS * B_GROUP_DIM
CHUNK = 128
REL_BUCKETS = 32
REL_MAX_DIST = 128
EPS = 1e-6

LANES = 128
VMEM_LIMIT_BYTES = 56 * 1024 * 1024

TOKEN_TILE = 512
Q_TILE = 256
HALVINGS_PER_HEAD = 3

_SPLITS = (A_WIDTH, A_HEAD_DIM, A_HEAD_DIM, A_WIDTH, IDX_HEADS * IDX_DIM, IDX_DIM,
           IDX_HEADS, B_WIDTH, B_WIDTH, B_WIDTH, D_MODEL, D_MODEL)
_OFFS = np.concatenate([[0], np.cumsum(_SPLITS)])
(_Q0, _K0, _V0, _GA0, _QI0, _KI0, _WI0, _U0, _VB0, _GB0, _MA0, _MB0, _END) = [int(o) for o in _OFFS]

f32 = jnp.float32
bf16 = jnp.bfloat16


def _t5_bucket_np(rel):
    max_exact = REL_BUCKETS // 2
    nf = np.maximum(rel, 1).astype(np.float32)
    large = max_exact + (np.log(nf / np.float32(max_exact))
                         / np.float32(math.log(REL_MAX_DIST / max_exact))
                         * np.float32(REL_BUCKETS - max_exact)).astype(np.int32)
    large = np.minimum(large, REL_BUCKETS - 1)
    return np.where(rel < max_exact, rel, large).astype(np.int32)


def _near_bucket_map():
    tq = np.arange(Q_TILE)[:, None]
    tk = np.arange(2 * Q_TILE)[None, :]
    rel = Q_TILE + tq - tk
    return _t5_bucket_np(np.maximum(rel, 0))


def _rms(x, g):
    ms = jnp.mean(x * x, axis=-1, keepdims=True)
    return x * lax.rsqrt(ms + EPS) * g


def _bias_band_kernel(rb_ref, bmap_ref, out_ref):
    h = pl.program_id(0)
    bmap = bmap_ref[...]
    acc = jnp.zeros(bmap.shape, f32)
    for b in range(REL_BUCKETS):
        acc = jnp.where(bmap == b, rb_ref[b, h], acc)
    out_ref[0] = acc - rb_ref[REL_BUCKETS - 1, h]


def _bias_band(rel_bias):
    bmap = jnp.asarray(_near_bucket_map())
    return pl.pallas_call(
        _bias_band_kernel,
        grid=(A_HEADS,),
        in_specs=[pl.BlockSpec(memory_space=pltpu.SMEM),
                  pl.BlockSpec((Q_TILE, 2 * Q_TILE), lambda h: (0, 0))],
        out_specs=pl.BlockSpec((1, Q_TILE, 2 * Q_TILE), lambda h: (h, 0, 0)),
        out_shape=jax.ShapeDtypeStruct((A_HEADS, Q_TILE, 2 * Q_TILE), f32),
        name="bias_band",
    )(rel_bias, bmap)


def _split2(s):
    hi = s.astype(bf16)
    lo = (s - hi.astype(f32)).astype(bf16)
    return hi, lo


def _proj_a_kernel(x_ref, g_ref, w_ref, qg_ref, kg_ref, bd_ref,
                   q_out, qi_out, kkt_out, vext_out, widx_out, gate_out):
    h = _rms(x_ref[0], g_ref[...]).astype(bf16)

    def proj(lo, hi):
        return jnp.dot(h, w_ref[:, lo:hi], preferred_element_type=f32)


    zkk = proj(3 * A_WIDTH, 3 * A_WIDTH + LANES)
    lane = lax.broadcasted_iota(jnp.int32, zkk.shape, 1)
    is_k = lane < A_HEAD_DIM
    ssk = jnp.sum(jnp.where(is_k, zkk * zkk, 0.0), axis=-1, keepdims=True)
    kn = zkk * lax.rsqrt(ssk * (1.0 / A_HEAD_DIM) + EPS) * kg_ref[...]
    kk = jnp.where(is_k, kn, zkk)
    kkt_out[0] = kk.T.astype(bf16)

    zvw = proj(3 * A_WIDTH + LANES, 3 * A_WIDTH + 2 * LANES)
    vext = jnp.where(lane < A_HEAD_DIM, zvw, jnp.where(lane == A_HEAD_DIM, 1.0, 0.0))
    vext_out[0] = vext.astype(bf16)
    widx_out[0] = zvw[:, A_HEAD_DIM:A_HEAD_DIM + IDX_HEADS] * (IDX_HEADS ** -0.5 * IDX_DIM ** -0.5)

    zq = proj(0, A_WIDTH)
    bd = bd_ref[...]
    ssq = sum(jnp.dot(t, bd, preferred_element_type=f32) for t in _split2(zq * zq))
    qn = zq * lax.rsqrt(ssq * (1.0 / A_HEAD_DIM) + EPS) * qg_ref[...]
    qn = (qn * (A_HEAD_DIM ** -0.5)).astype(bf16)
    zqi = proj(A_WIDTH, 2 * A_WIDTH).astype(bf16)
    for hd in range(A_HEADS):
        sl = slice(hd * A_HEAD_DIM, (hd + 1) * A_HEAD_DIM)
        q_out[0, hd] = qn[:, sl]
        qi_out[0, hd] = zqi[:, sl]

    gate_out[0] = proj(2 * A_WIDTH, 3 * A_WIDTH)


def _proj_a(x, norm_g, w_a, qg, kg, bd):
    B, S, _ = x.shape
    T = TOKEN_TILE
    n_a = w_a.shape[1]
    const = lambda b, i: (0, 0)
    return pl.pallas_call(
        _proj_a_kernel,
        grid=(B, S // T),
        in_specs=[pl.BlockSpec((1, T, D_MODEL), lambda b, i: (b, i, 0)),
                  pl.BlockSpec((1, D_MODEL), const),
                  pl.BlockSpec((D_MODEL, n_a), const),
                  pl.BlockSpec((1, A_WIDTH), const),
                  pl.BlockSpec((1, LANES), const),
                  pl.BlockSpec((A_WIDTH, A_WIDTH), const)],
        out_specs=[pl.BlockSpec((1, A_HEADS, T, A_HEAD_DIM), lambda b, i: (b, 0, i, 0)),
                   pl.BlockSpec((1, IDX_HEADS, T, IDX_DIM), lambda b, i: (b, 0, i, 0)),
                   pl.BlockSpec((1, LANES, T), lambda b, i: (b, 0, i)),
                   pl.BlockSpec((1, T, LANES), lambda b, i: (b, i, 0)),
                   pl.BlockSpec((1, T, IDX_HEADS), lambda b, i: (b, i, 0)),
                   pl.BlockSpec((1, T, A_WIDTH), lambda b, i: (b, i, 0))],
        out_shape=[jax.ShapeDtypeStruct((B, A_HEADS, S, A_HEAD_DIM), bf16),
                   jax.ShapeDtypeStruct((B, IDX_HEADS, S, IDX_DIM), bf16),
                   jax.ShapeDtypeStruct((B, LANES, S), bf16),
                   jax.ShapeDtypeStruct((B, S, LANES), bf16),
                   jax.ShapeDtypeStruct((B, S, IDX_HEADS), f32),
                   jax.ShapeDtypeStruct((B, S, A_WIDTH), f32)],
        compiler_params=pltpu.CompilerParams(
            dimension_semantics=("arbitrary", "arbitrary"),
            vmem_limit_bytes=VMEM_LIMIT_BYTES),
        name="proj_a",
    )(x, norm_g, w_a, qg, kg, bd)


def _count(pred):
    return jnp.sum(jnp.where(pred, 1.0, 0.0), axis=1, keepdims=True)


def _smallest_at_least(s, lo):
    a = jnp.min(jnp.where(s >= lo, s, jnp.inf), axis=1, keepdims=True)
    return a, _count(s > a)


def _tile_scores(c, qi_ref, kkt_ref, widx_ref, sc_ref):
    nk = (c + 1) * Q_TILE
    S = sc_ref.shape[1]
    k_top = float(TOPK_MAX)
    row = lax.broadcasted_iota(jnp.int32, (Q_TILE, nk), 0) + c * Q_TILE
    col = lax.broadcasted_iota(jnp.int32, (Q_TILE, nk), 1)
    causal = col <= row
    kit = kkt_ref[0, A_HEAD_DIM:2 * A_HEAD_DIM, 0:nk]
    widx = widx_ref[0]
    if nk < S:
        sc_ref[:, nk:S] = jnp.full((Q_TILE, S - nk), -jnp.inf, f32)

    s = jnp.dot(qi_ref[0].reshape(IDX_HEADS * Q_TILE, IDX_DIM), kit,
                preferred_element_type=f32)
    sc = None
    for hd in range(IDX_HEADS):
        term = widx[:, hd:hd + 1] * jnp.maximum(s[hd * Q_TILE:(hd + 1) * Q_TILE], 0.0)
        sc = term if sc is None else sc + term
    sc = jnp.where(causal, sc, -jnp.inf)
    sc_ref[:, 0:nk] = sc
    lo0 = jnp.min(jnp.where(causal, sc, jnp.inf), axis=1, keepdims=True)
    hi0 = jnp.max(sc, axis=1, keepdims=True)
    clo0 = (lax.broadcasted_iota(jnp.int32, (Q_TILE, 1), 0) + (c * Q_TILE + 1)).astype(f32)

    def halve(st):
        lo, hi, clo = st
        mid = lo + (hi - lo) * 0.5
        cnt = _count(sc_ref[:, 0:nk] >= mid)
        ge = cnt >= k_top
        return jnp.where(ge, mid, lo), jnp.where(ge, hi, mid), jnp.where(ge, cnt, clo)

    return (lo0, hi0, clo0), halve


def _finish_selection(sc_ref, st_ref, cut_ref, upre_ref, ltri_ref):
    S = sc_ref.shape[1]
    k_top = float(TOPK_MAX)

    def unfinished(st):
        return jnp.max(st[2]) >= k_top

    def refine(st):
        cge, a, cgt = st
        s = sc_ref[...]
        nxt = jnp.min(jnp.where(s > a, s, jnp.inf), axis=1, keepdims=True)
        todo = cgt >= k_top
        cge = jnp.where(todo, cgt, cge)
        a, cgt = _smallest_at_least(s, jnp.where(todo, nxt, a))
        return cge, a, cgt

    cge, thr, cgt = lax.while_loop(unfinished, refine, (st_ref[2], st_ref[0], st_ref[1]))
    st_ref[0] = thr

    cut_ref[...] = jnp.full(cut_ref.shape, S, jnp.int32)

    @pl.when(jnp.max(cge) > k_top)
    def _():
        need = k_top - cgt
        eqf = jnp.where(sc_ref[...] == thr, 1.0, 0.0)
        n_grp = S // LANES
        before = jnp.dot(eqf.astype(bf16), upre_ref[...],
                         preferred_element_type=f32)
        lane = lax.broadcasted_iota(jnp.int32, (Q_TILE, LANES), 1)
        jstar = _count((lane < n_grp) & (before < need)) - 1.0
        base = jnp.sum(jnp.where(lane.astype(f32) == jstar, before, 0.0), axis=1, keepdims=True)
        grp = jnp.zeros((Q_TILE, LANES), f32)
        for j in range(n_grp):
            grp = jnp.where(jstar == float(j), eqf[:, j * LANES:(j + 1) * LANES], grp)
        upto = jnp.dot(grp.astype(bf16), ltri_ref[...],
                       preferred_element_type=f32)
        lstar = _count(upto < need - base)
        cut_ref[...] = (jstar * LANES + lstar).astype(jnp.int32)


def _tile_heads(c, q_ref, kkt_ref, vext_ref, gate_ref, band_ref, y_out, neg_ref, yh_ref, lg_ref,
                side_work=None, side_state=0):
    nk = (c + 1) * Q_TILE
    kt = kkt_ref[0, 0:A_HEAD_DIM, 0:nk]
    vext = vext_ref[0, 0:nk, :]
    near_w = min(nk, 2 * Q_TILE)

    def masked_logits(hd):
        logit = jnp.dot(q_ref[0, hd], kt, preferred_element_type=f32)
        near = logit[:, nk - near_w:] + (band_ref[hd][:, 2 * Q_TILE - near_w:] + neg_ref[:, nk - near_w:nk])
        if nk > near_w:
            far = logit[:, :nk - near_w] + neg_ref[:, 0:nk - near_w]
            logit = jnp.concatenate([far, near], axis=1)
        else:
            logit = near
        return logit, jnp.max(logit, axis=1, keepdims=True)

    def weighted_values(hd, m):
        p = jnp.exp(lg_ref[:, 0:nk] - m).astype(bf16)
        yh_ref[hd] = jnp.dot(p, vext, preferred_element_type=f32)

    def head(hd, carry):
        m_prev, side = carry
        p = jnp.exp(lg_ref[:, 0:nk] - m_prev).astype(bf16)
        logit, m = masked_logits(hd)
        lg_ref[:, 0:nk] = logit
        yh_ref[hd - 1] = jnp.dot(p, vext, preferred_element_type=f32)
        if side_work is not None:
            side = side_work(side)
        return m, side

    logit0, m0 = masked_logits(0)
    lg_ref[:, 0:nk] = logit0
    m_last, side_state = lax.fori_loop(1, A_HEADS, head, (m0, side_state))
    weighted_values(A_HEADS - 1, m_last)
    ys = []
    for hd in range(A_HEADS):
        pv = yh_ref[hd]
        ys.append(pv[:, 0:A_HEAD_DIM] / pv[:, A_HEAD_DIM:A_HEAD_DIM + 1])
    y = jnp.concatenate(ys, axis=1)
    g = gate_ref[0]
    y_out[0] = (y * (g * jax.nn.sigmoid(g))).astype(bf16)
    return side_state


def _tile_step(c, n_tiles, q_ref, qin_ref, kkt_ref, vext_ref, widxn_ref, gate_ref, band_ref, y_out,
               sc_ref, neg_ref, st_ref, yh_ref, lg_ref):
    if c == 0:
        row = lax.broadcasted_iota(jnp.int32, (Q_TILE, Q_TILE), 0)
        col = lax.broadcasted_iota(jnp.int32, (Q_TILE, Q_TILE), 1)
        neg_ref[:, 0:Q_TILE] = jnp.where(col <= row, 0.0, -jnp.inf)
    heads = functools.partial(_tile_heads, c, q_ref, kkt_ref, vext_ref, gate_ref, band_ref, y_out,
                              neg_ref, yh_ref, lg_ref)
    if c + 1 == n_tiles:
        heads()
        return
    start, halve = _tile_scores(c + 1, qin_ref, kkt_ref, widxn_ref, sc_ref)

    def halvings(st):
        for _ in range(HALVINGS_PER_HEAD):
            st = halve(st)
        return st

    lo, _, clo = heads(side_work=halvings, side_state=start)
    a, cgt = _smallest_at_least(sc_ref[:, 0:(c + 2) * Q_TILE], lo)
    st_ref[0] = a
    st_ref[1] = cgt
    st_ref[2] = clo


def _tile_mask(c, sc_ref, neg_ref, st_ref, cut_ref):
    nk = (c + 1) * Q_TILE
    col = lax.broadcasted_iota(jnp.int32, (Q_TILE, nk), 1)
    s = sc_ref[:, 0:nk]
    thr = st_ref[0]
    keep = (s > thr) | ((s == thr) & (col <= cut_ref[...]))
    neg_ref[:, 0:nk] = jnp.where(keep, 0.0, -jnp.inf)


def _attn_kernel(q_ref, qin_ref, kkt_ref, vext_ref, widxn_ref, gate_ref, band_ref, upre_ref, ltri_ref,
                 y_out, sc_ref, neg_ref, st_ref, cut_ref, yh_ref, lg_ref):
    i = pl.program_id(1)
    n_tiles = kkt_ref.shape[2] // Q_TILE
    for c in range(n_tiles):
        @pl.when(i == c)
        def _(c=c):
            _tile_step(c, n_tiles, q_ref, qin_ref, kkt_ref, vext_ref, widxn_ref, gate_ref, band_ref,
                       y_out, sc_ref, neg_ref, st_ref, yh_ref, lg_ref)

    @pl.when(i + 1 < n_tiles)
    def _():
        _finish_selection(sc_ref, st_ref, cut_ref, upre_ref, ltri_ref)

    for c in range(1, n_tiles):
        @pl.when(i + 1 == c)
        def _(c=c):
            _tile_mask(c, sc_ref, neg_ref, st_ref, cut_ref)


def _attn(q, qi, kkt, vext, widx, gate, band):
    B, _, S, _ = q.shape
    key_group = np.arange(S) // LANES
    upre = jnp.asarray(key_group[:, None] < np.arange(LANES)[None, :], dtype=bf16)
    ltri = jnp.asarray(np.arange(LANES)[:, None] <= np.arange(LANES)[None, :], dtype=bf16)
    n_tiles = S // Q_TILE
    nxt = lambda i: jnp.minimum(i + 1, n_tiles - 1)
    return pl.pallas_call(
        _attn_kernel,
        grid=(B, n_tiles),
        in_specs=[pl.BlockSpec((1, A_HEADS, Q_TILE, A_HEAD_DIM), lambda b, i: (b, 0, i, 0)),
                  pl.BlockSpec((1, IDX_HEADS, Q_TILE, IDX_DIM), lambda b, i: (b, 0, nxt(i), 0)),
                  pl.BlockSpec((1, LANES, S), lambda b, i: (b, 0, 0)),
                  pl.BlockSpec((1, S, LANES), lambda b, i: (b, 0, 0)),
                  pl.BlockSpec((1, Q_TILE, IDX_HEADS), lambda b, i: (b, nxt(i), 0)),
                  pl.BlockSpec((1, Q_TILE, A_WIDTH), lambda b, i: (b, i, 0)),
                  pl.BlockSpec((A_HEADS, Q_TILE, 2 * Q_TILE), lambda b, i: (0, 0, 0)),
                  pl.BlockSpec((S, LANES), lambda b, i: (0, 0)),
                  pl.BlockSpec((LANES, LANES), lambda b, i: (0, 0))],
        out_specs=pl.BlockSpec((1, Q_TILE, A_WIDTH), lambda b, i: (b, i, 0)),
        out_shape=jax.ShapeDtypeStruct((B, S, A_WIDTH), bf16),
        scratch_shapes=[pltpu.VMEM((Q_TILE, S), f32),
                        pltpu.VMEM((Q_TILE, S), f32),
                        pltpu.VMEM((3, Q_TILE, 1), f32),
                        pltpu.VMEM((Q_TILE, 1), jnp.int32),
                        pltpu.VMEM((A_HEADS, Q_TILE, LANES), f32),
                        pltpu.VMEM((Q_TILE, S), f32)],
        compiler_params=pltpu.CompilerParams(
            dimension_semantics=("arbitrary", "arbitrary"),
            vmem_limit_bytes=VMEM_LIMIT_BYTES),
        name="attn",
    )(q, qi, kkt, vext, widx, gate, band, upre, ltri)


def _mix_kernel(x_ref, ya_ref, g_ref, w_ref, lng_ref, lnb_ref, ws_ref, bst_ref, wbr_ref, wo_ref,
                out_ref):
    x = x_ref[...]
    h = _rms(x, g_ref[...]).astype(bf16)
    T = x.shape[0]

    def proj(lo, hi):
        return jnp.dot(h, w_ref[:, lo:hi], preferred_element_type=f32)

    u = jax.nn.gelu(proj(0, B_WIDTH))
    vb = jax.nn.gelu(proj(B_WIDTH, 2 * B_WIDTH))
    mu = jnp.mean(vb, axis=-1, keepdims=True)
    var = jnp.mean(jnp.square(vb - mu), axis=-1, keepdims=True)
    vln = ((vb - mu) * lax.rsqrt(var + EPS) * lng_ref[...] + lnb_ref[...]).astype(bf16)

    n_ch = T // CHUNK
    tril = (lax.broadcasted_iota(jnp.int32, (CHUNK, CHUNK), 1)
            <= lax.broadcasted_iota(jnp.int32, (CHUNK, CHUNK), 0))
    bst = bst_ref[...]
    per_group = []
    for gi in range(B_GROUPS):
        wg = jnp.where(tril, ws_ref[gi], 0.0).astype(bf16)
        cols = slice(gi * B_GROUP_DIM, (gi + 1) * B_GROUP_DIM)
        vg = jnp.concatenate([vln[ci * CHUNK:(ci + 1) * CHUNK, cols] for ci in range(n_ch)], axis=1)
        sg = jnp.dot(wg, vg, preferred_element_type=f32) + bst[:, gi:gi + 1]
        per_group.append(sg)
    s = jnp.concatenate(
        [jnp.concatenate([per_group[gi][:, ci * B_GROUP_DIM:(ci + 1) * B_GROUP_DIM]
                          for gi in range(B_GROUPS)], axis=1)
         for ci in range(n_ch)], axis=0)

    gb = proj(2 * B_WIDTH, 3 * B_WIDTH)
    yb = (u * s * (gb * jax.nn.sigmoid(gb))).astype(bf16)
    yd_a = jnp.dot(ya_ref[...], wbr_ref[0], preferred_element_type=f32)
    yd_b = jnp.dot(yb, wbr_ref[1], preferred_element_type=f32)
    ma = proj(3 * B_WIDTH, 3 * B_WIDTH + D_MODEL)
    mb = proj(3 * B_WIDTH + D_MODEL, 3 * B_WIDTH + 2 * D_MODEL)
    merged = jax.nn.sigmoid(ma) * yd_a + jax.nn.sigmoid(mb) * yd_b
    out_ref[...] = x + jnp.dot(merged.astype(bf16), wo_ref[...], preferred_element_type=f32)


def _mix(x2, ya2, norm_g, w_b, ln_g, ln_b, w_sp, b_sp_t, w_br, w_o):
    N = x2.shape[0]
    T = TOKEN_TILE
    n_b = w_b.shape[1]
    c2 = lambda i: (0, 0)
    c3 = lambda i: (0, 0, 0)
    return pl.pallas_call(
        _mix_kernel,
        grid=(N // T,),
        in_specs=[pl.BlockSpec((T, D_MODEL), lambda i: (i, 0)),
                  pl.BlockSpec((T, A_WIDTH), lambda i: (i, 0)),
                  pl.BlockSpec((1, D_MODEL), c2),
                  pl.BlockSpec((D_MODEL, n_b), c2),
                  pl.BlockSpec((1, B_WIDTH), c2),
                  pl.BlockSpec((1, B_WIDTH), c2),
                  pl.BlockSpec((B_GROUPS, CHUNK, CHUNK), c3),
                  pl.BlockSpec((CHUNK, B_GROUPS), c2),
                  pl.BlockSpec((2, A_WIDTH, D_MODEL), c3),
                  pl.BlockSpec((D_MODEL, D_MODEL), c2)],
        out_specs=pl.BlockSpec((T, D_MODEL), lambda i: (i, 0)),
        out_shape=jax.ShapeDtypeStruct((N, D_MODEL), f32),
        compiler_params=pltpu.CompilerParams(
            dimension_semantics=("arbitrary",),
            vmem_limit_bytes=VMEM_LIMIT_BYTES),
        name="mix",
    )(x2, ya2, norm_g, w_b, ln_g, ln_b, w_sp, b_sp_t, w_br, w_o)


def _attn_side_weights(w):
    pad = jnp.zeros((D_MODEL, LANES - A_HEAD_DIM - IDX_HEADS), w.dtype)
    return jnp.concatenate(
        [w[:, _Q0:_K0], w[:, _QI0:_KI0], w[:, _GA0:_QI0],
         w[:, _K0:_V0], w[:, _KI0:_WI0],
         w[:, _V0:_GA0], w[:, _WI0:_U0], pad], axis=1).astype(bf16)


def kernel(x, norm_g, w_in, q_norm_g, k_norm_g, rel_bias, sgu_ln_g, sgu_ln_b,
           w_spatial, b_spatial, w_branch, w_out):
    B, S, D = x.shape
    depth = w_in.shape[0]
    band = _bias_band(rel_bias)
    head_of = np.arange(A_WIDTH) // A_HEAD_DIM
    bd = jnp.asarray(head_of[:, None] == head_of[None, :], dtype=bf16)
    for l in range(depth):
        w_a = _attn_side_weights(w_in[l])
        w_b = w_in[l][:, _U0:_END].astype(bf16)
        g = norm_g[l][None, :]
        qg = jnp.tile(q_norm_g[l], A_HEADS)[None, :]
        kg = jnp.concatenate([k_norm_g[l], jnp.ones((LANES - A_HEAD_DIM,), f32)])[None, :]
        q, qi, kkt, vext, widx, gate = _proj_a(x, g, w_a, qg, kg, bd)
        ya = _attn(q, qi, kkt, vext, widx, gate, band)
        x2 = _mix(x.reshape(B * S, D), ya.reshape(B * S, A_WIDTH), g, w_b,
                  sgu_ln_g[l][None, :], sgu_ln_b[l][None, :], w_spatial[l],
                  b_spatial[l].T, w_branch[l].astype(bf16), w_out[l].astype(bf16))
        x = x2.reshape(B, S, D)
    return x
```

```python
import functools
import math

import numpy as np
import jax
import jax.numpy as jnp
from jax import lax
from jax.experimental import pallas as pl
from jax.experimental.pallas import tpu as pltpu

D_MODEL = 1024
A_HEADS = 8
A_HEAD_DIM = 64
A_WIDTH = A_HEADS * A_HEAD_DIM
IDX_HEADS = 8
IDX_DIM = 64
TOPK_MAX = 256
B_GROUPS = 4
B_GROUP_DIM = 128
B_WIDTH = B_GROUPS * B_GROUP_DIM
CHUNK = 128
REL_BUCKETS = 32
REL_MAX_DIST = 128
EPS = 1e-6

LANES = 128
VMEM_LIMIT_BYTES = 56 * 1024 * 1024

TOKEN_TILE = 512
Q_TILE = 256
TIE_LOCAL_MAX_KEYS = 1280
HALVINGS_PER_HEAD = 3

_SPLITS = (A_WIDTH, A_HEAD_DIM, A_HEAD_DIM, A_WIDTH, IDX_HEADS * IDX_DIM, IDX_DIM,
           IDX_HEADS, B_WIDTH, B_WIDTH, B_WIDTH, D_MODEL, D_MODEL)
_OFFS = np.concatenate([[0], np.cumsum(_SPLITS)])
(_Q0, _K0, _V0, _GA0, _QI0, _KI0, _WI0, _U0, _VB0, _GB0, _MA0, _MB0, _END) = [int(o) for o in _OFFS]

f32 = jnp.float32
bf16 = jnp.bfloat16


def _t5_bucket_np(rel):
    max_exact = REL_BUCKETS // 2
    nf = np.maximum(rel, 1).astype(np.float32)
    large = max_exact + (np.log(nf / np.float32(max_exact))
                         / np.float32(math.log(REL_MAX_DIST / max_exact))
                         * np.float32(REL_BUCKETS - max_exact)).astype(np.int32)
    large = np.minimum(large, REL_BUCKETS - 1)
    return np.where(rel < max_exact, rel, large).astype(np.int32)


def _near_bucket_map():
    tq = np.arange(Q_TILE)[:, None]
    tk = np.arange(2 * Q_TILE)[None, :]
    rel = Q_TILE + tq - tk
    return _t5_bucket_np(np.maximum(rel, 0))


def _rms(x, g):
    ms = jnp.mean(x * x, axis=-1, keepdims=True)
    return x * lax.rsqrt(ms + EPS) * g


def _bias_band_kernel(rb_ref, bmap_ref, out_ref):
    h = pl.program_id(0)
    bmap = bmap_ref[...]
    acc = jnp.zeros(bmap.shape, f32)
    for b in range(REL_BUCKETS):
        acc = jnp.where(bmap == b, rb_ref[b, h], acc)
    out_ref[0] = acc - rb_ref[REL_BUCKETS - 1, h]


def _bias_band(rel_bias):
    bmap = jnp.asarray(_near_bucket_map())
    return pl.pallas_call(
        _bias_band_kernel,
        grid=(A_HEADS,),
        in_specs=[pl.BlockSpec(memory_space=pltpu.SMEM),
                  pl.BlockSpec((Q_TILE, 2 * Q_TILE), lambda h: (0, 0))],
        out_specs=pl.BlockSpec((1, Q_TILE, 2 * Q_TILE), lambda h: (h, 0, 0)),
        out_shape=jax.ShapeDtypeStruct((A_HEADS, Q_TILE, 2 * Q_TILE), f32),
        name="bias_band",
    )(rel_bias, bmap)


def _split2(s):
    hi = s.astype(bf16)
    lo = (s - hi.astype(f32)).astype(bf16)
    return hi, lo


def _proj_a_kernel(x_ref, g_ref, w_ref, qg_ref, kg_ref, bd_ref,
                   q_out, qi_out, kkt_out, vext_out, widx_out, gate_out):
    h = _rms(x_ref[0], g_ref[...]).astype(bf16)

    def proj(lo, hi):
        return jnp.dot(h, w_ref[:, lo:hi], preferred_element_type=f32)


    zkk = proj(3 * A_WIDTH, 3 * A_WIDTH + LANES)
    lane = lax.broadcasted_iota(jnp.int32, zkk.shape, 1)
    is_k = lane < A_HEAD_DIM
    ssk = jnp.sum(jnp.where(is_k, zkk * zkk, 0.0), axis=-1, keepdims=True)
    kn = zkk * lax.rsqrt(ssk * (1.0 / A_HEAD_DIM) + EPS) * kg_ref[...]
    kk = jnp.where(is_k, kn, zkk)
    kkt_out[0] = kk.T.astype(bf16)

    zvw = proj(3 * A_WIDTH + LANES, 3 * A_WIDTH + 2 * LANES)
    vext = jnp.where(lane < A_HEAD_DIM, zvw, jnp.where(lane == A_HEAD_DIM, 1.0, 0.0))
    vext_out[0] = vext.astype(bf16)
    widx_out[0] = zvw[:, A_HEAD_DIM:A_HEAD_DIM + IDX_HEADS] * (IDX_HEADS ** -0.5 * IDX_DIM ** -0.5)

    zq = proj(0, A_WIDTH)
    bd = bd_ref[...]
    ssq = sum(jnp.dot(t, bd, preferred_element_type=f32) for t in _split2(zq * zq))
    qn = zq * lax.rsqrt(ssq * (1.0 / A_HEAD_DIM) + EPS) * qg_ref[...]
    qn = (qn * (A_HEAD_DIM ** -0.5)).astype(bf16)
    zqi = proj(A_WIDTH, 2 * A_WIDTH).astype(bf16)
    for hd in range(A_HEADS):
        sl = slice(hd * A_HEAD_DIM, (hd + 1) * A_HEAD_DIM)
        q_out[0, hd] = qn[:, sl]
        qi_out[0, hd] = zqi[:, sl]

    gate_out[0] = proj(2 * A_WIDTH, 3 * A_WIDTH)


def _proj_a(x, norm_g, w_a, qg, kg, bd):
    B, S, _ = x.shape
    T = TOKEN_TILE
    n_a = w_a.shape[1]
    const = lambda b, i: (0, 0)
    return pl.pallas_call(
        _proj_a_kernel,
        grid=(B, S // T),
        in_specs=[pl.BlockSpec((1, T, D_MODEL), lambda b, i: (b, i, 0)),
                  pl.BlockSpec((1, D_MODEL), const),
                  pl.BlockSpec((D_MODEL, n_a), const),
                  pl.BlockSpec((1, A_WIDTH), const),
                  pl.BlockSpec((1, LANES), const),
                  pl.BlockSpec((A_WIDTH, A_WIDTH), const)],
        out_specs=[pl.BlockSpec((1, A_HEADS, T, A_HEAD_DIM), lambda b, i: (b, 0, i, 0)),
                   pl.BlockSpec((1, IDX_HEADS, T, IDX_DIM), lambda b, i: (b, 0, i, 0)),
                   pl.BlockSpec((1, LANES, T), lambda b, i: (b, 0, i)),
                   pl.BlockSpec((1, T, LANES), lambda b, i: (b, i, 0)),
                   pl.BlockSpec((1, T, IDX_HEADS), lambda b, i: (b, i, 0)),
                   pl.BlockSpec((1, T, A_WIDTH), lambda b, i: (b, i, 0))],
        out_shape=[jax.ShapeDtypeStruct((B, A_HEADS, S, A_HEAD_DIM), bf16),
                   jax.ShapeDtypeStruct((B, IDX_HEADS, S, IDX_DIM), bf16),
                   jax.ShapeDtypeStruct((B, LANES, S), bf16),
                   jax.ShapeDtypeStruct((B, S, LANES), bf16),
                   jax.ShapeDtypeStruct((B, S, IDX_HEADS), f32),
                   jax.ShapeDtypeStruct((B, S, A_WIDTH), f32)],
        compiler_params=pltpu.CompilerParams(
            dimension_semantics=("arbitrary", "arbitrary"),
            vmem_limit_bytes=VMEM_LIMIT_BYTES),
        name="proj_a",
    )(x, norm_g, w_a, qg, kg, bd)


def _count(pred):
    return jnp.sum(jnp.where(pred, 1.0, 0.0), axis=1, keepdims=True)


def _smallest_at_least(s, lo):
    a = jnp.min(jnp.where(s >= lo, s, jnp.inf), axis=1, keepdims=True)
    return a, _count(s > a)


def _tile_scores(c, qi_ref, kkt_ref, widx_ref, sc_ref):
    nk = (c + 1) * Q_TILE
    S = sc_ref.shape[1]
    k_top = float(TOPK_MAX)
    row = lax.broadcasted_iota(jnp.int32, (Q_TILE, nk), 0) + c * Q_TILE
    col = lax.broadcasted_iota(jnp.int32, (Q_TILE, nk), 1)
    causal = col <= row
    kit = kkt_ref[0, A_HEAD_DIM:2 * A_HEAD_DIM, 0:nk]
    widx = widx_ref[0]
    if nk < S:
        sc_ref[:, nk:S] = jnp.full((Q_TILE, S - nk), -jnp.inf, f32)

    s = jnp.dot(qi_ref[0].reshape(IDX_HEADS * Q_TILE, IDX_DIM), kit,
                preferred_element_type=f32)
    sc = None
    for hd in range(IDX_HEADS):
        term = widx[:, hd:hd + 1] * jnp.maximum(s[hd * Q_TILE:(hd + 1) * Q_TILE], 0.0)
        sc = term if sc is None else sc + term
    sc = jnp.where(causal, sc, -jnp.inf)
    sc_ref[:, 0:nk] = sc
    lo0 = jnp.min(jnp.where(causal, sc, jnp.inf), axis=1, keepdims=True)
    hi0 = jnp.max(sc, axis=1, keepdims=True)
    clo0 = (lax.broadcasted_iota(jnp.int32, (Q_TILE, 1), 0) + (c * Q_TILE + 1)).astype(f32)

    def halve(st):
        lo, hi, clo = st
        mid = lo + (hi - lo) * 0.5
        cnt = _count(sc_ref[:, 0:nk] >= mid)
        ge = cnt >= k_top
        return jnp.where(ge, mid, lo), jnp.where(ge, hi, mid), jnp.where(ge, cnt, clo)

    return (lo0, hi0, clo0), halve


def _tie_cut(sc_ref, width, thr, cgt, cut_ref, upre_ref, ltri_ref):
    need = float(TOPK_MAX) - cgt
    eqf = jnp.where(sc_ref[:, 0:width] == thr, 1.0, 0.0)
    n_grp = width // LANES
    before = jnp.dot(eqf.astype(bf16), upre_ref[0:width, :],
                     preferred_element_type=f32)
    lane = lax.broadcasted_iota(jnp.int32, (Q_TILE, LANES), 1)
    jstar = _count((lane < n_grp) & (before < need)) - 1.0
    base = jnp.sum(jnp.where(lane.astype(f32) == jstar, before, 0.0), axis=1, keepdims=True)
    grp = jnp.zeros((Q_TILE, LANES), f32)
    for j in range(n_grp):
        grp = jnp.where(jstar == float(j), eqf[:, j * LANES:(j + 1) * LANES], grp)
    upto = jnp.dot(grp.astype(bf16), ltri_ref[...],
                   preferred_element_type=f32)
    lstar = _count(upto < need - base)
    cut_ref[...] = (jstar * LANES + lstar).astype(jnp.int32)


def _finish_selection(sc_ref, st_ref, cut_ref, upre_ref, ltri_ref, wide_tile):
    S = sc_ref.shape[1]
    k_top = float(TOPK_MAX)

    def unfinished(st):
        return jnp.max(st[2]) >= k_top

    def refine(st):
        cge, a, cgt = st
        s = sc_ref[...]
        nxt = jnp.min(jnp.where(s > a, s, jnp.inf), axis=1, keepdims=True)
        todo = cgt >= k_top
        cge = jnp.where(todo, cgt, cge)
        a, cgt = _smallest_at_least(s, jnp.where(todo, nxt, a))
        return cge, a, cgt

    cge, thr, cgt = lax.while_loop(unfinished, refine, (st_ref[2], st_ref[0], st_ref[1]))
    st_ref[0] = thr
    st_ref[1] = cgt
    st_ref[2] = cge
    cut_ref[...] = jnp.full(cut_ref.shape, S, jnp.int32)

    @pl.when(wide_tile & (jnp.max(cge) > k_top))
    def _():
        _tie_cut(sc_ref, S, thr, cgt, cut_ref, upre_ref, ltri_ref)


def _tile_heads(c, q_ref, kkt_ref, vext_ref, gate_ref, band_ref, y_out, neg_ref, yh_ref, lg_ref,
                side_work=None, side_state=0):
    nk = (c + 1) * Q_TILE
    kt = kkt_ref[0, 0:A_HEAD_DIM, 0:nk]
    vext = vext_ref[0, 0:nk, :]
    near_w = min(nk, 2 * Q_TILE)

    def masked_logits(hd):
        logit = jnp.dot(q_ref[0, hd], kt, preferred_element_type=f32)
        near = logit[:, nk - near_w:] + (band_ref[hd][:, 2 * Q_TILE - near_w:] + neg_ref[:, nk - near_w:nk])
        if nk > near_w:
            far = logit[:, :nk - near_w] + neg_ref[:, 0:nk - near_w]
            logit = jnp.concatenate([far, near], axis=1)
        else:
            logit = near
        return logit, jnp.max(logit, axis=1, keepdims=True)

    def weighted_values(hd, m):
        p = jnp.exp(lg_ref[:, 0:nk] - m).astype(bf16)
        yh_ref[hd] = jnp.dot(p, vext, preferred_element_type=f32)

    def head(hd, carry):
        m_prev, side = carry
        p = jnp.exp(lg_ref[:, 0:nk] - m_prev).astype(bf16)
        logit, m = masked_logits(hd)
        lg_ref[:, 0:nk] = logit
        yh_ref[hd - 1] = jnp.dot(p, vext, preferred_element_type=f32)
        if side_work is not None:
            side = side_work(side)
        return m, side

    logit0, m0 = masked_logits(0)
    lg_ref[:, 0:nk] = logit0
    m_last, side_state = lax.fori_loop(1, A_HEADS, head, (m0, side_state))
    weighted_values(A_HEADS - 1, m_last)
    ys = []
    for hd in range(A_HEADS):
        pv = yh_ref[hd]
        ys.append(pv[:, 0:A_HEAD_DIM] / pv[:, A_HEAD_DIM:A_HEAD_DIM + 1])
    y = jnp.concatenate(ys, axis=1)
    g = gate_ref[0]
    y_out[0] = (y * (g * jax.nn.sigmoid(g))).astype(bf16)
    return side_state


def _tile_step(c, n_tiles, q_ref, qin_ref, kkt_ref, vext_ref, widxn_ref, gate_ref, band_ref, y_out,
               sc_ref, neg_ref, st_ref, yh_ref, lg_ref):
    if c == 0:
        row = lax.broadcasted_iota(jnp.int32, (Q_TILE, Q_TILE), 0)
        col = lax.broadcasted_iota(jnp.int32, (Q_TILE, Q_TILE), 1)
        neg_ref[:, 0:Q_TILE] = jnp.where(col <= row, 0.0, -jnp.inf)
    heads = functools.partial(_tile_heads, c, q_ref, kkt_ref, vext_ref, gate_ref, band_ref, y_out,
                              neg_ref, yh_ref, lg_ref)
    if c + 1 == n_tiles:
        heads()
        return
    start, halve = _tile_scores(c + 1, qin_ref, kkt_ref, widxn_ref, sc_ref)

    def halvings(st):
        for _ in range(HALVINGS_PER_HEAD):
            st = halve(st)
        return st

    lo, _, clo = heads(side_work=halvings, side_state=start)
    a, cgt = _smallest_at_least(sc_ref[:, 0:(c + 2) * Q_TILE], lo)
    st_ref[0] = a
    st_ref[1] = cgt
    st_ref[2] = clo


def _tile_mask(c, sc_ref, neg_ref, st_ref, cut_ref, upre_ref, ltri_ref):
    nk = (c + 1) * Q_TILE
    thr, cgt, cge = st_ref[0], st_ref[1], st_ref[2]
    if nk <= TIE_LOCAL_MAX_KEYS:
        @pl.when(jnp.max(cge) > float(TOPK_MAX))
        def _():
            _tie_cut(sc_ref, nk, thr, cgt, cut_ref, upre_ref, ltri_ref)
    col = lax.broadcasted_iota(jnp.int32, (Q_TILE, nk), 1)
    s = sc_ref[:, 0:nk]
    keep = (s > thr) | ((s == thr) & (col <= cut_ref[...]))
    neg_ref[:, 0:nk] = jnp.where(keep, 0.0, -jnp.inf)


def _attn_kernel(q_ref, qin_ref, kkt_ref, vext_ref, widxn_ref, gate_ref, band_ref, upre_ref, ltri_ref,
                 y_out, sc_ref, neg_ref, st_ref, cut_ref, yh_ref, lg_ref):
    i = pl.program_id(1)
    n_tiles = kkt_ref.shape[2] // Q_TILE
    for c in range(n_tiles):
        @pl.when(i == c)
        def _(c=c):
            _tile_step(c, n_tiles, q_ref, qin_ref, kkt_ref, vext_ref, widxn_ref, gate_ref, band_ref,
                       y_out, sc_ref, neg_ref, st_ref, yh_ref, lg_ref)

    @pl.when(i + 1 < n_tiles)
    def _():
        _finish_selection(sc_ref, st_ref, cut_ref, upre_ref, ltri_ref,
                          (i + 2) * Q_TILE > TIE_LOCAL_MAX_KEYS)

    for c in range(1, n_tiles):
        @pl.when(i + 1 == c)
        def _(c=c):
            _tile_mask(c, sc_ref, neg_ref, st_ref, cut_ref, upre_ref, ltri_ref)


def _attn(q, qi, kkt, vext, widx, gate, band):
    B, _, S, _ = q.shape
    key_group = np.arange(S) // LANES
    upre = jnp.asarray(key_group[:, None] < np.arange(LANES)[None, :], dtype=bf16)
    ltri = jnp.asarray(np.arange(LANES)[:, None] <= np.arange(LANES)[None, :], dtype=bf16)
    n_tiles = S // Q_TILE
    nxt = lambda i: jnp.minimum(i + 1, n_tiles - 1)
    return pl.pallas_call(
        _attn_kernel,
        grid=(B, n_tiles),
        in_specs=[pl.BlockSpec((1, A_HEADS, Q_TILE, A_HEAD_DIM), lambda b, i: (b, 0, i, 0)),
                  pl.BlockSpec((1, IDX_HEADS, Q_TILE, IDX_DIM), lambda b, i: (b, 0, nxt(i), 0)),
                  pl.BlockSpec((1, LANES, S), lambda b, i: (b, 0, 0)),
                  pl.BlockSpec((1, S, LANES), lambda b, i: (b, 0, 0)),
                  pl.BlockSpec((1, Q_TILE, IDX_HEADS), lambda b, i: (b, nxt(i), 0)),
                  pl.BlockSpec((1, Q_TILE, A_WIDTH), lambda b, i: (b, i, 0)),
                  pl.BlockSpec((A_HEADS, Q_TILE, 2 * Q_TILE), lambda b, i: (0, 0, 0)),
                  pl.BlockSpec((S, LANES), lambda b, i: (0, 0)),
                  pl.BlockSpec((LANES, LANES), lambda b, i: (0, 0))],
        out_specs=pl.BlockSpec((1, Q_TILE, A_WIDTH), lambda b, i: (b, i, 0)),
        out_shape=jax.ShapeDtypeStruct((B, S, A_WIDTH), bf16),
        scratch_shapes=[pltpu.VMEM((Q_TILE, S), f32),
                        pltpu.VMEM((Q_TILE, S), f32),
                        pltpu.VMEM((3, Q_TILE, 1), f32),
                        pltpu.VMEM((Q_TILE, 1), jnp.int32),
                        pltpu.VMEM((A_HEADS, Q_TILE, LANES), f32),
                        pltpu.VMEM((Q_TILE, S), f32)],
        compiler_params=pltpu.CompilerParams(
            dimension_semantics=("arbitrary", "arbitrary"),
            vmem_limit_bytes=VMEM_LIMIT_BYTES),
        name="attn",
    )(q, qi, kkt, vext, widx, gate, band, upre, ltri)


def _mix_kernel(x_ref, ya_ref, g_ref, w_ref, lng_ref, lnb_ref, ws_ref, bst_ref, wbr_ref, wo_ref,
                out_ref):
    x = x_ref[...]
    h = _rms(x, g_ref[...]).astype(bf16)
    T = x.shape[0]

    def proj(lo, hi):
        return jnp.dot(h, w_ref[:, lo:hi], preferred_element_type=f32)

    u = jax.nn.gelu(proj(0, B_WIDTH))
    vb = jax.nn.gelu(proj(B_WIDTH, 2 * B_WIDTH))
    mu = jnp.mean(vb, axis=-1, keepdims=True)
    var = jnp.mean(jnp.square(vb - mu), axis=-1, keepdims=True)
    vln = ((vb - mu) * lax.rsqrt(var + EPS) * lng_ref[...] + lnb_ref[...]).astype(bf16)

    n_ch = T // CHUNK
    tril = (lax.broadcasted_iota(jnp.int32, (CHUNK, CHUNK), 1)
            <= lax.broadcasted_iota(jnp.int32, (CHUNK, CHUNK), 0))
    bst = bst_ref[...]
    per_group = []
    for gi in range(B_GROUPS):
        wg = jnp.where(tril, ws_ref[gi], 0.0).astype(bf16)
        cols = slice(gi * B_GROUP_DIM, (gi + 1) * B_GROUP_DIM)
        vg = jnp.concatenate([vln[ci * CHUNK:(ci + 1) * CHUNK, cols] for ci in range(n_ch)], axis=1)
        sg = jnp.dot(wg, vg, preferred_element_type=f32) + bst[:, gi:gi + 1]
        per_group.append(sg)
    s = jnp.concatenate(
        [jnp.concatenate([per_group[gi][:, ci * B_GROUP_DIM:(ci + 1) * B_GROUP_DIM]
                          for gi in range(B_GROUPS)], axis=1)
         for ci in range(n_ch)], axis=0)

    gb = proj(2 * B_WIDTH, 3 * B_WIDTH)
    yb = (u * s * (gb * jax.nn.sigmoid(gb))).astype(bf16)
    yd_a = jnp.dot(ya_ref[...], wbr_ref[0], preferred_element_type=f32)
    yd_b = jnp.dot(yb, wbr_ref[1], preferred_element_type=f32)
    ma = proj(3 * B_WIDTH, 3 * B_WIDTH + D_MODEL)
    mb = proj(3 * B_WIDTH + D_MODEL, 3 * B_WIDTH + 2 * D_MODEL)
    merged = jax.nn.sigmoid(ma) * yd_a + jax.nn.sigmoid(mb) * yd_b
    out_ref[...] = x + jnp.dot(merged.astype(bf16), wo_ref[...], preferred_element_type=f32)


def _mix(x2, ya2, norm_g, w_b, ln_g, ln_b, w_sp, b_sp_t, w_br, w_o):
    N = x2.shape[0]
    T = TOKEN_TILE
    n_b = w_b.shape[1]
    c2 = lambda i: (0, 0)
    c3 = lambda i: (0, 0, 0)
    return pl.pallas_call(
        _mix_kernel,
        grid=(N // T,),
        in_specs=[pl.BlockSpec((T, D_MODEL), lambda i: (i, 0)),
                  pl.BlockSpec((T, A_WIDTH), lambda i: (i, 0)),
                  pl.BlockSpec((1, D_MODEL), c2),
                  pl.BlockSpec((D_MODEL, n_b), c2),
                  pl.BlockSpec((1, B_WIDTH), c2),
                  pl.BlockSpec((1, B_WIDTH), c2),
                  pl.BlockSpec((B_GROUPS, CHUNK, CHUNK), c3),
                  pl.BlockSpec((CHUNK, B_GROUPS), c2),
                  pl.BlockSpec((2, A_WIDTH, D_MODEL), c3),
                  pl.BlockSpec((D_MODEL, D_MODEL), c2)],
        out_specs=pl.BlockSpec((T, D_MODEL), lambda i: (i, 0)),
        out_shape=jax.ShapeDtypeStruct((N, D_MODEL), f32),
        compiler_params=pltpu.CompilerParams(
            dimension_semantics=("arbitrary",),
            vmem_limit_bytes=VMEM_LIMIT_BYTES),
        name="mix",
    )(x2, ya2, norm_g, w_b, ln_g, ln_b, w_sp, b_sp_t, w_br, w_o)


def _attn_side_weights(w):
    pad = jnp.zeros((D_MODEL, LANES - A_HEAD_DIM - IDX_HEADS), w.dtype)
    return jnp.concatenate(
        [w[:, _Q0:_K0], w[:, _QI0:_KI0], w[:, _GA0:_QI0],
         w[:, _K0:_V0], w[:, _KI0:_WI0],
         w[:, _V0:_GA0], w[:, _WI0:_U0], pad], axis=1).astype(bf16)


def kernel(x, norm_g, w_in, q_norm_g, k_norm_g, rel_bias, sgu_ln_g, sgu_ln_b,
           w_spatial, b_spatial, w_branch, w_out):
    B, S, D = x.shape
    depth = w_in.shape[0]
    band = _bias_band(rel_bias)
    head_of = np.arange(A_WIDTH) // A_HEAD_DIM
    bd = jnp.asarray(head_of[:, None] == head_of[None, :], dtype=bf16)
    for l in range(depth):
        w_a = _attn_side_weights(w_in[l])
        w_b = w_in[l][:, _U0:_END].astype(bf16)
        g = norm_g[l][None, :]
        qg = jnp.tile(q_norm_g[l], A_HEADS)[None, :]
        kg = jnp.concatenate([k_norm_g[l], jnp.ones((LANES - A_HEAD_DIM,), f32)])[None, :]
        q, qi, kkt, vext, widx, gate = _proj_a(x, g, w_a, qg, kg, bd)
        ya = _attn(q, qi, kkt, vext, widx, gate, band)
        x2 = _mix(x.reshape(B * S, D), ya.reshape(B * S, A_WIDTH), g, w_b,
                  sgu_ln_g[l][None, :], sgu_ln_b[l][None, :], w_spatial[l],
                  b_spatial[l].T, w_branch[l].astype(bf16), w_out[l].astype(bf16))
        x = x2.reshape(B, S, D)
    return x
```

```python
import functools
import math

import numpy as np
import jax
import jax.numpy as jnp
from jax import lax
from jax.experimental import pallas as pl
from jax.experimental.pallas import tpu as pltpu

D_MODEL = 1024
A_HEADS = 8
A_HEAD_DIM = 64
A_WIDTH = A_HEADS * A_HEAD_DIM
IDX_HEADS = 8
IDX_DIM = 64
TOPK_MAX = 256
B_GROUPS = 4
B_GROUP_DIM = 128
B_WIDTH = B_GROUPS * B_GROUP_DIM
CHUNK = 128
REL_BUCKETS = 32
REL_MAX_DIST = 128
EPS = 1e-6
LOG2E = math.log2(math.e)

LANES = 128
VMEM_LIMIT_BYTES = 56 * 1024 * 1024

TOKEN_TILE = 512
Q_TILE = 256
HALVINGS_PER_HEAD = 3

_SPLITS = (A_WIDTH, A_HEAD_DIM, A_HEAD_DIM, A_WIDTH, IDX_HEADS * IDX_DIM, IDX_DIM,
           IDX_HEADS, B_WIDTH, B_WIDTH, B_WIDTH, D_MODEL, D_MODEL)
_OFFS = np.concatenate([[0], np.cumsum(_SPLITS)])
(_Q0, _K0, _V0, _GA0, _QI0, _KI0, _WI0, _U0, _VB0, _GB0, _MA0, _MB0, _END) = [int(o) for o in _OFFS]

f32 = jnp.float32
bf16 = jnp.bfloat16


def _t5_bucket_np(rel):
    max_exact = REL_BUCKETS // 2
    nf = np.maximum(rel, 1).astype(np.float32)
    large = max_exact + (np.log(nf / np.float32(max_exact))
                         / np.float32(math.log(REL_MAX_DIST / max_exact))
                         * np.float32(REL_BUCKETS - max_exact)).astype(np.int32)
    large = np.minimum(large, REL_BUCKETS - 1)
    return np.where(rel < max_exact, rel, large).astype(np.int32)


def _near_bucket_map():
    tq = np.arange(Q_TILE)[:, None]
    tk = np.arange(2 * Q_TILE)[None, :]
    rel = Q_TILE + tq - tk
    return _t5_bucket_np(np.maximum(rel, 0))


def _rms(x, g):
    ms = jnp.mean(x * x, axis=-1, keepdims=True)
    return x * lax.rsqrt(ms + EPS) * g


def _bias_band_kernel(rb_ref, bmap_ref, out_ref):
    h = pl.program_id(0)
    bmap = bmap_ref[...]
    acc = jnp.zeros(bmap.shape, f32)
    for b in range(REL_BUCKETS):
        acc = jnp.where(bmap == b, rb_ref[b, h], acc)
    out_ref[0] = (acc - rb_ref[REL_BUCKETS - 1, h]) * LOG2E


def _bias_band(rel_bias):
    bmap = jnp.asarray(_near_bucket_map())
    return pl.pallas_call(
        _bias_band_kernel,
        grid=(A_HEADS,),
        in_specs=[pl.BlockSpec(memory_space=pltpu.SMEM),
                  pl.BlockSpec((Q_TILE, 2 * Q_TILE), lambda h: (0, 0))],
        out_specs=pl.BlockSpec((1, Q_TILE, 2 * Q_TILE), lambda h: (h, 0, 0)),
        out_shape=jax.ShapeDtypeStruct((A_HEADS, Q_TILE, 2 * Q_TILE), f32),
        name="bias_band",
    )(rel_bias, bmap)


def _split2(s):
    hi = s.astype(bf16)
    lo = (s - hi.astype(f32)).astype(bf16)
    return hi, lo


def _proj_a_kernel(x_ref, g_ref, w_ref, qg_ref, kg_ref, bd_ref,
                   q_out, qi_out, kkt_out, vext_out, widx_out, gate_out):
    h = _rms(x_ref[0], g_ref[...]).astype(bf16)

    def proj(lo, hi):
        return jnp.dot(h, w_ref[:, lo:hi], preferred_element_type=f32)


    zkk = proj(3 * A_WIDTH, 3 * A_WIDTH + LANES)
    lane = lax.broadcasted_iota(jnp.int32, zkk.shape, 1)
    is_k = lane < A_HEAD_DIM
    ssk = jnp.sum(jnp.where(is_k, zkk * zkk, 0.0), axis=-1, keepdims=True)
    kn = zkk * lax.rsqrt(ssk * (1.0 / A_HEAD_DIM) + EPS) * kg_ref[...]
    kk = jnp.where(is_k, kn, zkk)
    kkt_out[0] = kk.T.astype(bf16)

    zvw = proj(3 * A_WIDTH + LANES, 3 * A_WIDTH + 2 * LANES)
    vext = jnp.where(lane < A_HEAD_DIM, zvw, jnp.where(lane == A_HEAD_DIM, 1.0, 0.0))
    vext_out[0] = vext.astype(bf16)
    widx_out[0] = zvw[:, A_HEAD_DIM:A_HEAD_DIM + IDX_HEADS] * (IDX_HEADS ** -0.5 * IDX_DIM ** -0.5)

    zq = proj(0, A_WIDTH)
    bd = bd_ref[...]
    ssq = sum(jnp.dot(t, bd, preferred_element_type=f32) for t in _split2(zq * zq))
    qn = zq * lax.rsqrt(ssq * (1.0 / A_HEAD_DIM) + EPS) * qg_ref[...]
    qn = (qn * (A_HEAD_DIM ** -0.5 * LOG2E)).astype(bf16)
    zqi = proj(A_WIDTH, 2 * A_WIDTH).astype(bf16)
    for hd in range(A_HEADS):
        sl = slice(hd * A_HEAD_DIM, (hd + 1) * A_HEAD_DIM)
        q_out[0, hd] = qn[:, sl]
        qi_out[0, hd] = zqi[:, sl]

    gate_out[0] = proj(2 * A_WIDTH, 3 * A_WIDTH)


def _proj_a(x, norm_g, w_a, qg, kg, bd):
    B, S, _ = x.shape
    T = TOKEN_TILE
    n_a = w_a.shape[1]
    const = lambda b, i: (0, 0)
    return pl.pallas_call(
        _proj_a_kernel,
        grid=(B, S // T),
        in_specs=[pl.BlockSpec((1, T, D_MODEL), lambda b, i: (b, i, 0)),
                  pl.BlockSpec((1, D_MODEL), const),
                  pl.BlockSpec((D_MODEL, n_a), const),
                  pl.BlockSpec((1, A_WIDTH), const),
                  pl.BlockSpec((1, LANES), const),
                  pl.BlockSpec((A_WIDTH, A_WIDTH), const)],
        out_specs=[pl.BlockSpec((1, A_HEADS, T, A_HEAD_DIM), lambda b, i: (b, 0, i, 0)),
                   pl.BlockSpec((1, IDX_HEADS, T, IDX_DIM), lambda b, i: (b, 0, i, 0)),
                   pl.BlockSpec((1, LANES, T), lambda b, i: (b, 0, i)),
                   pl.BlockSpec((1, T, LANES), lambda b, i: (b, i, 0)),
                   pl.BlockSpec((1, T, IDX_HEADS), lambda b, i: (b, i, 0)),
                   pl.BlockSpec((1, T, A_WIDTH), lambda b, i: (b, i, 0))],
        out_shape=[jax.ShapeDtypeStruct((B, A_HEADS, S, A_HEAD_DIM), bf16),
                   jax.ShapeDtypeStruct((B, IDX_HEADS, S, IDX_DIM), bf16),
                   jax.ShapeDtypeStruct((B, LANES, S), bf16),
                   jax.ShapeDtypeStruct((B, S, LANES), bf16),
                   jax.ShapeDtypeStruct((B, S, IDX_HEADS), f32),
                   jax.ShapeDtypeStruct((B, S, A_WIDTH), f32)],
        compiler_params=pltpu.CompilerParams(
            dimension_semantics=("arbitrary", "arbitrary"),
            vmem_limit_bytes=VMEM_LIMIT_BYTES),
        name="proj_a",
    )(x, norm_g, w_a, qg, kg, bd)


def _count(pred):
    return jnp.sum(jnp.where(pred, 1.0, 0.0), axis=1, keepdims=True)


def _smallest_at_least(s, lo):
    a = jnp.min(jnp.where(s >= lo, s, jnp.inf), axis=1, keepdims=True)
    return a, _count(s > a)


def _tile_scores(c, qi_ref, kkt_ref, widx_ref, sc_ref):
    nk = (c + 1) * Q_TILE
    S = sc_ref.shape[1]
    k_top = float(TOPK_MAX)
    row = lax.broadcasted_iota(jnp.int32, (Q_TILE, nk), 0) + c * Q_TILE
    col = lax.broadcasted_iota(jnp.int32, (Q_TILE, nk), 1)
    causal = col <= row
    kit = kkt_ref[0, A_HEAD_DIM:2 * A_HEAD_DIM, 0:nk]
    widx = widx_ref[0]
    if nk < S:
        sc_ref[:, nk:S] = jnp.full((Q_TILE, S - nk), -jnp.inf, f32)

    s = jnp.dot(qi_ref[0].reshape(IDX_HEADS * Q_TILE, IDX_DIM), kit,
                preferred_element_type=f32)
    sc = None
    for hd in range(IDX_HEADS):
        term = widx[:, hd:hd + 1] * jnp.maximum(s[hd * Q_TILE:(hd + 1) * Q_TILE], 0.0)
        sc = term if sc is None else sc + term
    sc = jnp.where(causal, sc, -jnp.inf)
    sc_ref[:, 0:nk] = sc
    lo0 = jnp.min(jnp.where(causal, sc, jnp.inf), axis=1, keepdims=True)
    hi0 = jnp.max(sc, axis=1, keepdims=True)
    clo0 = (lax.broadcasted_iota(jnp.int32, (Q_TILE, 1), 0) + (c * Q_TILE + 1)).astype(f32)

    def halve(st):
        lo, hi, clo = st
        mid = lo + (hi - lo) * 0.5
        cnt = _count(sc_ref[:, 0:nk] >= mid)
        ge = cnt >= k_top
        return jnp.where(ge, mid, lo), jnp.where(ge, hi, mid), jnp.where(ge, cnt, clo)

    return (lo0, hi0, clo0), halve


def _finish_selection(sc_ref, st_ref, cut_ref, upre_ref, ltri_ref):
    S = sc_ref.shape[1]
    k_top = float(TOPK_MAX)

    def unfinished(st):
        return jnp.max(st[2]) >= k_top

    def refine(st):
        cge, a, cgt = st
        s = sc_ref[...]
        nxt = jnp.min(jnp.where(s > a, s, jnp.inf), axis=1, keepdims=True)
        todo = cgt >= k_top
        cge = jnp.where(todo, cgt, cge)
        a, cgt = _smallest_at_least(s, jnp.where(todo, nxt, a))
        return cge, a, cgt

    cge, thr, cgt = lax.while_loop(unfinished, refine, (st_ref[2], st_ref[0], st_ref[1]))
    st_ref[0] = thr

    cut_ref[...] = jnp.full(cut_ref.shape, S, jnp.int32)

    @pl.when(jnp.max(cge) > k_top)
    def _():
        need = k_top - cgt
        eqf = jnp.where(sc_ref[...] == thr, 1.0, 0.0)
        n_grp = S // LANES
        before = jnp.dot(eqf.astype(bf16), upre_ref[...],
                         preferred_element_type=f32)
        lane = lax.broadcasted_iota(jnp.int32, (Q_TILE, LANES), 1)
        jstar = _count((lane < n_grp) & (before < need)) - 1.0
        base = jnp.sum(jnp.where(lane.astype(f32) == jstar, before, 0.0), axis=1, keepdims=True)
        grp = jnp.zeros((Q_TILE, LANES), f32)
        for j in range(n_grp):
            grp = jnp.where(jstar == float(j), eqf[:, j * LANES:(j + 1) * LANES], grp)
        upto = jnp.dot(grp.astype(bf16), ltri_ref[...],
                       preferred_element_type=f32)
        lstar = _count(upto < need - base)
        cut_ref[...] = (jstar * LANES + lstar).astype(jnp.int32)


def _tile_heads(c, q_ref, kkt_ref, vext_ref, gate_ref, band_ref, y_out, neg_ref, yh_ref, lg_ref,
                side_work=None, side_state=0):
    nk = (c + 1) * Q_TILE
    kt = kkt_ref[0, 0:A_HEAD_DIM, 0:nk]
    vext = vext_ref[0, 0:nk, :]
    near_w = min(nk, 2 * Q_TILE)

    def masked_logits(hd):
        logit = jnp.dot(q_ref[0, hd], kt, preferred_element_type=f32)
        near = logit[:, nk - near_w:] + (band_ref[hd][:, 2 * Q_TILE - near_w:] + neg_ref[:, nk - near_w:nk])
        if nk > near_w:
            far = logit[:, :nk - near_w] + neg_ref[:, 0:nk - near_w]
            logit = jnp.concatenate([far, near], axis=1)
        else:
            logit = near
        return logit, jnp.max(logit, axis=1, keepdims=True)

    def weighted_values(hd, m):
        p = jnp.exp2(lg_ref[:, 0:nk] - m).astype(bf16)
        yh_ref[hd] = jnp.dot(p, vext, preferred_element_type=f32)

    def head(hd, carry):
        m_prev, side = carry
        p = jnp.exp2(lg_ref[:, 0:nk] - m_prev).astype(bf16)
        logit, m = masked_logits(hd)
        lg_ref[:, 0:nk] = logit
        yh_ref[hd - 1] = jnp.dot(p, vext, preferred_element_type=f32)
        if side_work is not None:
            side = side_work(side)
        return m, side

    logit0, m0 = masked_logits(0)
    lg_ref[:, 0:nk] = logit0
    m_last, side_state = lax.fori_loop(1, A_HEADS, head, (m0, side_state))
    weighted_values(A_HEADS - 1, m_last)
    ys = []
    for hd in range(A_HEADS):
        pv = yh_ref[hd]
        ys.append(pv[:, 0:A_HEAD_DIM] / pv[:, A_HEAD_DIM:A_HEAD_DIM + 1])
    y = jnp.concatenate(ys, axis=1)
    g = gate_ref[0]
    y_out[0] = (y * (g * jax.nn.sigmoid(g))).astype(bf16)
    return side_state


def _tile_step(c, n_tiles, q_ref, qin_ref, kkt_ref, vext_ref, widxn_ref, gate_ref, band_ref, y_out,
               sc_ref, neg_ref, st_ref, yh_ref, lg_ref):
    if c == 0:
        row = lax.broadcasted_iota(jnp.int32, (Q_TILE, Q_TILE), 0)
        col = lax.broadcasted_iota(jnp.int32, (Q_TILE, Q_TILE), 1)
        neg_ref[:, 0:Q_TILE] = jnp.where(col <= row, 0.0, -jnp.inf)
    heads = functools.partial(_tile_heads, c, q_ref, kkt_ref, vext_ref, gate_ref, band_ref, y_out,
                              neg_ref, yh_ref, lg_ref)
    if c + 1 == n_tiles:
        heads()
        return
    start, halve = _tile_scores(c + 1, qin_ref, kkt_ref, widxn_ref, sc_ref)

    def halvings(st):
        for _ in range(HALVINGS_PER_HEAD):
            st = halve(st)
        return st

    lo, _, clo = heads(side_work=halvings, side_state=start)
    a, cgt = _smallest_at_least(sc_ref[:, 0:(c + 2) * Q_TILE], lo)
    st_ref[0] = a
    st_ref[1] = cgt
    st_ref[2] = clo


def _tile_mask(c, sc_ref, neg_ref, st_ref, cut_ref):
    nk = (c + 1) * Q_TILE
    col = lax.broadcasted_iota(jnp.int32, (Q_TILE, nk), 1)
    s = sc_ref[:, 0:nk]
    thr = st_ref[0]
    keep = (s > thr) | ((s == thr) & (col <= cut_ref[...]))
    neg_ref[:, 0:nk] = jnp.where(keep, 0.0, -jnp.inf)


def _attn_kernel(q_ref, qin_ref, kkt_ref, vext_ref, widxn_ref, gate_ref, band_ref, upre_ref, ltri_ref,
                 y_out, sc_ref, neg_ref, st_ref, cut_ref, yh_ref, lg_ref):
    i = pl.program_id(1)
    n_tiles = kkt_ref.shape[2] // Q_TILE
    for c in range(n_tiles):
        @pl.when(i == c)
        def _(c=c):
            _tile_step(c, n_tiles, q_ref, qin_ref, kkt_ref, vext_ref, widxn_ref, gate_ref, band_ref,
                       y_out, sc_ref, neg_ref, st_ref, yh_ref, lg_ref)

    @pl.when(i + 1 < n_tiles)
    def _():
        _finish_selection(sc_ref, st_ref, cut_ref, upre_ref, ltri_ref)

    for c in range(1, n_tiles):
        @pl.when(i + 1 == c)
        def _(c=c):
            _tile_mask(c, sc_ref, neg_ref, st_ref, cut_ref)


def _attn(q, qi, kkt, vext, widx, gate, band):
    B, _, S, _ = q.shape
    key_group = np.arange(S) // LANES
    upre = jnp.asarray(key_group[:, None] < np.arange(LANES)[None, :], dtype=bf16)
    ltri = jnp.asarray(np.arange(LANES)[:, None] <= np.arange(LANES)[None, :], dtype=bf16)
    n_tiles = S // Q_TILE
    nxt = lambda i: jnp.minimum(i + 1, n_tiles - 1)
    return pl.pallas_call(
        _attn_kernel,
        grid=(B, n_tiles),
        in_specs=[pl.BlockSpec((1, A_HEADS, Q_TILE, A_HEAD_DIM), lambda b, i: (b, 0, i, 0)),
                  pl.BlockSpec((1, IDX_HEADS, Q_TILE, IDX_DIM), lambda b, i: (b, 0, nxt(i), 0)),
                  pl.BlockSpec((1, LANES, S), lambda b, i: (b, 0, 0)),
                  pl.BlockSpec((1, S, LANES), lambda b, i: (b, 0, 0)),
                  pl.BlockSpec((1, Q_TILE, IDX_HEADS), lambda b, i: (b, nxt(i), 0)),
                  pl.BlockSpec((1, Q_TILE, A_WIDTH), lambda b, i: (b, i, 0)),
                  pl.BlockSpec((A_HEADS, Q_TILE, 2 * Q_TILE), lambda b, i: (0, 0, 0)),
                  pl.BlockSpec((S, LANES), lambda b, i: (0, 0)),
                  pl.BlockSpec((LANES, LANES), lambda b, i: (0, 0))],
        out_specs=pl.BlockSpec((1, Q_TILE, A_WIDTH), lambda b, i: (b, i, 0)),
        out_shape=jax.ShapeDtypeStruct((B, S, A_WIDTH), bf16),
        scratch_shapes=[pltpu.VMEM((Q_TILE, S), f32),
                        pltpu.VMEM((Q_TILE, S), f32),
                        pltpu.VMEM((3, Q_TILE, 1), f32),
                        pltpu.VMEM((Q_TILE, 1), jnp.int32),
                        pltpu.VMEM((A_HEADS, Q_TILE, LANES), f32),
                        pltpu.VMEM((Q_TILE, S), f32)],
        compiler_params=pltpu.CompilerParams(
            dimension_semantics=("arbitrary", "arbitrary"),
            vmem_limit_bytes=VMEM_LIMIT_BYTES),
        name="attn",
    )(q, qi, kkt, vext, widx, gate, band, upre, ltri)


def _mix_kernel(x_ref, ya_ref, g_ref, w_ref, lng_ref, lnb_ref, ws_ref, bst_ref, wbr_ref, wo_ref,
                out_ref):
    x = x_ref[...]
    h = _rms(x, g_ref[...]).astype(bf16)
    T = x.shape[0]

    def proj(lo, hi):
        return jnp.dot(h, w_ref[:, lo:hi], preferred_element_type=f32)

    u = jax.nn.gelu(proj(0, B_WIDTH))
    vb = jax.nn.gelu(proj(B_WIDTH, 2 * B_WIDTH))
    mu = jnp.mean(vb, axis=-1, keepdims=True)
    var = jnp.mean(jnp.square(vb - mu), axis=-1, keepdims=True)
    vln = ((vb - mu) * lax.rsqrt(var + EPS) * lng_ref[...] + lnb_ref[...]).astype(bf16)

    n_ch = T // CHUNK
    tril = (lax.broadcasted_iota(jnp.int32, (CHUNK, CHUNK), 1)
            <= lax.broadcasted_iota(jnp.int32, (CHUNK, CHUNK), 0))
    bst = bst_ref[...]
    per_group = []
    for gi in range(B_GROUPS):
        wg = jnp.where(tril, ws_ref[gi], 0.0).astype(bf16)
        cols = slice(gi * B_GROUP_DIM, (gi + 1) * B_GROUP_DIM)
        vg = jnp.concatenate([vln[ci * CHUNK:(ci + 1) * CHUNK, cols] for ci in range(n_ch)], axis=1)
        sg = jnp.dot(wg, vg, preferred_element_type=f32) + bst[:, gi:gi + 1]
        per_group.append(sg)
    s = jnp.concatenate(
        [jnp.concatenate([per_group[gi][:, ci * B_GROUP_DIM:(ci + 1) * B_GROUP_DIM]
                          for gi in range(B_GROUPS)], axis=1)
         for ci in range(n_ch)], axis=0)

    gb = proj(2 * B_WIDTH, 3 * B_WIDTH)
    yb = (u * s * (gb * jax.nn.sigmoid(gb))).astype(bf16)
    yd_a = jnp.dot(ya_ref[...], wbr_ref[0], preferred_element_type=f32)
    yd_b = jnp.dot(yb, wbr_ref[1], preferred_element_type=f32)
    ma = proj(3 * B_WIDTH, 3 * B_WIDTH + D_MODEL)
    mb = proj(3 * B_WIDTH + D_MODEL, 3 * B_WIDTH + 2 * D_MODEL)
    merged = jax.nn.sigmoid(ma) * yd_a + jax.nn.sigmoid(mb) * yd_b
    out_ref[...] = x + jnp.dot(merged.astype(bf16), wo_ref[...], preferred_element_type=f32)


def _mix(x2, ya2, norm_g, w_b, ln_g, ln_b, w_sp, b_sp_t, w_br, w_o):
    N = x2.shape[0]
    T = TOKEN_TILE
    n_b = w_b.shape[1]
    c2 = lambda i: (0, 0)
    c3 = lambda i: (0, 0, 0)
    return pl.pallas_call(
        _mix_kernel,
        grid=(N // T,),
        in_specs=[pl.BlockSpec((T, D_MODEL), lambda i: (i, 0)),
                  pl.BlockSpec((T, A_WIDTH), lambda i: (i, 0)),
                  pl.BlockSpec((1, D_MODEL), c2),
                  pl.BlockSpec((D_MODEL, n_b), c2),
                  pl.BlockSpec((1, B_WIDTH), c2),
                  pl.BlockSpec((1, B_WIDTH), c2),
                  pl.BlockSpec((B_GROUPS, CHUNK, CHUNK), c3),
                  pl.BlockSpec((CHUNK, B_GROUPS), c2),
                  pl.BlockSpec((2, A_WIDTH, D_MODEL), c3),
                  pl.BlockSpec((D_MODEL, D_MODEL), c2)],
        out_specs=pl.BlockSpec((T, D_MODEL), lambda i: (i, 0)),
        out_shape=jax.ShapeDtypeStruct((N, D_MODEL), f32),
        compiler_params=pltpu.CompilerParams(
            dimension_semantics=("arbitrary",),
            vmem_limit_bytes=VMEM_LIMIT_BYTES),
        name="mix",
    )(x2, ya2, norm_g, w_b, ln_g, ln_b, w_sp, b_sp_t, w_br, w_o)


def _attn_side_weights(w):
    pad = jnp.zeros((D_MODEL, LANES - A_HEAD_DIM - IDX_HEADS), w.dtype)
    return jnp.concatenate(
        [w[:, _Q0:_K0], w[:, _QI0:_KI0], w[:, _GA0:_QI0],
         w[:, _K0:_V0], w[:, _KI0:_WI0],
         w[:, _V0:_GA0], w[:, _WI0:_U0], pad], axis=1).astype(bf16)


def kernel(x, norm_g, w_in, q_norm_g, k_norm_g, rel_bias, sgu_ln_g, sgu_ln_b,
           w_spatial, b_spatial, w_branch, w_out):
    B, S, D = x.shape
    depth = w_in.shape[0]
    band = _bias_band(rel_bias)
    head_of = np.arange(A_WIDTH) // A_HEAD_DIM
    bd = jnp.asarray(head_of[:, None] == head_of[None, :], dtype=bf16)
    for l in range(depth):
        w_a = _attn_side_weights(w_in[l])
        w_b = w_in[l][:, _U0:_END].astype(bf16)
        g = norm_g[l][None, :]
        qg = jnp.tile(q_norm_g[l], A_HEADS)[None, :]
        kg = jnp.concatenate([k_norm_g[l], jnp.ones((LANES - A_HEAD_DIM,), f32)])[None, :]
        q, qi, kkt, vext, widx, gate = _proj_a(x, g, w_a, qg, kg, bd)
        ya = _attn(q, qi, kkt, vext, widx, gate, band)
        x2 = _mix(x.reshape(B * S, D), ya.reshape(B * S, A_WIDTH), g, w_b,
                  sgu_ln_g[l][None, :], sgu_ln_b[l][None, :], w_spatial[l],
                  b_spatial[l].T, w_branch[l].astype(bf16), w_out[l].astype(bf16))
        x = x2.reshape(B, S, D)
    return x
```

```python
import functools
import math

import numpy as np
import jax
import jax.numpy as jnp
from jax import lax
from jax.experimental import pallas as pl
from jax.experimental.pallas import tpu as pltpu

D_MODEL = 1024
A_HEADS = 8
A_HEAD_DIM = 64
A_WIDTH = A_HEADS * A_HEAD_DIM
IDX_HEADS = 8
IDX_DIM = 64
TOPK_MAX = 256
B_GROUPS = 4
B_GROUP_DIM = 128
B_WIDTH = B_GROUPS * B_GROUP_DIM
CHUNK = 128
REL_BUCKETS = 32
REL_MAX_DIST = 128
EPS = 1e-6
LOG2E = math.log2(math.e)

LANES = 128
VMEM_LIMIT_BYTES = 56 * 1024 * 1024

TOKEN_TILE = 512
Q_TILE = 256
HALVINGS_PER_HEAD = 3

_SPLITS = (A_WIDTH, A_HEAD_DIM, A_HEAD_DIM, A_WIDTH, IDX_HEADS * IDX_DIM, IDX_DIM,
           IDX_HEADS, B_WIDTH, B_WIDTH, B_WIDTH, D_MODEL, D_MODEL)
_OFFS = np.concatenate([[0], np.cumsum(_SPLITS)])
(_Q0, _K0, _V0, _GA0, _QI0, _KI0, _WI0, _U0, _VB0, _GB0, _MA0, _MB0, _END) = [int(o) for o in _OFFS]

f32 = jnp.float32
bf16 = jnp.bfloat16


def _t5_bucket_np(rel):
    max_exact = REL_BUCKETS // 2
    nf = np.maximum(rel, 1).astype(np.float32)
    large = max_exact + (np.log(nf / np.float32(max_exact))
                         / np.float32(math.log(REL_MAX_DIST / max_exact))
                         * np.float32(REL_BUCKETS - max_exact)).astype(np.int32)
    large = np.minimum(large, REL_BUCKETS - 1)
    return np.where(rel < max_exact, rel, large).astype(np.int32)


def _near_bucket_map():
    tq = np.arange(Q_TILE)[:, None]
    tk = np.arange(2 * Q_TILE)[None, :]
    rel = Q_TILE + tq - tk
    return _t5_bucket_np(np.maximum(rel, 0))


def _rms(x, g):
    ms = jnp.mean(x * x, axis=-1, keepdims=True)
    return x * lax.rsqrt(ms + EPS) * g


def _bias_band_kernel(rb_ref, bmap_ref, out_ref):
    h = pl.program_id(0)
    bmap = bmap_ref[...]
    acc = jnp.zeros(bmap.shape, f32)
    for b in range(REL_BUCKETS):
        acc = jnp.where(bmap == b, rb_ref[b, h], acc)
    out_ref[0] = (acc - rb_ref[REL_BUCKETS - 1, h]) * LOG2E


def _bias_band(rel_bias):
    bmap = jnp.asarray(_near_bucket_map())
    return pl.pallas_call(
        _bias_band_kernel,
        grid=(A_HEADS,),
        in_specs=[pl.BlockSpec(memory_space=pltpu.SMEM),
                  pl.BlockSpec((Q_TILE, 2 * Q_TILE), lambda h: (0, 0))],
        out_specs=pl.BlockSpec((1, Q_TILE, 2 * Q_TILE), lambda h: (h, 0, 0)),
        out_shape=jax.ShapeDtypeStruct((A_HEADS, Q_TILE, 2 * Q_TILE), f32),
        name="bias_band",
    )(rel_bias, bmap)


def _split2(s):
    hi = s.astype(bf16)
    lo = (s - hi.astype(f32)).astype(bf16)
    return hi, lo


def _proj_a_kernel(x_ref, g_ref, w_ref, qg_ref, kg_ref, bd_ref,
                   q_out, qi_out, kkt_out, vext_out, widx_out, gate_out):
    h = _rms(x_ref[0], g_ref[...]).astype(bf16)

    def proj(lo, hi):
        return jnp.dot(h, w_ref[:, lo:hi], preferred_element_type=f32)


    zkk = proj(3 * A_WIDTH, 3 * A_WIDTH + LANES)
    lane = lax.broadcasted_iota(jnp.int32, zkk.shape, 1)
    is_k = lane < A_HEAD_DIM
    ssk = jnp.sum(jnp.where(is_k, zkk * zkk, 0.0), axis=-1, keepdims=True)
    kn = zkk * lax.rsqrt(ssk * (1.0 / A_HEAD_DIM) + EPS) * kg_ref[...]
    kk = jnp.where(is_k, kn, zkk)
    kkt_out[0] = kk.T.astype(bf16)

    zvw = proj(3 * A_WIDTH + LANES, 3 * A_WIDTH + 2 * LANES)
    vext = jnp.where(lane < A_HEAD_DIM, zvw, jnp.where(lane == A_HEAD_DIM, 1.0, 0.0))
    vext_out[0] = vext.astype(bf16)
    widx_out[0] = zvw[:, A_HEAD_DIM:A_HEAD_DIM + IDX_HEADS] * (IDX_HEADS ** -0.5 * IDX_DIM ** -0.5)

    zq = proj(0, A_WIDTH)
    bd = bd_ref[...]
    ssq = sum(jnp.dot(t, bd, preferred_element_type=f32) for t in _split2(zq * zq))
    qn = zq * lax.rsqrt(ssq * (1.0 / A_HEAD_DIM) + EPS) * qg_ref[...]
    qn = (qn * (A_HEAD_DIM ** -0.5 * LOG2E)).astype(bf16)
    zqi = proj(A_WIDTH, 2 * A_WIDTH).astype(bf16)
    for hd in range(A_HEADS):
        sl = slice(hd * A_HEAD_DIM, (hd + 1) * A_HEAD_DIM)
        q_out[0, hd] = qn[:, sl]
        qi_out[0, hd] = zqi[:, sl]

    gate_out[0] = proj(2 * A_WIDTH, 3 * A_WIDTH)


def _proj_a(x, norm_g, w_a, qg, kg, bd):
    B, S, _ = x.shape
    T = TOKEN_TILE
    n_a = w_a.shape[1]
    const = lambda b, i: (0, 0)
    return pl.pallas_call(
        _proj_a_kernel,
        grid=(B, S // T),
        in_specs=[pl.BlockSpec((1, T, D_MODEL), lambda b, i: (b, i, 0)),
                  pl.BlockSpec((1, D_MODEL), const),
                  pl.BlockSpec((D_MODEL, n_a), const),
                  pl.BlockSpec((1, A_WIDTH), const),
                  pl.BlockSpec((1, LANES), const),
                  pl.BlockSpec((A_WIDTH, A_WIDTH), const)],
        out_specs=[pl.BlockSpec((1, A_HEADS, T, A_HEAD_DIM), lambda b, i: (b, 0, i, 0)),
                   pl.BlockSpec((1, IDX_HEADS, T, IDX_DIM), lambda b, i: (b, 0, i, 0)),
                   pl.BlockSpec((1, LANES, T), lambda b, i: (b, 0, i)),
                   pl.BlockSpec((1, T, LANES), lambda b, i: (b, i, 0)),
                   pl.BlockSpec((1, T, IDX_HEADS), lambda b, i: (b, i, 0)),
                   pl.BlockSpec((1, T, A_WIDTH), lambda b, i: (b, i, 0))],
        out_shape=[jax.ShapeDtypeStruct((B, A_HEADS, S, A_HEAD_DIM), bf16),
                   jax.ShapeDtypeStruct((B, IDX_HEADS, S, IDX_DIM), bf16),
                   jax.ShapeDtypeStruct((B, LANES, S), bf16),
                   jax.ShapeDtypeStruct((B, S, LANES), bf16),
                   jax.ShapeDtypeStruct((B, S, IDX_HEADS), f32),
                   jax.ShapeDtypeStruct((B, S, A_WIDTH), f32)],
        compiler_params=pltpu.CompilerParams(
            dimension_semantics=("arbitrary", "arbitrary"),
            vmem_limit_bytes=VMEM_LIMIT_BYTES),
        name="proj_a",
    )(x, norm_g, w_a, qg, kg, bd)


def _count(pred):
    return jnp.sum(jnp.where(pred, 1.0, 0.0), axis=1, keepdims=True)


def _smallest_at_least(s, lo):
    a = jnp.min(jnp.where(s >= lo, s, jnp.inf), axis=1, keepdims=True)
    return a, _count(s > a)


def _tile_scores(c, qi_ref, kkt_ref, widx_ref, sc_ref):
    nk = (c + 1) * Q_TILE
    S = sc_ref.shape[1]
    k_top = float(TOPK_MAX)
    row = lax.broadcasted_iota(jnp.int32, (Q_TILE, nk), 0) + c * Q_TILE
    col = lax.broadcasted_iota(jnp.int32, (Q_TILE, nk), 1)
    causal = col <= row
    kit = kkt_ref[0, A_HEAD_DIM:2 * A_HEAD_DIM, 0:nk]
    widx = widx_ref[0]
    if nk < S:
        sc_ref[:, nk:S] = jnp.full((Q_TILE, S - nk), -jnp.inf, f32)

    s = jnp.dot(qi_ref[0].reshape(IDX_HEADS * Q_TILE, IDX_DIM), kit,
                preferred_element_type=f32)
    sc = None
    for hd in range(IDX_HEADS):
        term = widx[:, hd:hd + 1] * jnp.maximum(s[hd * Q_TILE:(hd + 1) * Q_TILE], 0.0)
        sc = term if sc is None else sc + term
    sc = jnp.where(causal, sc, -jnp.inf)
    sc_ref[:, 0:nk] = sc
    lo0 = jnp.min(jnp.where(causal, sc, jnp.inf), axis=1, keepdims=True)
    hi0 = jnp.max(sc, axis=1, keepdims=True)
    clo0 = (lax.broadcasted_iota(jnp.int32, (Q_TILE, 1), 0) + (c * Q_TILE + 1)).astype(f32)

    def halve(st):
        lo, hi, clo = st
        mid = lo + (hi - lo) * 0.5
        cnt = _count(sc_ref[:, 0:nk] >= mid)
        ge = cnt >= k_top
        return jnp.where(ge, mid, lo), jnp.where(ge, hi, mid), jnp.where(ge, cnt, clo)

    return (lo0, hi0, clo0), halve


def _finish_selection(sc_ref, st_ref, cut_ref, upre_ref, ltri_ref):
    S = sc_ref.shape[1]
    k_top = float(TOPK_MAX)

    def unfinished(st):
        return jnp.max(st[2]) >= k_top

    def refine(st):
        cge, a, cgt = st
        s = sc_ref[...]
        nxt = jnp.min(jnp.where(s > a, s, jnp.inf), axis=1, keepdims=True)
        todo = cgt >= k_top
        cge = jnp.where(todo, cgt, cge)
        a, cgt = _smallest_at_least(s, jnp.where(todo, nxt, a))
        return cge, a, cgt

    cge, thr, cgt = lax.while_loop(unfinished, refine, (st_ref[2], st_ref[0], st_ref[1]))
    st_ref[0] = thr

    cut_ref[...] = jnp.full(cut_ref.shape, S, jnp.int32)

    @pl.when(jnp.max(cge) > k_top)
    def _():
        need = k_top - cgt
        eqf = jnp.where(sc_ref[...] == thr, 1.0, 0.0)
        n_grp = S // LANES
        before = jnp.dot(eqf.astype(bf16), upre_ref[...],
                         preferred_element_type=f32)
        lane = lax.broadcasted_iota(jnp.int32, (Q_TILE, LANES), 1)
        jstar = _count((lane < n_grp) & (before < need)) - 1.0
        base = jnp.sum(jnp.where(lane.astype(f32) == jstar, before, 0.0), axis=1, keepdims=True)
        grp = jnp.zeros((Q_TILE, LANES), f32)
        for j in range(n_grp):
            grp = jnp.where(jstar == float(j), eqf[:, j * LANES:(j + 1) * LANES], grp)
        upto = jnp.dot(grp.astype(bf16), ltri_ref[...],
                       preferred_element_type=f32)
        lstar = _count(upto < need - base)
        cut_ref[...] = (jstar * LANES + lstar).astype(jnp.int32)


def _tile_heads(c, q_ref, kkt_ref, vext_ref, gate_ref, band_ref, y_out, neg_ref, yh_ref, lg_ref,
                side_work=None, side_state=0):
    nk = (c + 1) * Q_TILE
    kt = lambda: kkt_ref[0, 0:A_HEAD_DIM, 0:nk]
    vext = lambda: vext_ref[0, 0:nk, :]
    near_w = min(nk, 2 * Q_TILE)

    def masked_logits(hd):
        logit = jnp.dot(q_ref[0, hd], kt(), preferred_element_type=f32)
        near = logit[:, nk - near_w:] + (band_ref[hd][:, 2 * Q_TILE - near_w:] + neg_ref[:, nk - near_w:nk])
        if nk > near_w:
            far = logit[:, :nk - near_w] + neg_ref[:, 0:nk - near_w]
            logit = jnp.concatenate([far, near], axis=1)
        else:
            logit = near
        return logit, jnp.max(logit, axis=1, keepdims=True)

    def weighted_values(hd, m):
        p = jnp.exp2(lg_ref[:, 0:nk] - m).astype(bf16)
        yh_ref[hd] = jnp.dot(p, vext(), preferred_element_type=f32)

    def head(hd, carry):
        m_prev, side = carry
        p = jnp.exp2(lg_ref[:, 0:nk] - m_prev).astype(bf16)
        logit, m = masked_logits(hd)
        lg_ref[:, 0:nk] = logit
        yh_ref[hd - 1] = jnp.dot(p, vext(), preferred_element_type=f32)
        if side_work is not None:
            side = side_work(side)
        return m, side

    logit0, m0 = masked_logits(0)
    lg_ref[:, 0:nk] = logit0
    m_last, side_state = lax.fori_loop(1, A_HEADS, head, (m0, side_state))
    weighted_values(A_HEADS - 1, m_last)
    ys = []
    for hd in range(A_HEADS):
        pv = yh_ref[hd]
        ys.append(pv[:, 0:A_HEAD_DIM] / pv[:, A_HEAD_DIM:A_HEAD_DIM + 1])
    y = jnp.concatenate(ys, axis=1)
    g = gate_ref[0]
    y_out[0] = (y * (g * jax.nn.sigmoid(g))).astype(bf16)
    return side_state


def _tile_step(c, n_tiles, q_ref, qin_ref, kkt_ref, vext_ref, widxn_ref, gate_ref, band_ref, y_out,
               sc_ref, neg_ref, st_ref, yh_ref, lg_ref):
    if c == 0:
        row = lax.broadcasted_iota(jnp.int32, (Q_TILE, Q_TILE), 0)
        col = lax.broadcasted_iota(jnp.int32, (Q_TILE, Q_TILE), 1)
        neg_ref[:, 0:Q_TILE] = jnp.where(col <= row, 0.0, -jnp.inf)
    heads = functools.partial(_tile_heads, c, q_ref, kkt_ref, vext_ref, gate_ref, band_ref, y_out,
                              neg_ref, yh_ref, lg_ref)
    if c + 1 == n_tiles:
        heads()
        return
    start, halve = _tile_scores(c + 1, qin_ref, kkt_ref, widxn_ref, sc_ref)

    def halvings(st):
        for _ in range(HALVINGS_PER_HEAD):
            st = halve(st)
        return st

    lo, _, clo = heads(side_work=halvings, side_state=start)
    a, cgt = _smallest_at_least(sc_ref[:, 0:(c + 2) * Q_TILE], lo)
    st_ref[0] = a
    st_ref[1] = cgt
    st_ref[2] = clo


def _tile_mask(c, sc_ref, neg_ref, st_ref, cut_ref):
    nk = (c + 1) * Q_TILE
    col = lax.broadcasted_iota(jnp.int32, (Q_TILE, nk), 1)
    s = sc_ref[:, 0:nk]
    thr = st_ref[0]
    keep = (s > thr) | ((s == thr) & (col <= cut_ref[...]))
    neg_ref[:, 0:nk] = jnp.where(keep, 0.0, -jnp.inf)


def _attn_kernel(q_ref, qin_ref, kkt_ref, vext_ref, widxn_ref, gate_ref, band_ref, upre_ref, ltri_ref,
                 y_out, sc_ref, neg_ref, st_ref, cut_ref, yh_ref, lg_ref):
    i = pl.program_id(1)
    n_tiles = kkt_ref.shape[2] // Q_TILE
    for c in range(n_tiles):
        @pl.when(i == c)
        def _(c=c):
            _tile_step(c, n_tiles, q_ref, qin_ref, kkt_ref, vext_ref, widxn_ref, gate_ref, band_ref,
                       y_out, sc_ref, neg_ref, st_ref, yh_ref, lg_ref)

    @pl.when(i + 1 < n_tiles)
    def _():
        _finish_selection(sc_ref, st_ref, cut_ref, upre_ref, ltri_ref)

    for c in range(1, n_tiles):
        @pl.when(i + 1 == c)
        def _(c=c):
            _tile_mask(c, sc_ref, neg_ref, st_ref, cut_ref)


def _attn(q, qi, kkt, vext, widx, gate, band):
    B, _, S, _ = q.shape
    key_group = np.arange(S) // LANES
    upre = jnp.asarray(key_group[:, None] < np.arange(LANES)[None, :], dtype=bf16)
    ltri = jnp.asarray(np.arange(LANES)[:, None] <= np.arange(LANES)[None, :], dtype=bf16)
    n_tiles = S // Q_TILE
    nxt = lambda i: jnp.minimum(i + 1, n_tiles - 1)
    return pl.pallas_call(
        _attn_kernel,
        grid=(B, n_tiles),
        in_specs=[pl.BlockSpec((1, A_HEADS, Q_TILE, A_HEAD_DIM), lambda b, i: (b, 0, i, 0)),
                  pl.BlockSpec((1, IDX_HEADS, Q_TILE, IDX_DIM), lambda b, i: (b, 0, nxt(i), 0)),
                  pl.BlockSpec((1, LANES, S), lambda b, i: (b, 0, 0)),
                  pl.BlockSpec((1, S, LANES), lambda b, i: (b, 0, 0)),
                  pl.BlockSpec((1, Q_TILE, IDX_HEADS), lambda b, i: (b, nxt(i), 0)),
                  pl.BlockSpec((1, Q_TILE, A_WIDTH), lambda b, i: (b, i, 0)),
                  pl.BlockSpec((A_HEADS, Q_TILE, 2 * Q_TILE), lambda b, i: (0, 0, 0)),
                  pl.BlockSpec((S, LANES), lambda b, i: (0, 0)),
                  pl.BlockSpec((LANES, LANES), lambda b, i: (0, 0))],
        out_specs=pl.BlockSpec((1, Q_TILE, A_WIDTH), lambda b, i: (b, i, 0)),
        out_shape=jax.ShapeDtypeStruct((B, S, A_WIDTH), bf16),
        scratch_shapes=[pltpu.VMEM((Q_TILE, S), f32),
                        pltpu.VMEM((Q_TILE, S), f32),
                        pltpu.VMEM((3, Q_TILE, 1), f32),
                        pltpu.VMEM((Q_TILE, 1), jnp.int32),
                        pltpu.VMEM((A_HEADS, Q_TILE, LANES), f32),
                        pltpu.VMEM((Q_TILE, S), f32)],
        compiler_params=pltpu.CompilerParams(
            dimension_semantics=("arbitrary", "arbitrary"),
            vmem_limit_bytes=VMEM_LIMIT_BYTES),
        name="attn",
    )(q, qi, kkt, vext, widx, gate, band, upre, ltri)


def _mix_kernel(x_ref, ya_ref, g_ref, w_ref, lng_ref, lnb_ref, ws_ref, bst_ref, wbr_ref, wo_ref,
                out_ref):
    x = x_ref[...]
    h = _rms(x, g_ref[...]).astype(bf16)
    T = x.shape[0]

    def proj(lo, hi):
        return jnp.dot(h, w_ref[:, lo:hi], preferred_element_type=f32)

    u = jax.nn.gelu(proj(0, B_WIDTH))
    vb = jax.nn.gelu(proj(B_WIDTH, 2 * B_WIDTH))
    mu = jnp.mean(vb, axis=-1, keepdims=True)
    var = jnp.mean(jnp.square(vb - mu), axis=-1, keepdims=True)
    vln = ((vb - mu) * lax.rsqrt(var + EPS) * lng_ref[...] + lnb_ref[...]).astype(bf16)

    n_ch = T // CHUNK
    tril = (lax.broadcasted_iota(jnp.int32, (CHUNK, CHUNK), 1)
            <= lax.broadcasted_iota(jnp.int32, (CHUNK, CHUNK), 0))
    bst = bst_ref[...]
    per_group = []
    for gi in range(B_GROUPS):
        wg = jnp.where(tril, ws_ref[gi], 0.0).astype(bf16)
        cols = slice(gi * B_GROUP_DIM, (gi + 1) * B_GROUP_DIM)
        vg = jnp.concatenate([vln[ci * CHUNK:(ci + 1) * CHUNK, cols] for ci in range(n_ch)], axis=1)
        sg = jnp.dot(wg, vg, preferred_element_type=f32) + bst[:, gi:gi + 1]
        per_group.append(sg)
    s = jnp.concatenate(
        [jnp.concatenate([per_group[gi][:, ci * B_GROUP_DIM:(ci + 1) * B_GROUP_DIM]
                          for gi in range(B_GROUPS)], axis=1)
         for ci in range(n_ch)], axis=0)

    gb = proj(2 * B_WIDTH, 3 * B_WIDTH)
    yb = (u * s * (gb * jax.nn.sigmoid(gb))).astype(bf16)
    yd_a = jnp.dot(ya_ref[...], wbr_ref[0], preferred_element_type=f32)
    yd_b = jnp.dot(yb, wbr_ref[1], preferred_element_type=f32)
    ma = proj(3 * B_WIDTH, 3 * B_WIDTH + D_MODEL)
    mb = proj(3 * B_WIDTH + D_MODEL, 3 * B_WIDTH + 2 * D_MODEL)
    merged = jax.nn.sigmoid(ma) * yd_a + jax.nn.sigmoid(mb) * yd_b
    out_ref[...] = x + jnp.dot(merged.astype(bf16), wo_ref[...], preferred_element_type=f32)


def _mix(x2, ya2, norm_g, w_b, ln_g, ln_b, w_sp, b_sp_t, w_br, w_o):
    N = x2.shape[0]
    T = TOKEN_TILE
    n_b = w_b.shape[1]
    c2 = lambda i: (0, 0)
    c3 = lambda i: (0, 0, 0)
    return pl.pallas_call(
        _mix_kernel,
        grid=(N // T,),
        in_specs=[pl.BlockSpec((T, D_MODEL), lambda i: (i, 0)),
                  pl.BlockSpec((T, A_WIDTH), lambda i: (i, 0)),
                  pl.BlockSpec((1, D_MODEL), c2),
                  pl.BlockSpec((D_MODEL, n_b), c2),
                  pl.BlockSpec((1, B_WIDTH), c2),
                  pl.BlockSpec((1, B_WIDTH), c2),
                  pl.BlockSpec((B_GROUPS, CHUNK, CHUNK), c3),
                  pl.BlockSpec((CHUNK, B_GROUPS), c2),
                  pl.BlockSpec((2, A_WIDTH, D_MODEL), c3),
                  pl.BlockSpec((D_MODEL, D_MODEL), c2)],
        out_specs=pl.BlockSpec((T, D_MODEL), lambda i: (i, 0)),
        out_shape=jax.ShapeDtypeStruct((N, D_MODEL), f32),
        compiler_params=pltpu.CompilerParams(
            dimension_semantics=("arbitrary",),
            vmem_limit_bytes=VMEM_LIMIT_BYTES),
        name="mix",
    )(x2, ya2, norm_g, w_b, ln_g, ln_b, w_sp, b_sp_t, w_br, w_o)


def _attn_side_weights(w):
    pad = jnp.zeros((D_MODEL, LANES - A_HEAD_DIM - IDX_HEADS), w.dtype)
    return jnp.concatenate(
        [w[:, _Q0:_K0], w[:, _QI0:_KI0], w[:, _GA0:_QI0],
         w[:, _K0:_V0], w[:, _KI0:_WI0],
         w[:, _V0:_GA0], w[:, _WI0:_U0], pad], axis=1).astype(bf16)


def kernel(x, norm_g, w_in, q_norm_g, k_norm_g, rel_bias, sgu_ln_g, sgu_ln_b,
           w_spatial, b_spatial, w_branch, w_out):
    B, S, D = x.shape
    depth = w_in.shape[0]
    band = _bias_band(rel_bias)
    head_of = np.arange(A_WIDTH) // A_HEAD_DIM
    bd = jnp.asarray(head_of[:, None] == head_of[None, :], dtype=bf16)
    for l in range(depth):
        w_a = _attn_side_weights(w_in[l])
        w_b = w_in[l][:, _U0:_END].astype(bf16)
        g = norm_g[l][None, :]
        qg = jnp.tile(q_norm_g[l], A_HEADS)[None, :]
        kg = jnp.concatenate([k_norm_g[l], jnp.ones((LANES - A_HEAD_DIM,), f32)])[None, :]
        q, qi, kkt, vext, widx, gate = _proj_a(x, g, w_a, qg, kg, bd)
        ya = _attn(q, qi, kkt, vext, widx, gate, band)
        x2 = _mix(x.reshape(B * S, D), ya.reshape(B * S, A_WIDTH), g, w_b,
                  sgu_ln_g[l][None, :], sgu_ln_b[l][None, :], w_spatial[l],
                  b_spatial[l].T, w_branch[l].astype(bf16), w_out[l].astype(bf16))
        x = x2.reshape(B, S, D)
    return x
```
